```python
import math
import jax, jax.numpy as jnp
from jax import lax
import numpy as np

D_MODEL = 4096
BATCH = 4
SEQ = 2048
DEPTH = 2

N_EVEN = (DEPTH + 1) // 2
N_ODD = DEPTH // 2
EPS = 1e-6
ROPE_THETA = 500000.0
HEAD_DIM = 128
ROT_DIM = HEAD_DIM // 4

S5_WIDTH = D_MODEL // 4
S5_GROUP = 16
S5_GROUPS = S5_WIDTH // S5_GROUP
S5_STATE = 64

DIL_PATTERNS = ((128, 1), (512, 4), (2048, 16))
N_DIL = len(DIL_PATTERNS)
DIL_HEADS = D_MODEL // 512
DIL_WIDTH = DIL_HEADS * HEAD_DIM

EVEN_IN = 2 * S5_WIDTH + 3 * N_DIL * DIL_WIDTH + DIL_WIDTH
EVEN_MIX = S5_WIDTH + DIL_WIDTH

MLA_HEADS = D_MODEL // HEAD_DIM
MLA_NOPE = HEAD_DIM - ROT_DIM
MLA_V = HEAD_DIM
Q_RANK = 1536
KV_RANK = 512
IDX_HEADS = D_MODEL // 128
IDX_DIM = 128
IDX_TOPK_MAX = 256
Q_BLOCK = 128
C_WIDTH = MLA_HEADS * MLA_V
ODD_IN = Q_RANK + KV_RANK + ROT_DIM + IDX_DIM + IDX_HEADS + C_WIDTH

kernel_name = 'hybrid_s5_dilated_dsa_trunk'


def rmsnorm(x, g):
    xf = x.astype(jnp.float32)
    y = xf * lax.rsqrt(jnp.mean(xf * xf, axis=-1, keepdims=True) + EPS)
    return (y * g.astype(jnp.float32)).astype(x.dtype)


def rope_tables(L):
    inv = ROPE_THETA ** (-jnp.arange(0, ROT_DIM, 2, dtype=jnp.float32) / ROT_DIM)
    ang = jnp.arange(L, dtype=jnp.float32)[:, None] * inv[None, :]
    return jnp.cos(ang), jnp.sin(ang)


def apply_rope(x, cos, sin):
    half = ROT_DIM // 2
    c = cos[None, :, None, :].astype(x.dtype)
    s = sin[None, :, None, :].astype(x.dtype)
    x1 = x[..., :half]
    x2 = x[..., half:ROT_DIM]
    return jnp.concatenate([x1 * c - x2 * s, x2 * c + x1 * s, x[..., ROT_DIM:]], axis=-1)


def _complex_affine_combine(e1, e2):
    a1r, a1i, b1r, b1i = e1
    a2r, a2i, b2r, b2i = e2
    ar = a2r * a1r - a2i * a1i
    ai = a2r * a1i + a2i * a1r
    br = a2r * b1r - a2i * b1i + b2r
    bi = a2r * b1i + a2i * b1r + b2i
    return ar, ai, br, bi


def s5_ssm(u, lam_re, lam_im, log_step, b_re, b_im, c_re, c_im, d_skip):
    bsz, L, _ = u.shape
    uf = u.astype(jnp.float32).reshape(bsz, L, S5_GROUPS, S5_GROUP)
    lr = jnp.minimum(lam_re.astype(jnp.float32), -1e-4)
    li = lam_im.astype(jnp.float32)
    dt = jnp.exp(log_step.astype(jnp.float32))[:, None]
    mag = jnp.exp(lr * dt)
    ar = mag * jnp.cos(li * dt)
    ai = mag * jnp.sin(li * dt)
    den = lr * lr + li * li
    nr, ni = ar - 1.0, ai
    cr = (nr * lr + ni * li) / den
    ci = (ni * lr - nr * li) / den
    br32, bi32 = b_re.astype(jnp.float32), b_im.astype(jnp.float32)
    bbr = cr[..., None] * br32 - ci[..., None] * bi32
    bbi = cr[..., None] * bi32 + ci[..., None] * br32
    bu_r = jnp.einsum('blgj,gpj->blgp', uf, bbr)
    bu_i = jnp.einsum('blgj,gpj->blgp', uf, bbi)
    a_r = jnp.broadcast_to(ar, bu_r.shape)
    a_i = jnp.broadcast_to(ai, bu_i.shape)
    _, _, s_r, s_i = lax.associative_scan(_complex_affine_combine, (a_r, a_i, bu_r, bu_i), axis=1)
    y = (jnp.einsum('gjp,blgp->blgj', c_re.astype(jnp.float32), s_r)
         - jnp.einsum('gjp,blgp->blgj', c_im.astype(jnp.float32), s_i))
    y = y.reshape(bsz, L, S5_WIDTH) + d_skip.astype(jnp.float32) * uf.reshape(bsz, L, S5_WIDTH)
    return y.astype(u.dtype)


def dilated_window_attention(q, k, v, dil, sub_window):
    bsz, L, H, D = q.shape
    n = L // dil
    qb = min(sub_window, n)
    nb = -(-n // qb)
    n_pad = nb * qb

    def to_sub(t):
        t = t.reshape(bsz, n, dil, H, D).transpose(0, 2, 3, 1, 4)
        return jnp.pad(t, ((0, 0), (0, 0), (0, 0), (0, n_pad - n), (0, 0)))

    def prev_cur(t):
        cur = t.reshape(bsz, dil, H, nb, qb, D)
        prev = jnp.pad(t, ((0, 0), (0, 0), (0, 0), (qb, 0), (0, 0)))[:, :, :, :n_pad]
        prev = prev.reshape(bsz, dil, H, nb, qb, D)
        return jnp.concatenate([prev, cur], axis=-2)

    qs = to_sub(q).reshape(bsz, dil, H, nb, qb, D)
    kb = prev_cur(to_sub(k))
    vb = prev_cur(to_sub(v))
    s = jnp.einsum('brhnqe,brhnke->brhnqk', qs, kb).astype(jnp.float32) * (HEAD_DIM ** -0.5)
    qi = jnp.arange(qb)[:, None]
    kj = jnp.arange(2 * qb)[None, :]
    dist = qi + qb - kj
    blk = jnp.arange(nb)[:, None, None]
    mask = (dist >= 0) & (dist <= sub_window) & ((blk > 0) | (kj >= qb))
    s = jnp.where(mask, s, -jnp.inf)
    lse = jax.nn.logsumexp(s, axis=-1)
    p = jnp.exp(s - lse[..., None]).astype(v.dtype)
    o = jnp.einsum('brhnqk,brhnke->brhnqe', p, vb)
    o = o.reshape(bsz, dil, H, n_pad, D)[:, :, :, :n]
    o = o.transpose(0, 3, 1, 2, 4).reshape(bsz, L, H, D)
    lse = lse.reshape(bsz, dil, H, n_pad)[..., :n].transpose(0, 3, 1, 2).reshape(bsz, L, H)
    return o, lse


def even_mixer(h, w_in, lam_re, lam_im, log_step, b_re, b_im, c_re, c_im, d_skip, glu_w, glu_b, w_out):
    bsz, L, _ = h.shape
    proj = h @ w_in
    u, z_a, qkv, z_b = jnp.split(
        proj, [S5_WIDTH, 2 * S5_WIDTH, 2 * S5_WIDTH + 3 * N_DIL * DIL_WIDTH], axis=-1)
    y = jax.nn.gelu(s5_ssm(u, lam_re, lam_im, log_step, b_re, b_im, c_re, c_im, d_skip))
    y = y * jax.nn.sigmoid(y @ glu_w + glu_b)
    a_out = y * jax.nn.silu(z_a)
    qkv = qkv.reshape(bsz, L, N_DIL, 3, DIL_HEADS, HEAD_DIM)
    cos, sin = rope_tables(L)
    outs, lses = [], []
    for g, (window, dil) in enumerate(DIL_PATTERNS):
        q = apply_rope(qkv[:, :, g, 0], cos, sin)
        k = apply_rope(qkv[:, :, g, 1], cos, sin)
        o, lse = dilated_window_attention(q, k, qkv[:, :, g, 2], dil, window // dil)
        outs.append(o)
        lses.append(lse)
    alpha = jax.nn.softmax(jnp.stack(lses, axis=0), axis=0).astype(h.dtype)
    o = jnp.sum(alpha[..., None] * jnp.stack(outs, axis=0), axis=0).reshape(bsz, L, DIL_WIDTH)
    b_out = o * jax.nn.silu(z_b)
    return jnp.concatenate([a_out, b_out], axis=-1) @ w_out


def odd_mixer(h, w_in, q_norm, kv_norm, k_idx_norm, w_uq, w_uk, w_uv, w_iq, w_out):
    bsz, L, _ = h.shape
    proj = h @ w_in
    c_q, c_kv, k_r, k_i, w_i, gate = jnp.split(
        proj, np.cumsum([Q_RANK, KV_RANK, ROT_DIM, IDX_DIM, IDX_HEADS]).tolist(), axis=-1)
    c_q = rmsnorm(c_q, q_norm)
    c_kv = rmsnorm(c_kv, kv_norm)
    cos, sin = rope_tables(L)
    q = apply_rope((c_q @ w_uq).reshape(bsz, L, MLA_HEADS, HEAD_DIM), cos, sin)
    q_rope, q_nope = q[..., :ROT_DIM], q[..., ROT_DIM:]
    k_rope = apply_rope(k_r[:, :, None, :], cos, sin)[:, :, 0]
    q_idx = apply_rope((c_q @ w_iq).reshape(bsz, L, IDX_HEADS, IDX_DIM), cos, sin)
    k_idx = apply_rope(rmsnorm(k_i, k_idx_norm)[:, :, None, :], cos, sin)[:, :, 0]
    topk = min(IDX_TOPK_MAX, L // 4)
    nb = L // Q_BLOCK
    kpos = jnp.arange(L)

    def to_blocks(t):
        return t.reshape(bsz, nb, Q_BLOCK, *t.shape[2:]).swapaxes(0, 1)

    def block_fn(args):
        start, qn, qr, qi, wi = args
        qpos = start + jnp.arange(Q_BLOCK)
        si = jnp.einsum('bqhd,bkd->bqhk', qi, k_idx).astype(jnp.float32) * (IDX_DIM ** -0.5)
        si = jnp.einsum('bqhk,bqh->bqk', jax.nn.relu(si),
                        wi.astype(jnp.float32) * (IDX_HEADS ** -0.5))
        si = jnp.where((kpos[None, :] <= qpos[:, None])[None], si, -jnp.inf)
        _, sel = lax.top_k(si, topk)
        valid = sel <= qpos[None, :, None]
        c_sel = jax.vmap(lambda c, i: c[i])(c_kv, sel)
        kr_sel = jax.vmap(lambda c, i: c[i])(k_rope, sel)
        q_lat = jnp.einsum('bqhd,chd->bqhc', qn, w_uk)
        s = (jnp.einsum('bqhc,bqkc->bqhk', q_lat, c_sel)
             + jnp.einsum('bqhr,bqkr->bqhk', qr, kr_sel)).astype(jnp.float32) * (HEAD_DIM ** -0.5)
        s = jnp.where(valid[:, :, None, :], s, -jnp.inf)
        p = jax.nn.softmax(s, axis=-1).astype(c_sel.dtype)
        o_lat = jnp.einsum('bqhk,bqkc->bqhc', p, c_sel)
        return jnp.einsum('bqhc,chd->bqhd', o_lat, w_uv)

    starts = jnp.arange(nb) * Q_BLOCK
    o = lax.map(block_fn, (starts, to_blocks(q_nope), to_blocks(q_rope), to_blocks(q_idx), to_blocks(w_i)))
    o = o.swapaxes(0, 1).reshape(bsz, L, C_WIDTH)
    return (o * jax.nn.silu(gate)) @ w_out


def setup_inputs(seed: int = 0) -> dict:
    key = jax.random.key(seed)
    ks = jax.random.split(key, 26)
    f32 = jnp.float32

    def nrm(k, shape, scale):
        return jax.random.normal(k, shape, f32) * scale

    lam_im_base = jnp.pi * jnp.arange(S5_STATE, dtype=f32)
    return {
        'x': nrm(ks[0], (BATCH, SEQ, D_MODEL), 1.0),
        'even_norm': 1.0 + nrm(ks[1], (N_EVEN, D_MODEL), 0.02),
        'even_w_in': nrm(ks[2], (N_EVEN, D_MODEL, EVEN_IN), D_MODEL ** -0.5),
        's5_lambda_re': -0.5 + nrm(ks[3], (N_EVEN, S5_GROUPS, S5_STATE), 0.01),
        's5_lambda_im': lam_im_base + nrm(ks[4], (N_EVEN, S5_GROUPS, S5_STATE), 0.01),
        's5_log_step': jax.random.uniform(ks[5], (N_EVEN, S5_GROUPS), f32, math.log(1e-3), math.log(1e-1)),
        's5_b_re': nrm(ks[6], (N_EVEN, S5_GROUPS, S5_STATE, S5_GROUP), (2 * S5_GROUP) ** -0.5),
        's5_b_im': nrm(ks[7], (N_EVEN, S5_GROUPS, S5_STATE, S5_GROUP), (2 * S5_GROUP) ** -0.5),
        's5_c_re': nrm(ks[8], (N_EVEN, S5_GROUPS, S5_GROUP, S5_STATE), 0.5),
        's5_c_im': nrm(ks[9], (N_EVEN, S5_GROUPS, S5_GROUP, S5_STATE), 0.5),
        's5_d': nrm(ks[10], (N_EVEN, S5_WIDTH), 1.0),
        's5_glu_w': nrm(ks[11], (N_EVEN, S5_WIDTH, S5_WIDTH), S5_WIDTH ** -0.5),
        's5_glu_b': nrm(ks[12], (N_EVEN, S5_WIDTH), 0.01),
        'even_w_out': nrm(ks[13], (N_EVEN, EVEN_MIX, D_MODEL), EVEN_MIX ** -0.5),
        'odd_norm': 1.0 + nrm(ks[14], (N_ODD, D_MODEL), 0.02),
        'odd_w_in': nrm(ks[15], (N_ODD, D_MODEL, ODD_IN), D_MODEL ** -0.5),
        'mla_q_norm': 1.0 + nrm(ks[16], (N_ODD, Q_RANK), 0.02),
        'mla_kv_norm': 1.0 + nrm(ks[17], (N_ODD, KV_RANK), 0.02),
        'idx_k_norm': 1.0 + nrm(ks[18], (N_ODD, IDX_DIM), 0.02),
        'mla_w_uq': nrm(ks[19], (N_ODD, Q_RANK, MLA_HEADS * HEAD_DIM), Q_RANK ** -0.5),
        'mla_w_uk': nrm(ks[20], (N_ODD, KV_RANK, MLA_HEADS, MLA_NOPE), KV_RANK ** -0.5),
        'mla_w_uv': nrm(ks[21], (N_ODD, KV_RANK, MLA_HEADS, MLA_V), KV_RANK ** -0.5),
        'idx_w_q': nrm(ks[22], (N_ODD, Q_RANK, IDX_HEADS * IDX_DIM), Q_RANK ** -0.5),
        'odd_w_out': nrm(ks[23], (N_ODD, C_WIDTH, D_MODEL), C_WIDTH ** -0.5),
        'final_norm': 1.0 + nrm(ks[24], (D_MODEL,), 0.02),
    }


def reference(x, even_norm, even_w_in, s5_lambda_re, s5_lambda_im, s5_log_step, s5_b_re, s5_b_im,
              s5_c_re, s5_c_im, s5_d, s5_glu_w, s5_glu_b, even_w_out, odd_norm, odd_w_in,
              mla_q_norm, mla_kv_norm, idx_k_norm, mla_w_uq, mla_w_uk, mla_w_uv, idx_w_q,
              odd_w_out, final_norm):
    for layer in range(DEPTH):
        i = layer // 2
        if layer % 2 == 0:
            h = rmsnorm(x, even_norm[i])
            x = x + even_mixer(h, even_w_in[i], s5_lambda_re[i], s5_lambda_im[i], s5_log_step[i],
                               s5_b_re[i], s5_b_im[i], s5_c_re[i], s5_c_im[i], s5_d[i],
                               s5_glu_w[i], s5_glu_b[i], even_w_out[i])
        else:
            h = rmsnorm(x, odd_norm[i])
            x = x + odd_mixer(h, odd_w_in[i], mla_q_norm[i], mla_kv_norm[i], idx_k_norm[i],
                              mla_w_uq[i], mla_w_uk[i], mla_w_uv[i], idx_w_q[i], odd_w_out[i])
    return rmsnorm(x, final_norm)
```

```python
import functools

import jax
import jax.numpy as jnp
from jax import lax
from jax.experimental import pallas as pl
from jax.experimental.pallas import tpu as pltpu

F32 = jnp.float32
BF16 = jnp.bfloat16

D_MODEL = 4096
EPS = 1e-6
ROPE_THETA = 500000.0
HEAD_DIM = 128
ROT_DIM = HEAD_DIM // 4
ROT_HALF = ROT_DIM // 2

S5_WIDTH = D_MODEL // 4
S5_GROUP = 16
S5_GROUPS = S5_WIDTH // S5_GROUP
S5_STATE = 64
S5_BLOCKS = S5_WIDTH // 128
S5_BLOCK_GROUPS = S5_GROUPS // S5_BLOCKS
S5_BLOCK_STATES = S5_BLOCK_GROUPS * S5_STATE

DIL_PATTERNS = ((128, 1), (512, 4), (2048, 16))
N_DIL = len(DIL_PATTERNS)
DIL_HEADS = D_MODEL // 512
DIL_WIDTH = DIL_HEADS * HEAD_DIM
QKV_WIDTH = 3 * N_DIL * DIL_WIDTH

MLA_HEADS = D_MODEL // HEAD_DIM
MLA_NOPE = HEAD_DIM - ROT_DIM
Q_RANK = 1536
KV_RANK = 512
IDX_HEADS = D_MODEL // 128
IDX_DIM = 128
IDX_TOPK_MAX = 256
C_WIDTH = MLA_HEADS * HEAD_DIM
ODD_SMALL = Q_RANK + KV_RANK + 3 * 128
ODD_SMALL_PAD = 2560

LANES = 128
SUBLANES = 8
Q_TILE = 128
MASK_NEG = -1e30
INT_MIN = -(2 ** 31)


def _params(n_grid, vmem_mb):
    return pltpu.CompilerParams(
        dimension_semantics=("arbitrary",) * n_grid,
        vmem_limit_bytes=vmem_mb * 1024 * 1024)


def _rope_tables(L):
    inv = ROPE_THETA ** (-jnp.arange(0, ROT_DIM, 2, dtype=F32) / ROT_DIM)
    ang = jnp.arange(L, dtype=F32)[:, None] * inv[None, :]
    cos, sin = jnp.cos(ang), jnp.sin(ang)
    ones = jnp.ones((L, HEAD_DIM - ROT_DIM), F32)
    zeros = jnp.zeros((L, HEAD_DIM - ROT_DIM), F32)
    zh = jnp.zeros((L, ROT_HALF), F32)
    c = jnp.concatenate([cos, cos, ones], axis=1)
    sa = jnp.concatenate([-sin, zh, zeros], axis=1)
    sb = jnp.concatenate([zh, sin, zeros], axis=1)
    return c, sa, sb


def _rope_head(x, c, sa, sb):
    return (x * c + pltpu.roll(x, HEAD_DIM - ROT_HALF, 1) * sa
            + pltpu.roll(x, ROT_HALF, 1) * sb)


def _rmsnorm_kernel(x_ref, g_ref, o_ref):
    x = x_ref[...]
    ms = jnp.mean(x * x, axis=-1, keepdims=True)
    o_ref[...] = (x * lax.rsqrt(ms + EPS) * g_ref[...]).astype(o_ref.dtype)


def _rmsnorm(x, g, out_dtype, tm=256):
    m, d = x.shape
    return pl.pallas_call(
        _rmsnorm_kernel,
        out_shape=jax.ShapeDtypeStruct((m, d), out_dtype),
        grid=(m // tm,),
        in_specs=[pl.BlockSpec((tm, d), lambda i: (i, 0)),
                  pl.BlockSpec((1, d), lambda i: (0, 0))],
        out_specs=pl.BlockSpec((tm, d), lambda i: (i, 0)),
        compiler_params=_params(1, 32),
        name="rmsnorm",
    )(x, g.reshape(1, d))


def _mm_kernel(*refs, n_a, has_res, epilogue, scale, tiles_per_kind):
    a_refs = refs[:n_a]
    w_refs = refs[n_a:2 * n_a]
    pos = 2 * n_a
    res_ref = None
    if has_res:
        res_ref = refs[pos]
        pos += 1
    extra = refs[pos:-1]
    o_ref = refs[-1]

    acc = None
    for a_ref, w_ref in zip(a_refs, w_refs):
        d = jnp.dot(a_ref[...], w_ref[...].astype(BF16), preferred_element_type=F32)
        acc = d if acc is None else acc + d
    if has_res:
        acc = acc + res_ref[...]
    n_heads = acc.shape[1] // HEAD_DIM

    def roped():
        c, sa, sb = extra[0][...], extra[1][...], extra[2][...]
        cols = []
        for h in range(n_heads):
            y = _rope_head(acc[:, h * HEAD_DIM:(h + 1) * HEAD_DIM], c, sa, sb)
            cols.append(y * scale if scale != 1.0 else y)
        return jnp.concatenate(cols, axis=1)

    if epilogue == "none":
        o_ref[...] = acc.astype(o_ref.dtype)
    elif epilogue == "rope":
        o_ref[...] = roped().astype(o_ref.dtype)
    elif epilogue == "rope_qk":
        kind = (pl.program_id(1) // tiles_per_kind) % 3

        @pl.when(kind == 2)
        def _():
            o_ref[...] = acc.astype(o_ref.dtype)

        @pl.when(kind != 2)
        def _():
            o_ref[...] = roped().astype(o_ref.dtype)
    elif epilogue == "add_head":
        kr = extra[0][...]
        cols = [acc[:, h * HEAD_DIM:(h + 1) * HEAD_DIM] + kr for h in range(n_heads)]
        o_ref[...] = jnp.concatenate(cols, axis=1).astype(o_ref.dtype)
    else:
        raise ValueError(epilogue)


def _matmul(a_list, w_list, n, out_dtype, *, w_col0=0, w_row_blocks=None, res=None,
            epilogue="none", extra=(), scale=1.0, seq_len=None, tm=1024, tn=512,
            vmem_mb=52, name="matmul"):
    m = a_list[0].shape[0]
    assert m % tm == 0 and n % tn == 0 and w_col0 % tn == 0
    col_off = w_col0 // tn
    n_a = len(a_list)
    if w_row_blocks is None:
        w_row_blocks = [0] * n_a
    in_specs, args = [], []
    for a in a_list:
        k = a.shape[1]
        in_specs.append(pl.BlockSpec((tm, k), lambda i, j: (i, 0)))
        args.append(a)
    for a, w, rb in zip(a_list, w_list, w_row_blocks):
        k = a.shape[1]
        in_specs.append(pl.BlockSpec((k, tn), lambda i, j, rb=rb: (rb, j + col_off)))
        args.append(w)
    if res is not None:
        in_specs.append(pl.BlockSpec((tm, tn), lambda i, j: (i, j)))
        args.append(res)
    tiles_per_seq = None if seq_len is None else seq_len // tm
    for e in extra:
        if e.shape[0] == m:
            in_specs.append(pl.BlockSpec((tm, e.shape[1]), lambda i, j: (i, 0)))
        else:
            assert seq_len is not None and e.shape[0] == seq_len and seq_len % tm == 0
            in_specs.append(pl.BlockSpec((tm, e.shape[1]),
                                         lambda i, j: (i % tiles_per_seq, 0)))
        args.append(e)
    kern = functools.partial(_mm_kernel, n_a=n_a, has_res=res is not None,
                             epilogue=epilogue, scale=scale,
                             tiles_per_kind=DIL_WIDTH // tn)
    return pl.pallas_call(
        kern,
        out_shape=jax.ShapeDtypeStruct((m, n), out_dtype),
        grid=(m // tm, n // tn),
        in_specs=in_specs,
        out_specs=pl.BlockSpec((tm, tn), lambda i, j: (i, j)),
        compiler_params=_params(2, vmem_mb),
        name=name,
    )(*args)


def _s5_discretise(lam_re, lam_im, log_step, b_re, b_im, c_re, c_im):
    lr = jnp.minimum(lam_re.astype(F32), -1e-4)
    li = lam_im.astype(F32)
    dt = jnp.exp(log_step.astype(F32))[:, None]
    mag = jnp.exp(lr * dt)
    ar = mag * jnp.cos(li * dt)
    ai = mag * jnp.sin(li * dt)
    den = lr * lr + li * li
    nr, ni = ar - 1.0, ai
    cr = (nr * lr + ni * li) / den
    ci = (ni * lr - nr * li) / den
    br32, bi32 = b_re.astype(F32), b_im.astype(F32)
    bbr = cr[..., None] * br32 - ci[..., None] * bi32
    bbi = cr[..., None] * bi32 + ci[..., None] * br32

    eye = jnp.eye(S5_BLOCK_GROUPS, dtype=F32)

    def in_blockdiag(t):
        t = t.reshape(S5_BLOCKS, S5_BLOCK_GROUPS, S5_STATE, S5_GROUP)
        t = jnp.einsum("bgpj,gh->bgjhp", t, eye)
        return t.reshape(S5_BLOCKS, 128, S5_BLOCK_STATES)

    def out_blockdiag(t):
        t = t.reshape(S5_BLOCKS, S5_BLOCK_GROUPS, S5_GROUP, S5_STATE)
        t = jnp.einsum("bgjp,gh->bgphj", t, eye)
        return t.reshape(S5_BLOCKS, S5_BLOCK_STATES, 128)

    b_mat = jnp.concatenate([in_blockdiag(bbr), in_blockdiag(bbi)], axis=2).astype(BF16)
    c_mat = jnp.concatenate([out_blockdiag(c_re.astype(F32)),
                             -out_blockdiag(c_im.astype(F32))], axis=1).astype(BF16)

    a_r = ar.reshape(S5_BLOCKS, S5_BLOCK_STATES)
    a_i = ai.reshape(S5_BLOCKS, S5_BLOCK_STATES)
    pr, pi = [a_r], [a_i]
    for _ in range(SUBLANES - 1):
        pr.append(pr[-1] * a_r - pi[-1] * a_i)
        pi.append(pr[-2] * a_i + pi[-1] * a_r)
    pow_re = jnp.stack(pr, axis=1)
    pow_im = jnp.stack(pi, axis=1)
    return b_mat, c_mat, pow_re, pow_im


def _s5_kernel(u_ref, b_ref, c_ref, pre_ref, pim_ref, d_ref, y_ref, s_ref):
    L = u_ref.shape[1]
    ns = S5_BLOCK_STATES
    u = u_ref[0]
    s_ref[...] = jnp.dot(u.astype(BF16), b_ref[0], preferred_element_type=F32)

    row = lax.broadcasted_iota(jnp.int32, (SUBLANES, ns), 0)
    pre = pre_ref[0]
    pim = pim_ref[0]

    def bcast(v, r):
        return jnp.broadcast_to(v[r:r + 1, :], (SUBLANES, ns))

    steps = tuple((d, bcast(pre, d - 1), bcast(pim, d - 1)) for d in (1, 2, 4))

    def block(i, carry):
        cr, ci = carry
        off = pl.multiple_of(i * SUBLANES, SUBLANES)
        xr = s_ref[pl.ds(off, SUBLANES), 0:ns]
        xi = s_ref[pl.ds(off, SUBLANES), ns:2 * ns]
        for d, ar, ai in steps:
            sr = jnp.where(row >= d, pltpu.roll(xr, d, 0), 0.0)
            si = jnp.where(row >= d, pltpu.roll(xi, d, 0), 0.0)
            xr, xi = xr + ar * sr - ai * si, xi + ar * si + ai * sr
        crb = jnp.broadcast_to(cr, (SUBLANES, ns))
        cib = jnp.broadcast_to(ci, (SUBLANES, ns))
        xr, xi = xr + pre * crb - pim * cib, xi + pre * cib + pim * crb
        s_ref[pl.ds(off, SUBLANES), 0:ns] = xr
        s_ref[pl.ds(off, SUBLANES), ns:2 * ns] = xi
        return xr[SUBLANES - 1:SUBLANES, :], xi[SUBLANES - 1:SUBLANES, :]

    zero = jnp.zeros((1, ns), F32)
    lax.fori_loop(0, L // SUBLANES, block, (zero, zero))

    y = jnp.dot(s_ref[...].astype(BF16), c_ref[0], preferred_element_type=F32)
    y_ref[0] = y + d_ref[0] * u


def _s5(proj_a, b_mat, c_mat, pow_re, pow_im, d_skip):
    bsz, L, _ = proj_a.shape
    ns = S5_BLOCK_STATES
    return pl.pallas_call(
        _s5_kernel,
        out_shape=jax.ShapeDtypeStruct((bsz, L, S5_WIDTH), F32),
        grid=(bsz, S5_BLOCKS),
        in_specs=[
            pl.BlockSpec((1, L, 128), lambda b, g: (b, 0, g)),
            pl.BlockSpec((1, 128, 2 * ns), lambda b, g: (g, 0, 0)),
            pl.BlockSpec((1, 2 * ns, 128), lambda b, g: (g, 0, 0)),
            pl.BlockSpec((1, SUBLANES, ns), lambda b, g: (g, 0, 0)),
            pl.BlockSpec((1, SUBLANES, ns), lambda b, g: (g, 0, 0)),
            pl.BlockSpec((1, 1, 128), lambda b, g: (g, 0, 0)),
        ],
        out_specs=pl.BlockSpec((1, L, 128), lambda b, g: (b, 0, g)),
        scratch_shapes=[pltpu.VMEM((L, 2 * ns), F32)],
        compiler_params=_params(2, 40),
        name="s5_scan",
    )(proj_a, b_mat, c_mat, pow_re, pow_im, d_skip.reshape(S5_BLOCKS, 1, 128))


def _glu_kernel(y_ref, z_ref, w_ref, b_ref, o_ref):
    g = jax.nn.gelu(y_ref[...])
    t = jnp.dot(g.astype(BF16), w_ref[...].astype(BF16), preferred_element_type=F32) + b_ref[...]
    z = z_ref[...]
    gate = z / (1.0 + jnp.exp(-z))
    o_ref[...] = (g / (1.0 + jnp.exp(-t)) * gate).astype(o_ref.dtype)


def _glu(y, proj_a, glu_w, glu_b, tm=512):
    m = y.shape[0]
    return pl.pallas_call(
        _glu_kernel,
        out_shape=jax.ShapeDtypeStruct((m, S5_WIDTH), BF16),
        grid=(m // tm,),
        in_specs=[pl.BlockSpec((tm, S5_WIDTH), lambda i: (i, 0)),
                  pl.BlockSpec((tm, S5_WIDTH), lambda i: (i, 1)),
                  pl.BlockSpec((S5_WIDTH, S5_WIDTH), lambda i: (0, 0)),
                  pl.BlockSpec((1, S5_WIDTH), lambda i: (0, 0))],
        out_specs=pl.BlockSpec((tm, S5_WIDTH), lambda i: (i, 0)),
        compiler_params=_params(1, 40),
        name="s5_glu",
    )(y, proj_a, glu_w, glu_b.reshape(1, S5_WIDTH))


def _dil_kernel(q_ref, k_ref, v_ref, o_ref, lse_ref, *, n, sub_window, nh):
    nb = n // Q_TILE
    cw = Q_TILE if nb == 1 else 2 * Q_TILE

    def qblock(qb, _):
        q0 = pl.multiple_of(qb * Q_TILE, Q_TILE)
        kb = jnp.clip(qb - 1, 0, max(nb - 2, 0))
        k0 = pl.multiple_of(kb * Q_TILE, Q_TILE)
        qpos = q0 + lax.broadcasted_iota(jnp.int32, (Q_TILE, cw), 0)
        kpos = k0 + lax.broadcasted_iota(jnp.int32, (Q_TILE, cw), 1)
        half = sub_window // 2
        valid = jnp.abs(qpos - kpos - half) <= half
        for h in range(nh):
            hs = slice(h * HEAD_DIM, (h + 1) * HEAD_DIM)
            q = q_ref[0, pl.ds(q0, Q_TILE), hs]
            k = k_ref[0, pl.ds(k0, cw), hs]
            v = v_ref[0, pl.ds(k0, cw), hs]
            s = lax.dot_general(q, k, (((1,), (1,)), ((), ())),
                                preferred_element_type=F32) * (HEAD_DIM ** -0.5)
            s = jnp.where(valid, s, MASK_NEG)
            m = jnp.max(s, axis=1, keepdims=True)
            p = jnp.exp(s - m)
            l = jnp.sum(p, axis=1, keepdims=True)
            o = jnp.dot(p.astype(BF16), v, preferred_element_type=F32) / l
            o_ref[0, pl.ds(q0, Q_TILE), hs] = o
            lse_ref[0, pl.ds(q0, Q_TILE), hs] = jnp.broadcast_to(m + jnp.log(l),
                                                                 (Q_TILE, HEAD_DIM))
        return 0

    lax.fori_loop(0, nb, qblock, 0)


def _dilated_attention(qkv, group, window, dil, nh=4):
    bsz, L, _ = qkv.shape
    n = L // dil
    assert n % Q_TILE == 0 and (window // dil) % 2 == 0
    qkv_r = qkv.reshape(bsz, n, dil * QKV_WIDTH)
    w = nh * HEAD_DIM
    splits = DIL_WIDTH // w
    blocks_per_row = QKV_WIDTH // w

    def in_spec(kind):
        return pl.BlockSpec(
            (1, n, w),
            lambda b, r, s: (b, 0, r * blocks_per_row + (group * 3 + kind) * splits + s))

    out_spec = pl.BlockSpec((1, n, w), lambda b, r, s: (b, 0, r * splits + s))
    o, lse = pl.pallas_call(
        functools.partial(_dil_kernel, n=n, sub_window=window // dil, nh=nh),
        out_shape=[jax.ShapeDtypeStruct((bsz, n, dil * DIL_WIDTH), F32)] * 2,
        grid=(bsz, dil, splits),
        in_specs=[in_spec(0), in_spec(1), in_spec(2)],
        out_specs=[out_spec, out_spec],
        compiler_params=_params(3, 40),
        name=f"dilated_attention_{dil}",
    )(qkv_r, qkv_r, qkv_r)
    return o.reshape(bsz * L, DIL_WIDTH), lse.reshape(bsz * L, DIL_WIDTH)


def _dil_merge_kernel(o0, o1, o2, l0, l1, l2, z_ref, out_ref):
    a0, a1, a2 = l0[...], l1[...], l2[...]
    m = jnp.maximum(jnp.maximum(a0, a1), a2)
    e0, e1, e2 = jnp.exp(a0 - m), jnp.exp(a1 - m), jnp.exp(a2 - m)
    o = (e0 * o0[...] + e1 * o1[...] + e2 * o2[...]) / (e0 + e1 + e2)
    z = z_ref[...]
    out_ref[...] = (o * (z / (1.0 + jnp.exp(-z)))).astype(out_ref.dtype)


def _dil_merge(outs, lses, proj_b, tm=512):
    m = outs[0].shape[0]
    spec = pl.BlockSpec((tm, DIL_WIDTH), lambda i: (i, 0))
    return pl.pallas_call(
        _dil_merge_kernel,
        out_shape=jax.ShapeDtypeStruct((m, DIL_WIDTH), BF16),
        grid=(m // tm,),
        in_specs=[spec] * 7,
        out_specs=spec,
        compiler_params=_params(1, 48),
        name="dilated_merge",
    )(*outs, *lses, proj_b)


def _even_layer(x2d, bsz, L, norm, w_in, lam_re, lam_im, log_step, b_re, b_im, c_re, c_im,
                d_skip, glu_w, glu_b, w_out, tables):
    h = _rmsnorm(x2d, norm, BF16)
    proj_a = _matmul([h], [w_in], 2 * S5_WIDTH, F32, w_col0=0, name="even_in_a")
    qkv = _matmul([h], [w_in], QKV_WIDTH, BF16, w_col0=2 * S5_WIDTH, epilogue="rope_qk",
                  extra=tables, seq_len=L, name="even_in_qkv")
    proj_b = _matmul([h], [w_in], DIL_WIDTH, F32, w_col0=2 * S5_WIDTH + QKV_WIDTH,
                     name="even_in_zb")

    b_mat, c_mat, pow_re, pow_im = _s5_discretise(lam_re, lam_im, log_step, b_re, b_im,
                                                  c_re, c_im)
    y = _s5(proj_a.reshape(bsz, L, 2 * S5_WIDTH), b_mat, c_mat, pow_re, pow_im, d_skip)
    a_out = _glu(y.reshape(bsz * L, S5_WIDTH), proj_a, glu_w, glu_b)

    qkv3 = qkv.reshape(bsz, L, QKV_WIDTH)
    outs, lses = [], []
    for g, (window, dil) in enumerate(DIL_PATTERNS):
        o, lse = _dilated_attention(qkv3, g, window, dil)
        outs.append(o)
        lses.append(lse)
    b_out = _dil_merge(outs, lses, proj_b)

    return _matmul([a_out, b_out], [w_out, w_out], D_MODEL, F32, w_row_blocks=[0, 1],
                   res=x2d, name="even_out")


def _odd_prep_kernel(p_ref, qn_ref, kvn_ref, kin_ref, c_ref, sa_ref, sb_ref,
                     cq_ref, ckv_ref, kr_ref, ki_ref, wi_ref):
    def norm(x, g):
        ms = jnp.mean(x * x, axis=-1, keepdims=True)
        return x * lax.rsqrt(ms + EPS) * g

    c, sa, sb = c_ref[...], sa_ref[...], sb_ref[...]
    o = 0
    cq_ref[...] = norm(p_ref[:, o:o + Q_RANK], qn_ref[...]).astype(cq_ref.dtype)
    o += Q_RANK
    ckv_ref[...] = norm(p_ref[:, o:o + KV_RANK], kvn_ref[...]).astype(ckv_ref.dtype)
    o += KV_RANK
    kr_ref[...] = _rope_head(p_ref[:, o:o + 128], c, sa, sb)
    o += 128
    ki_ref[...] = _rope_head(norm(p_ref[:, o:o + 128], kin_ref[...]), c, sa, sb
                             ).astype(ki_ref.dtype)
    o += 128
    wi_ref[...] = p_ref[:, o:o + 128] * (IDX_HEADS ** -0.5)


def _odd_prep(p_small, q_norm, kv_norm, k_idx_norm, tables, L, tm=256):
    m = p_small.shape[0]
    tps = L // tm
    row = lambda w: pl.BlockSpec((tm, w), lambda i: (i, 0))
    tab = pl.BlockSpec((tm, 128), lambda i: (i % tps, 0))
    vec = lambda w: pl.BlockSpec((1, w), lambda i: (0, 0))
    return pl.pallas_call(
        _odd_prep_kernel,
        out_shape=[jax.ShapeDtypeStruct((m, Q_RANK), BF16),
                   jax.ShapeDtypeStruct((m, KV_RANK), BF16),
                   jax.ShapeDtypeStruct((m, 128), F32),
                   jax.ShapeDtypeStruct((m, 128), BF16),
                   jax.ShapeDtypeStruct((m, 128), F32)],
        grid=(m // tm,),
        in_specs=[row(ODD_SMALL_PAD), vec(Q_RANK), vec(KV_RANK), vec(IDX_DIM), tab, tab, tab],
        out_specs=[row(Q_RANK), row(KV_RANK), row(128), row(128), row(128)],
        compiler_params=_params(1, 32),
        name="odd_prep",
    )(p_small, q_norm.reshape(1, -1), kv_norm.reshape(1, -1), k_idx_norm.reshape(1, -1),
      *tables)


def _float_sort_key(s):
    b = pltpu.bitcast(s, jnp.int32)
    return b ^ ((b >> 31) & 0x7FFFFFFF)


def _indexer_kernel(q_ref, k_ref, w_ref, bias_ref, si_ref, key_ref, jmax_ref, *, L, topk, kc):
    i = pl.program_id(1)
    nch = L // kc
    n_vis = ((i + 1) * Q_TILE + kc - 1) // kc

    si_ref[...] = jnp.full(si_ref.shape, -jnp.inf, F32)

    def chunk(c, _):
        k0 = pl.multiple_of(c * kc, kc)
        kblk = k_ref[0, pl.ds(k0, kc), :]
        acc = jnp.zeros((Q_TILE, kc), F32)
        for h in range(IDX_HEADS):
            qh = q_ref[0, :, h * IDX_DIM:(h + 1) * IDX_DIM]
            s = lax.dot_general(qh, kblk, (((1,), (1,)), ((), ())),
                                preferred_element_type=F32)
            acc = acc + w_ref[0, :, h:h + 1] * jnp.maximum(s, 0.0)
        si_ref[c] = acc
        return 0

    lax.fori_loop(0, n_vis, chunk, 0)

    qpos = i * Q_TILE + lax.broadcasted_iota(jnp.int32, (Q_TILE, 1), 0)
    lane = lax.broadcasted_iota(jnp.int32, (Q_TILE, kc), 1)
    for c in range(nch):
        s = si_ref[c]
        s = jnp.where(lane + c * kc <= qpos, s, -jnp.inf)
        s = jnp.where(s == 0.0, 0.0, s)
        key_ref[c] = _float_sort_key(s)

    def count(pred):
        tot = jnp.zeros((Q_TILE, 1), F32)
        for c in range(nch):
            tot = tot + jnp.sum(jnp.where(pred(key_ref[c], c), 1.0, 0.0), axis=1, keepdims=True)
        return tot

    def tbit(b, t_u):
        cand_u = t_u | jnp.left_shift(jnp.int32(1), 31 - b)
        cand_s = cand_u ^ INT_MIN
        cnt = count(lambda key, c: key >= cand_s)
        return jnp.where(cnt >= topk, cand_u, t_u)

    t_u = lax.fori_loop(0, 32, tbit, jnp.zeros((Q_TILE, 1), jnp.int32))
    thr = t_u ^ INT_MIN

    need = topk - count(lambda key, c: key > thr)
    n_eq = count(lambda key, c: key == thr)
    jbits = max(1, (L - 1).bit_length())
    jmax_ref[...] = jnp.full((Q_TILE, 1), (1 << jbits) - 1, jnp.int32)

    @pl.when(jnp.max(jnp.abs(n_eq - need)) > 0.0)
    def _():
        def jbit(b, j):
            cand = j | jnp.left_shift(jnp.int32(1), jbits - 1 - b)
            cnt = count(lambda key, c: jnp.where(lane + c * kc < cand, key, thr - 1) == thr)
            return jnp.where(cnt < need, cand, j)

        jmax_ref[...] = lax.fori_loop(0, jbits, jbit, jnp.zeros((Q_TILE, 1), jnp.int32))

    jlim = jnp.minimum(jmax_ref[...], qpos)
    for c in range(nch):
        key = key_ref[c]
        tie = jnp.where(lane + c * kc <= jlim, 0.0, MASK_NEG)
        bias = jnp.where(key > thr, 0.0, jnp.where(key == thr, tie, MASK_NEG))
        bias_ref[0, 0, c] = bias.astype(bias_ref.dtype)


def _indexer(q_idx, k_idx, w_idx, topk, kc):
    bsz, L, _ = q_idx.shape
    nq, nch = L // Q_TILE, L // kc
    return pl.pallas_call(
        functools.partial(_indexer_kernel, L=L, topk=topk, kc=kc),
        out_shape=jax.ShapeDtypeStruct((bsz, nq, nch, Q_TILE, kc), BF16),
        grid=(bsz, nq),
        in_specs=[pl.BlockSpec((1, Q_TILE, IDX_HEADS * IDX_DIM), lambda b, i: (b, i, 0)),
                  pl.BlockSpec((1, L, IDX_DIM), lambda b, i: (b, 0, 0)),
                  pl.BlockSpec((1, Q_TILE, 128), lambda b, i: (b, i, 0))],
        out_specs=pl.BlockSpec((1, 1, nch, Q_TILE, kc), lambda b, i: (b, i, 0, 0, 0)),
        scratch_shapes=[pltpu.VMEM((nch, Q_TILE, kc), F32),
                        pltpu.VMEM((nch, Q_TILE, kc), jnp.int32),
                        pltpu.VMEM((Q_TILE, 1), jnp.int32)],
        compiler_params=_params(2, 32),
        name="dsa_indexer",
    )(q_idx, k_idx, w_idx)


def _mla_kernel(q_ref, k_ref, v_ref, bias_ref, g_ref, o_ref, m_ref, l_ref, acc_ref, *, kc, nh):
    i = pl.program_id(2)
    n_vis = ((i + 1) * Q_TILE + kc - 1) // kc
    m_ref[...] = jnp.full(m_ref.shape, MASK_NEG, F32)
    l_ref[...] = jnp.zeros(l_ref.shape, F32)
    acc_ref[...] = jnp.zeros(acc_ref.shape, F32)

    def chunk(c, _):
        k0 = pl.multiple_of(c * kc, kc)
        bias = bias_ref[0, 0, c].astype(F32)
        for h in range(nh):
            hs = slice(h * HEAD_DIM, (h + 1) * HEAD_DIM)
            q = q_ref[0, :, hs]
            k = k_ref[0, pl.ds(k0, kc), hs]
            v = v_ref[0, pl.ds(k0, kc), hs]
            s = lax.dot_general(q, k, (((1,), (1,)), ((), ())),
                                preferred_element_type=F32) + bias
            m_old = m_ref[h]
            m_new = jnp.maximum(m_old, jnp.max(s, axis=1, keepdims=True))
            alpha = jnp.exp(m_old - m_new)
            p = jnp.exp(s - m_new)
            l_ref[h] = alpha * l_ref[h] + jnp.sum(p, axis=1, keepdims=True)
            acc_ref[h] = alpha * acc_ref[h] + jnp.dot(p.astype(BF16), v,
                                                      preferred_element_type=F32)
            m_ref[h] = m_new
        return 0

    lax.fori_loop(0, n_vis, chunk, 0)

    for h in range(nh):
        hs = slice(h * HEAD_DIM, (h + 1) * HEAD_DIM)
        g = g_ref[0, :, hs].astype(F32)
        o = acc_ref[h] / l_ref[h]
        o_ref[0, :, hs] = (o * (g / (1.0 + jnp.exp(-g)))).astype(o_ref.dtype)


def _masked_attention(q, k, v, bias, gate, kc, nh=8):
    bsz, L, _ = q.shape
    nq, nch = L // Q_TILE, L // kc
    w = nh * HEAD_DIM
    qspec = pl.BlockSpec((1, Q_TILE, w), lambda b, g, i: (b, i, g))
    kvspec = pl.BlockSpec((1, L, w), lambda b, g, i: (b, 0, g))
    return pl.pallas_call(
        functools.partial(_mla_kernel, kc=kc, nh=nh),
        out_shape=jax.ShapeDtypeStruct((bsz, L, C_WIDTH), BF16),
        grid=(bsz, MLA_HEADS // nh, nq),
        in_specs=[qspec, kvspec, kvspec,
                  pl.BlockSpec((1, 1, nch, Q_TILE, kc), lambda b, g, i: (b, i, 0, 0, 0)),
                  qspec],
        out_specs=qspec,
        scratch_shapes=[pltpu.VMEM((nh, Q_TILE, 1), F32),
                        pltpu.VMEM((nh, Q_TILE, 1), F32),
                        pltpu.VMEM((nh, Q_TILE, HEAD_DIM), F32)],
        compiler_params=_params(3, 48),
        name="dsa_attention",
    )(q, k, v, bias, gate)


def _odd_layer(x2d, bsz, L, norm, w_in, q_norm, kv_norm, k_idx_norm, w_uq, w_uk, w_uv, w_iq,
               w_out, tables):
    m = bsz * L
    h = _rmsnorm(x2d, norm, BF16)

    c0 = Q_RANK + KV_RANK
    zpad = lambda w: jnp.zeros((D_MODEL, w), F32)
    w_small = jnp.concatenate([
        w_in[:, :c0],
        w_in[:, c0:c0 + ROT_DIM], zpad(128 - ROT_DIM),
        w_in[:, c0 + ROT_DIM:c0 + ROT_DIM + IDX_DIM],
        w_in[:, c0 + ROT_DIM + IDX_DIM:c0 + ROT_DIM + IDX_DIM + IDX_HEADS], zpad(128 - IDX_HEADS),
        zpad(ODD_SMALL_PAD - ODD_SMALL)], axis=1)
    w_gate = w_in[:, c0 + ROT_DIM + IDX_DIM + IDX_HEADS:]

    p_small = _matmul([h], [w_small], ODD_SMALL_PAD, F32, name="odd_in_small")
    gate = _matmul([h], [w_gate], C_WIDTH, BF16, name="odd_in_gate")
    c_q, c_kv, k_rope, k_idx, w_idx = _odd_prep(p_small, q_norm, kv_norm, k_idx_norm, tables, L)

    q = _matmul([c_q], [w_uq], C_WIDTH, BF16, epilogue="rope", extra=tables, seq_len=L,
                scale=HEAD_DIM ** -0.5, name="odd_q_up")
    q_idx = _matmul([c_q], [w_iq], IDX_HEADS * IDX_DIM, BF16, epilogue="rope", extra=tables,
                    seq_len=L, scale=IDX_DIM ** -0.5, name="odd_q_idx")

    w_k = jnp.pad(w_uk, ((0, 0), (0, 0), (ROT_DIM, 0))).reshape(KV_RANK, C_WIDTH)
    k = _matmul([c_kv], [w_k], C_WIDTH, BF16, epilogue="add_head", extra=(k_rope,),
                name="odd_k_up")
    v = _matmul([c_kv], [w_uv.reshape(KV_RANK, C_WIDTH)], C_WIDTH, BF16, name="odd_v_up")

    topk = min(IDX_TOPK_MAX, L // 4)
    kc = min(512, L)
    bias = _indexer(q_idx.reshape(bsz, L, -1), k_idx.reshape(bsz, L, IDX_DIM),
                    w_idx.reshape(bsz, L, 128), topk, kc)
    o = _masked_attention(q.reshape(bsz, L, C_WIDTH), k.reshape(bsz, L, C_WIDTH),
                          v.reshape(bsz, L, C_WIDTH), bias, gate.reshape(bsz, L, C_WIDTH), kc)
    return _matmul([o.reshape(m, C_WIDTH)], [w_out], D_MODEL, F32, res=x2d, name="odd_out")


def kernel(x, even_norm, even_w_in, s5_lambda_re, s5_lambda_im, s5_log_step, s5_b_re, s5_b_im,
           s5_c_re, s5_c_im, s5_d, s5_glu_w, s5_glu_b, even_w_out, odd_norm, odd_w_in,
           mla_q_norm, mla_kv_norm, idx_k_norm, mla_w_uq, mla_w_uk, mla_w_uv, idx_w_q,
           odd_w_out, final_norm):
    bsz, L, d = x.shape
    depth = even_norm.shape[0] + odd_norm.shape[0]
    tables = _rope_tables(L)
    x2d = x.reshape(bsz * L, d)
    for layer in range(depth):
        i = layer // 2
        if layer % 2 == 0:
            x2d = _even_layer(x2d, bsz, L, even_norm[i], even_w_in[i], s5_lambda_re[i],
                              s5_lambda_im[i], s5_log_step[i], s5_b_re[i], s5_b_im[i],
                              s5_c_re[i], s5_c_im[i], s5_d[i], s5_glu_w[i], s5_glu_b[i],
                              even_w_out[i], tables)
        else:
            x2d = _odd_layer(x2d, bsz, L, odd_norm[i], odd_w_in[i], mla_q_norm[i],
                             mla_kv_norm[i], idx_k_norm[i], mla_w_uq[i], mla_w_uk[i],
                             mla_w_uv[i], idx_w_q[i], odd_w_out[i], tables)
    return _rmsnorm(x2d, final_norm, F32).reshape(bsz, L, d)
```

```python
import functools

import jax
import jax.numpy as jnp
from jax import lax
from jax.experimental import pallas as pl
from jax.experimental.pallas import tpu as pltpu

F32 = jnp.float32
BF16 = jnp.bfloat16

D_MODEL = 4096
EPS = 1e-6
ROPE_THETA = 500000.0
HEAD_DIM = 128
ROT_DIM = HEAD_DIM // 4
ROT_HALF = ROT_DIM // 2

S5_WIDTH = D_MODEL // 4
S5_GROUP = 16
S5_GROUPS = S5_WIDTH // S5_GROUP
S5_STATE = 64
S5_BLOCKS = S5_WIDTH // 128
S5_BLOCK_GROUPS = S5_GROUPS // S5_BLOCKS
S5_BLOCK_STATES = S5_BLOCK_GROUPS * S5_STATE

DIL_PATTERNS = ((128, 1), (512, 4), (2048, 16))
N_DIL = len(DIL_PATTERNS)
DIL_HEADS = D_MODEL // 512
DIL_WIDTH = DIL_HEADS * HEAD_DIM
QKV_WIDTH = 3 * N_DIL * DIL_WIDTH

MLA_HEADS = D_MODEL // HEAD_DIM
MLA_NOPE = HEAD_DIM - ROT_DIM
Q_RANK = 1536
KV_RANK = 512
IDX_HEADS = D_MODEL // 128
IDX_DIM = 128
IDX_TOPK_MAX = 256
C_WIDTH = MLA_HEADS * HEAD_DIM
ODD_SMALL = Q_RANK + KV_RANK + 3 * 128
ODD_SMALL_PAD = 2560

LANES = 128
SUBLANES = 8
Q_TILE = 128
MASK_NEG = -1e30
LOG2E = 1.4426950408889634
INT_MIN = -(2 ** 31)


def _params(n_grid, vmem_mb):
    return pltpu.CompilerParams(
        dimension_semantics=("arbitrary",) * n_grid,
        vmem_limit_bytes=vmem_mb * 1024 * 1024)


def _rope_tables(L):
    inv = ROPE_THETA ** (-jnp.arange(0, ROT_DIM, 2, dtype=F32) / ROT_DIM)
    ang = jnp.arange(L, dtype=F32)[:, None] * inv[None, :]
    cos, sin = jnp.cos(ang), jnp.sin(ang)
    ones = jnp.ones((L, HEAD_DIM - ROT_DIM), F32)
    zeros = jnp.zeros((L, HEAD_DIM - ROT_DIM), F32)
    zh = jnp.zeros((L, ROT_HALF), F32)
    c = jnp.concatenate([cos, cos, ones], axis=1)
    sa = jnp.concatenate([-sin, zh, zeros], axis=1)
    sb = jnp.concatenate([zh, sin, zeros], axis=1)
    return c, sa, sb


def _rope_head(x, c, sa, sb):
    return (x * c + pltpu.roll(x, HEAD_DIM - ROT_HALF, 1) * sa
            + pltpu.roll(x, ROT_HALF, 1) * sb)


def _rmsnorm_kernel(x_ref, g_ref, o_ref):
    x = x_ref[...]
    ms = jnp.mean(x * x, axis=-1, keepdims=True)
    o_ref[...] = (x * lax.rsqrt(ms + EPS) * g_ref[...]).astype(o_ref.dtype)


def _rmsnorm(x, g, out_dtype, tm=256):
    m, d = x.shape
    return pl.pallas_call(
        _rmsnorm_kernel,
        out_shape=jax.ShapeDtypeStruct((m, d), out_dtype),
        grid=(m // tm,),
        in_specs=[pl.BlockSpec((tm, d), lambda i: (i, 0)),
                  pl.BlockSpec((1, d), lambda i: (0, 0))],
        out_specs=pl.BlockSpec((tm, d), lambda i: (i, 0)),
        compiler_params=_params(1, 32),
        name="rmsnorm",
    )(x, g.reshape(1, d))


def _mm_kernel(*refs, n_a, has_res, epilogue, scale, tiles_per_kind):
    a_refs = refs[:n_a]
    w_refs = refs[n_a:2 * n_a]
    pos = 2 * n_a
    res_ref = None
    if has_res:
        res_ref = refs[pos]
        pos += 1
    extra = refs[pos:-1]
    o_ref = refs[-1]

    acc = None
    for a_ref, w_ref in zip(a_refs, w_refs):
        d = jnp.dot(a_ref[...], w_ref[...].astype(BF16), preferred_element_type=F32)
        acc = d if acc is None else acc + d
    if has_res:
        acc = acc + res_ref[...]
    n_heads = acc.shape[1] // HEAD_DIM

    def roped():
        c, sa, sb = extra[0][...], extra[1][...], extra[2][...]
        cols = []
        for h in range(n_heads):
            y = _rope_head(acc[:, h * HEAD_DIM:(h + 1) * HEAD_DIM], c, sa, sb)
            cols.append(y * scale if scale != 1.0 else y)
        return jnp.concatenate(cols, axis=1)

    if epilogue == "none":
        o_ref[...] = acc.astype(o_ref.dtype)
    elif epilogue == "rope":
        o_ref[...] = roped().astype(o_ref.dtype)
    elif epilogue == "rope_qk":
        kind = (pl.program_id(1) // tiles_per_kind) % 3

        @pl.when(kind == 2)
        def _():
            o_ref[...] = acc.astype(o_ref.dtype)

        @pl.when(kind != 2)
        def _():
            o_ref[...] = roped().astype(o_ref.dtype)
    elif epilogue == "add_head":
        kr = extra[0][...]
        cols = [acc[:, h * HEAD_DIM:(h + 1) * HEAD_DIM] + kr for h in range(n_heads)]
        o_ref[...] = jnp.concatenate(cols, axis=1).astype(o_ref.dtype)
    else:
        raise ValueError(epilogue)


def _matmul(a_list, w_list, n, out_dtype, *, w_col0=0, w_row_blocks=None, res=None,
            epilogue="none", extra=(), scale=1.0, seq_len=None, tm=1024, tn=512,
            vmem_mb=52, name="matmul"):
    m = a_list[0].shape[0]
    assert m % tm == 0 and n % tn == 0 and w_col0 % tn == 0
    col_off = w_col0 // tn
    n_a = len(a_list)
    if w_row_blocks is None:
        w_row_blocks = [0] * n_a
    in_specs, args = [], []
    for a in a_list:
        k = a.shape[1]
        in_specs.append(pl.BlockSpec((tm, k), lambda i, j: (i, 0)))
        args.append(a)
    for a, w, rb in zip(a_list, w_list, w_row_blocks):
        k = a.shape[1]
        in_specs.append(pl.BlockSpec((k, tn), lambda i, j, rb=rb: (rb, j + col_off)))
        args.append(w)
    if res is not None:
        in_specs.append(pl.BlockSpec((tm, tn), lambda i, j: (i, j)))
        args.append(res)
    tiles_per_seq = None if seq_len is None else seq_len // tm
    for e in extra:
        if e.shape[0] == m:
            in_specs.append(pl.BlockSpec((tm, e.shape[1]), lambda i, j: (i, 0)))
        else:
            assert seq_len is not None and e.shape[0] == seq_len and seq_len % tm == 0
            in_specs.append(pl.BlockSpec((tm, e.shape[1]),
                                         lambda i, j: (i % tiles_per_seq, 0)))
        args.append(e)
    kern = functools.partial(_mm_kernel, n_a=n_a, has_res=res is not None,
                             epilogue=epilogue, scale=scale,
                             tiles_per_kind=DIL_WIDTH // tn)
    return pl.pallas_call(
        kern,
        out_shape=jax.ShapeDtypeStruct((m, n), out_dtype),
        grid=(m // tm, n // tn),
        in_specs=in_specs,
        out_specs=pl.BlockSpec((tm, tn), lambda i, j: (i, j)),
        compiler_params=_params(2, vmem_mb),
        name=name,
    )(*args)


def _s5_discretise(lam_re, lam_im, log_step, b_re, b_im, c_re, c_im):
    lr = jnp.minimum(lam_re.astype(F32), -1e-4)
    li = lam_im.astype(F32)
    dt = jnp.exp(log_step.astype(F32))[:, None]
    mag = jnp.exp(lr * dt)
    ar = mag * jnp.cos(li * dt)
    ai = mag * jnp.sin(li * dt)
    den = lr * lr + li * li
    nr, ni = ar - 1.0, ai
    cr = (nr * lr + ni * li) / den
    ci = (ni * lr - nr * li) / den
    br32, bi32 = b_re.astype(F32), b_im.astype(F32)
    bbr = cr[..., None] * br32 - ci[..., None] * bi32
    bbi = cr[..., None] * bi32 + ci[..., None] * br32

    eye = jnp.eye(S5_BLOCK_GROUPS, dtype=F32)

    def in_blockdiag(t):
        t = t.reshape(S5_BLOCKS, S5_BLOCK_GROUPS, S5_STATE, S5_GROUP)
        t = jnp.einsum("bgpj,gh->bgjhp", t, eye)
        return t.reshape(S5_BLOCKS, 128, S5_BLOCK_STATES)

    def out_blockdiag(t):
        t = t.reshape(S5_BLOCKS, S5_BLOCK_GROUPS, S5_GROUP, S5_STATE)
        t = jnp.einsum("bgjp,gh->bgphj", t, eye)
        return t.reshape(S5_BLOCKS, S5_BLOCK_STATES, 128)

    b_mat = jnp.concatenate([in_blockdiag(bbr), in_blockdiag(bbi)], axis=2).astype(BF16)
    c_mat = jnp.concatenate([out_blockdiag(c_re.astype(F32)),
                             -out_blockdiag(c_im.astype(F32))], axis=1).astype(BF16)

    a_r = ar.reshape(S5_BLOCKS, S5_BLOCK_STATES)
    a_i = ai.reshape(S5_BLOCKS, S5_BLOCK_STATES)
    pr, pi = [a_r], [a_i]
    for _ in range(SUBLANES - 1):
        pr.append(pr[-1] * a_r - pi[-1] * a_i)
        pi.append(pr[-2] * a_i + pi[-1] * a_r)
    pow_re = jnp.stack(pr, axis=1)
    pow_im = jnp.stack(pi, axis=1)
    return b_mat, c_mat, pow_re, pow_im


def _s5_kernel(u_ref, b_ref, c_ref, pre_ref, pim_ref, d_ref, y_ref, s_ref):
    L = u_ref.shape[1]
    ns = S5_BLOCK_STATES
    u = u_ref[0]
    s_ref[...] = jnp.dot(u.astype(BF16), b_ref[0], preferred_element_type=F32)

    row = lax.broadcasted_iota(jnp.int32, (SUBLANES, ns), 0)
    pre = pre_ref[0]
    pim = pim_ref[0]

    def bcast(v, r):
        return jnp.broadcast_to(v[r:r + 1, :], (SUBLANES, ns))

    steps = tuple((d, bcast(pre, d - 1), bcast(pim, d - 1)) for d in (1, 2, 4))

    def block(i, carry):
        cr, ci = carry
        off = pl.multiple_of(i * SUBLANES, SUBLANES)
        xr = s_ref[pl.ds(off, SUBLANES), 0:ns]
        xi = s_ref[pl.ds(off, SUBLANES), ns:2 * ns]
        for d, ar, ai in steps:
            sr = jnp.where(row >= d, pltpu.roll(xr, d, 0), 0.0)
            si = jnp.where(row >= d, pltpu.roll(xi, d, 0), 0.0)
            xr, xi = xr + ar * sr - ai * si, xi + ar * si + ai * sr
        crb = jnp.broadcast_to(cr, (SUBLANES, ns))
        cib = jnp.broadcast_to(ci, (SUBLANES, ns))
        xr, xi = xr + pre * crb - pim * cib, xi + pre * cib + pim * crb
        s_ref[pl.ds(off, SUBLANES), 0:ns] = xr
        s_ref[pl.ds(off, SUBLANES), ns:2 * ns] = xi
        return xr[SUBLANES - 1:SUBLANES, :], xi[SUBLANES - 1:SUBLANES, :]

    zero = jnp.zeros((1, ns), F32)
    lax.fori_loop(0, L // SUBLANES, block, (zero, zero))

    y = jnp.dot(s_ref[...].astype(BF16), c_ref[0], preferred_element_type=F32)
    y_ref[0] = y + d_ref[0] * u


def _s5(proj_a, b_mat, c_mat, pow_re, pow_im, d_skip):
    bsz, L, _ = proj_a.shape
    ns = S5_BLOCK_STATES
    return pl.pallas_call(
        _s5_kernel,
        out_shape=jax.ShapeDtypeStruct((bsz, L, S5_WIDTH), F32),
        grid=(bsz, S5_BLOCKS),
        in_specs=[
            pl.BlockSpec((1, L, 128), lambda b, g: (b, 0, g)),
            pl.BlockSpec((1, 128, 2 * ns), lambda b, g: (g, 0, 0)),
            pl.BlockSpec((1, 2 * ns, 128), lambda b, g: (g, 0, 0)),
            pl.BlockSpec((1, SUBLANES, ns), lambda b, g: (g, 0, 0)),
            pl.BlockSpec((1, SUBLANES, ns), lambda b, g: (g, 0, 0)),
            pl.BlockSpec((1, 1, 128), lambda b, g: (g, 0, 0)),
        ],
        out_specs=pl.BlockSpec((1, L, 128), lambda b, g: (b, 0, g)),
        scratch_shapes=[pltpu.VMEM((L, 2 * ns), F32)],
        compiler_params=_params(2, 40),
        name="s5_scan",
    )(proj_a, b_mat, c_mat, pow_re, pow_im, d_skip.reshape(S5_BLOCKS, 1, 128))


def _glu_kernel(y_ref, z_ref, w_ref, b_ref, o_ref):
    g = jax.nn.gelu(y_ref[...])
    t = jnp.dot(g.astype(BF16), w_ref[...].astype(BF16), preferred_element_type=F32) + b_ref[...]
    z = z_ref[...]
    gate = z / (1.0 + jnp.exp(-z))
    o_ref[...] = (g / (1.0 + jnp.exp(-t)) * gate).astype(o_ref.dtype)


def _glu(y, proj_a, glu_w, glu_b, tm=512):
    m = y.shape[0]
    return pl.pallas_call(
        _glu_kernel,
        out_shape=jax.ShapeDtypeStruct((m, S5_WIDTH), BF16),
        grid=(m // tm,),
        in_specs=[pl.BlockSpec((tm, S5_WIDTH), lambda i: (i, 0)),
                  pl.BlockSpec((tm, S5_WIDTH), lambda i: (i, 1)),
                  pl.BlockSpec((S5_WIDTH, S5_WIDTH), lambda i: (0, 0)),
                  pl.BlockSpec((1, S5_WIDTH), lambda i: (0, 0))],
        out_specs=pl.BlockSpec((tm, S5_WIDTH), lambda i: (i, 0)),
        compiler_params=_params(1, 40),
        name="s5_glu",
    )(y, proj_a, glu_w, glu_b.reshape(1, S5_WIDTH))


DIL_UNROLL = 4
MERGE_ROWS = 256


def _dil_kernel(*refs, L):
    qkv_refs = refs[:3 * N_DIL]
    z_ref, out_ref, og_ref, lg_ref = refs[3 * N_DIL:]
    n_tiles = L // Q_TILE

    for g, (window, dil) in enumerate(DIL_PATTERNS):
        q_ref, k_ref, v_ref = qkv_refs[3 * g:3 * g + 3]
        nb = (L // dil) // Q_TILE
        cw = Q_TILE if nb == 1 else 2 * Q_TILE
        half = (window // dil) // 2

        def rows(start, size, dil=dil):
            return pl.ds(start, size) if dil == 1 else pl.ds(start, size, stride=dil)

        def block(it, g=g, q_ref=q_ref, k_ref=k_ref, v_ref=v_ref, nb=nb, cw=cw, half=half,
                  dil=dil, rows=rows):
            r = it // nb
            qb = it % nb
            kb = jnp.clip(qb - 1, 0, max(nb - 2, 0))
            q0 = r + qb * (Q_TILE * dil)
            k0 = r + kb * (Q_TILE * dil)
            q = q_ref[0, rows(q0, Q_TILE), :].astype(BF16)
            k = k_ref[0, rows(k0, cw), :].astype(BF16)
            v = v_ref[0, rows(k0, cw), :].astype(BF16)
            qpos = qb * Q_TILE + lax.broadcasted_iota(jnp.int32, (Q_TILE, cw), 0)
            kpos = kb * Q_TILE + lax.broadcasted_iota(jnp.int32, (Q_TILE, cw), 1)
            valid = jnp.abs(qpos - kpos - half) <= half
            s = lax.dot_general(q, k, (((1,), (1,)), ((), ())),
                                preferred_element_type=F32) * (HEAD_DIM ** -0.5)
            s = jnp.where(valid, s, MASK_NEG)
            m = jnp.max(s, axis=1, keepdims=True)
            p = jnp.exp(s - m)
            l = jnp.sum(p, axis=1, keepdims=True)
            o = jnp.dot(p.astype(BF16), v, preferred_element_type=F32) / l
            og_ref[g, rows(q0, Q_TILE), :] = o
            lg_ref[g, rows(q0, Q_TILE), :] = jnp.broadcast_to(m + jnp.log(l),
                                                             (Q_TILE, HEAD_DIM))

        def unrolled(j, _, block=block):
            for u in range(DIL_UNROLL):
                block(j * DIL_UNROLL + u)
            return 0

        lax.fori_loop(0, n_tiles // DIL_UNROLL, unrolled, 0)

    def merge(j, _):
        rs = pl.ds(pl.multiple_of(j * MERGE_ROWS, MERGE_ROWS), MERGE_ROWS)
        a = [lg_ref[g, rs, :] for g in range(N_DIL)]
        m = functools.reduce(jnp.maximum, a)
        e = [jnp.exp(x - m) for x in a]
        o = sum(e[g] * og_ref[g, rs, :] for g in range(N_DIL)) / sum(e)
        z = z_ref[0, rs, :]
        out_ref[0, rs, :] = (o * (z / (1.0 + jnp.exp(-z)))).astype(out_ref.dtype)
        return 0

    lax.fori_loop(0, L // MERGE_ROWS, merge, 0)


def _dilated_attention(qkv, proj_b):
    bsz, L, _ = qkv.shape
    for window, dil in DIL_PATTERNS:
        assert (L // dil) % Q_TILE == 0 and (window // dil) % 2 == 0
    assert (L // Q_TILE) % DIL_UNROLL == 0 and L % MERGE_ROWS == 0

    def in_spec(g, kind):
        return pl.BlockSpec((1, L, HEAD_DIM),
                            lambda b, h: (b, 0, (g * 3 + kind) * DIL_HEADS + h))

    head_spec = pl.BlockSpec((1, L, HEAD_DIM), lambda b, h: (b, 0, h))
    return pl.pallas_call(
        functools.partial(_dil_kernel, L=L),
        out_shape=jax.ShapeDtypeStruct((bsz, L, DIL_WIDTH), BF16),
        grid=(bsz, DIL_HEADS),
        in_specs=[in_spec(g, kind) for g in range(N_DIL) for kind in range(3)] + [head_spec],
        out_specs=head_spec,
        scratch_shapes=[pltpu.VMEM((N_DIL, L, HEAD_DIM), F32),
                        pltpu.VMEM((N_DIL, L, HEAD_DIM), F32)],
        compiler_params=_params(2, 48),
        name="dilated_attention",
    )(*([qkv] * (3 * N_DIL)), proj_b)


def _even_layer(x2d, bsz, L, norm, w_in, lam_re, lam_im, log_step, b_re, b_im, c_re, c_im,
                d_skip, glu_w, glu_b, w_out, tables):
    h = _rmsnorm(x2d, norm, BF16)
    proj_a = _matmul([h], [w_in], 2 * S5_WIDTH, F32, w_col0=0, name="even_in_a")
    qkv = _matmul([h], [w_in], QKV_WIDTH, F32, w_col0=2 * S5_WIDTH, epilogue="rope_qk",
                  extra=tables, seq_len=L, name="even_in_qkv")
    proj_b = _matmul([h], [w_in], DIL_WIDTH, F32, w_col0=2 * S5_WIDTH + QKV_WIDTH,
                     name="even_in_zb")

    b_mat, c_mat, pow_re, pow_im = _s5_discretise(lam_re, lam_im, log_step, b_re, b_im,
                                                  c_re, c_im)
    y = _s5(proj_a.reshape(bsz, L, 2 * S5_WIDTH), b_mat, c_mat, pow_re, pow_im, d_skip)
    a_out = _glu(y.reshape(bsz * L, S5_WIDTH), proj_a, glu_w, glu_b)

    b_out = _dilated_attention(qkv.reshape(bsz, L, QKV_WIDTH),
                               proj_b.reshape(bsz, L, DIL_WIDTH)).reshape(bsz * L, DIL_WIDTH)

    return _matmul([a_out, b_out], [w_out, w_out], D_MODEL, F32, w_row_blocks=[0, 1],
                   res=x2d, name="even_out")


def _odd_prep_kernel(p_ref, qn_ref, kvn_ref, kin_ref, c_ref, sa_ref, sb_ref,
                     cq_ref, ckv_ref, kr_ref, ki_ref, wi_ref):
    def norm(x, g):
        ms = jnp.mean(x * x, axis=-1, keepdims=True)
        return x * lax.rsqrt(ms + EPS) * g

    c, sa, sb = c_ref[...], sa_ref[...], sb_ref[...]
    o = 0
    cq_ref[...] = norm(p_ref[:, o:o + Q_RANK], qn_ref[...]).astype(cq_ref.dtype)
    o += Q_RANK
    ckv_ref[...] = norm(p_ref[:, o:o + KV_RANK], kvn_ref[...]).astype(ckv_ref.dtype)
    o += KV_RANK
    kr_ref[...] = _rope_head(p_ref[:, o:o + 128], c, sa, sb)
    o += 128
    ki_ref[...] = _rope_head(norm(p_ref[:, o:o + 128], kin_ref[...]), c, sa, sb
                             ).astype(ki_ref.dtype)
    o += 128
    wi_ref[...] = p_ref[:, o:o + 128] * (IDX_HEADS ** -0.5)


def _odd_prep(p_small, q_norm, kv_norm, k_idx_norm, tables, L, tm=256):
    m = p_small.shape[0]
    tps = L // tm
    row = lambda w: pl.BlockSpec((tm, w), lambda i: (i, 0))
    tab = pl.BlockSpec((tm, 128), lambda i: (i % tps, 0))
    vec = lambda w: pl.BlockSpec((1, w), lambda i: (0, 0))
    return pl.pallas_call(
        _odd_prep_kernel,
        out_shape=[jax.ShapeDtypeStruct((m, Q_RANK), BF16),
                   jax.ShapeDtypeStruct((m, KV_RANK), BF16),
                   jax.ShapeDtypeStruct((m, 128), F32),
                   jax.ShapeDtypeStruct((m, 128), BF16),
                   jax.ShapeDtypeStruct((m, 128), F32)],
        grid=(m // tm,),
        in_specs=[row(ODD_SMALL_PAD), vec(Q_RANK), vec(KV_RANK), vec(IDX_DIM), tab, tab, tab],
        out_specs=[row(Q_RANK), row(KV_RANK), row(128), row(128), row(128)],
        compiler_params=_params(1, 32),
        name="odd_prep",
    )(p_small, q_norm.reshape(1, -1), kv_norm.reshape(1, -1), k_idx_norm.reshape(1, -1),
      *tables)


def _float_sort_key(s):
    b = pltpu.bitcast(s, jnp.int32)
    return b ^ ((b >> 31) & 0x7FFFFFFF)


def _indexer_kernel(q_ref, k_ref, w_ref, bias_ref, si_ref, key_ref, jmax_ref, *, L, topk, kc):
    i = pl.program_id(1)
    nch = L // kc
    n_vis = ((i + 1) * Q_TILE + kc - 1) // kc

    si_ref[...] = jnp.full(si_ref.shape, -jnp.inf, F32)

    def chunk(c, _):
        k0 = pl.multiple_of(c * kc, kc)
        kblk = k_ref[0, pl.ds(k0, kc), :]
        acc = jnp.zeros((Q_TILE, kc), F32)
        for h in range(IDX_HEADS):
            qh = q_ref[0, :, h * IDX_DIM:(h + 1) * IDX_DIM]
            s = lax.dot_general(qh, kblk, (((1,), (1,)), ((), ())),
                                preferred_element_type=F32)
            acc = acc + w_ref[0, :, h:h + 1] * jnp.maximum(s, 0.0)
        si_ref[c] = acc
        return 0

    lax.fori_loop(0, n_vis, chunk, 0)

    qpos = i * Q_TILE + lax.broadcasted_iota(jnp.int32, (Q_TILE, 1), 0)
    lane = lax.broadcasted_iota(jnp.int32, (Q_TILE, kc), 1)
    for c in range(nch):
        s = si_ref[c]
        s = jnp.where(lane + c * kc <= qpos, s, -jnp.inf)
        s = jnp.where(s == 0.0, 0.0, s)
        key_ref[c] = _float_sort_key(s)

    def count(pred):
        tot = jnp.zeros((Q_TILE, 1), F32)
        for c in range(nch):
            tot = tot + jnp.sum(jnp.where(pred(key_ref[c], c), 1.0, 0.0), axis=1, keepdims=True)
        return tot

    def tbit(b, t_u):
        cand_u = t_u | jnp.left_shift(jnp.int32(1), 31 - b)
        cand_s = cand_u ^ INT_MIN
        cnt = count(lambda key, c: key >= cand_s)
        return jnp.where(cnt >= topk, cand_u, t_u)

    t_u = lax.fori_loop(0, 32, tbit, jnp.zeros((Q_TILE, 1), jnp.int32))
    thr = t_u ^ INT_MIN

    need = topk - count(lambda key, c: key > thr)
    n_eq = count(lambda key, c: key == thr)
    jbits = max(1, (L - 1).bit_length())
    jmax_ref[...] = jnp.full((Q_TILE, 1), (1 << jbits) - 1, jnp.int32)

    @pl.when(jnp.max(jnp.abs(n_eq - need)) > 0.0)
    def _():
        def jbit(b, j):
            cand = j | jnp.left_shift(jnp.int32(1), jbits - 1 - b)
            cnt = count(lambda key, c: jnp.where(lane + c * kc < cand, key, thr - 1) == thr)
            return jnp.where(cnt < need, cand, j)

        jmax_ref[...] = lax.fori_loop(0, jbits, jbit, jnp.zeros((Q_TILE, 1), jnp.int32))

    jlim = jnp.minimum(jmax_ref[...], qpos)
    for c in range(nch):
        key = key_ref[c]
        tie = jnp.where(lane + c * kc <= jlim, 0.0, MASK_NEG)
        bias = jnp.where(key > thr, 0.0, jnp.where(key == thr, tie, MASK_NEG))
        bias_ref[0, 0, c] = bias.astype(bias_ref.dtype)


def _indexer(q_idx, k_idx, w_idx, topk, kc):
    bsz, L, _ = q_idx.shape
    nq, nch = L // Q_TILE, L // kc
    return pl.pallas_call(
        functools.partial(_indexer_kernel, L=L, topk=topk, kc=kc),
        out_shape=jax.ShapeDtypeStruct((bsz, nq, nch, Q_TILE, kc), BF16),
        grid=(bsz, nq),
        in_specs=[pl.BlockSpec((1, Q_TILE, IDX_HEADS * IDX_DIM), lambda b, i: (b, i, 0)),
                  pl.BlockSpec((1, L, IDX_DIM), lambda b, i: (b, 0, 0)),
                  pl.BlockSpec((1, Q_TILE, 128), lambda b, i: (b, i, 0))],
        out_specs=pl.BlockSpec((1, 1, nch, Q_TILE, kc), lambda b, i: (b, i, 0, 0, 0)),
        scratch_shapes=[pltpu.VMEM((nch, Q_TILE, kc), F32),
                        pltpu.VMEM((nch, Q_TILE, kc), jnp.int32),
                        pltpu.VMEM((Q_TILE, 1), jnp.int32)],
        compiler_params=_params(2, 32),
        name="dsa_indexer",
    )(q_idx, k_idx, w_idx)


def _mla_kernel(q_ref, k_ref, v_ref, bias_ref, g_ref, o_ref, s_ref, mx_ref, ls_ref, acc_ref,
                *, kc, nh):
    i = pl.program_id(2)
    n_vis = ((i + 1) * Q_TILE + kc - 1) // kc
    slabs = kc // LANES

    def fold(x, op):
        t = x[:, 0:LANES]
        for j in range(1, slabs):
            t = op(t, x[:, j * LANES:(j + 1) * LANES])
        return t

    mx_ref[...] = jnp.full(mx_ref.shape, MASK_NEG, F32)

    def scores(c, _):
        k0 = pl.multiple_of(c * kc, kc)
        bias = bias_ref[0, 0, c].astype(F32)
        for h in range(nh):
            hs = slice(h * HEAD_DIM, (h + 1) * HEAD_DIM)
            s = lax.dot_general(q_ref[0, :, hs], k_ref[0, pl.ds(k0, kc), hs],
                                (((1,), (1,)), ((), ())), preferred_element_type=F32) + bias
            s_ref[c, h] = s
            mx_ref[h] = jnp.maximum(mx_ref[h], fold(s, jnp.maximum))
        return 0

    lax.fori_loop(0, n_vis, scores, 0)

    for h in range(nh):
        m = jnp.max(mx_ref[h], axis=1, keepdims=True)
        mx_ref[h] = jnp.broadcast_to(m, (Q_TILE, LANES))
    ls_ref[...] = jnp.zeros(ls_ref.shape, F32)
    acc_ref[...] = jnp.zeros(acc_ref.shape, F32)

    def weights(c, _):
        k0 = pl.multiple_of(c * kc, kc)
        for h in range(nh):
            hs = slice(h * HEAD_DIM, (h + 1) * HEAD_DIM)
            m = mx_ref[h]
            s = s_ref[c, h]
            p = jnp.concatenate(
                [jnp.exp2(s[:, j * LANES:(j + 1) * LANES] - m) for j in range(slabs)], axis=1)
            ls_ref[h] += fold(p, jnp.add)
            acc_ref[h] += jnp.dot(p.astype(BF16), v_ref[0, pl.ds(k0, kc), hs],
                                  preferred_element_type=F32)
        return 0

    lax.fori_loop(0, n_vis, weights, 0)

    for h in range(nh):
        hs = slice(h * HEAD_DIM, (h + 1) * HEAD_DIM)
        g = g_ref[0, :, hs].astype(F32)
        o = acc_ref[h] / jnp.sum(ls_ref[h], axis=1, keepdims=True)
        o_ref[0, :, hs] = (o * (g / (1.0 + jnp.exp(-g)))).astype(o_ref.dtype)


def _masked_attention(q, k, v, bias, gate, kc, nh=8):
    bsz, L, _ = q.shape
    nq, nch = L // Q_TILE, L // kc
    w = nh * HEAD_DIM
    qspec = pl.BlockSpec((1, Q_TILE, w), lambda b, g, i: (b, i, g))
    kvspec = pl.BlockSpec((1, L, w), lambda b, g, i: (b, 0, g))
    return pl.pallas_call(
        functools.partial(_mla_kernel, kc=kc, nh=nh),
        out_shape=jax.ShapeDtypeStruct((bsz, L, C_WIDTH), BF16),
        grid=(bsz, MLA_HEADS // nh, nq),
        in_specs=[qspec, kvspec, kvspec,
                  pl.BlockSpec((1, 1, nch, Q_TILE, kc), lambda b, g, i: (b, i, 0, 0, 0)),
                  qspec],
        out_specs=qspec,
        scratch_shapes=[pltpu.VMEM((nch, nh, Q_TILE, kc), F32),
                        pltpu.VMEM((nh, Q_TILE, LANES), F32),
                        pltpu.VMEM((nh, Q_TILE, LANES), F32),
                        pltpu.VMEM((nh, Q_TILE, HEAD_DIM), F32)],
        compiler_params=_params(3, 48),
        name="dsa_attention",
    )(q, k, v, bias, gate)


def _odd_layer(x2d, bsz, L, norm, w_in, q_norm, kv_norm, k_idx_norm, w_uq, w_uk, w_uv, w_iq,
               w_out, tables):
    m = bsz * L
    h = _rmsnorm(x2d, norm, BF16)

    c0 = Q_RANK + KV_RANK
    zpad = lambda w: jnp.zeros((D_MODEL, w), F32)
    w_small = jnp.concatenate([
        w_in[:, :c0],
        w_in[:, c0:c0 + ROT_DIM], zpad(128 - ROT_DIM),
        w_in[:, c0 + ROT_DIM:c0 + ROT_DIM + IDX_DIM],
        w_in[:, c0 + ROT_DIM + IDX_DIM:c0 + ROT_DIM + IDX_DIM + IDX_HEADS], zpad(128 - IDX_HEADS),
        zpad(ODD_SMALL_PAD - ODD_SMALL)], axis=1)
    w_gate = w_in[:, c0 + ROT_DIM + IDX_DIM + IDX_HEADS:]

    p_small = _matmul([h], [w_small], ODD_SMALL_PAD, F32, name="odd_in_small")
    gate = _matmul([h], [w_gate], C_WIDTH, BF16, name="odd_in_gate")
    c_q, c_kv, k_rope, k_idx, w_idx = _odd_prep(p_small, q_norm, kv_norm, k_idx_norm, tables, L)

    q = _matmul([c_q], [w_uq], C_WIDTH, BF16, epilogue="rope", extra=tables, seq_len=L,
                scale=LOG2E * HEAD_DIM ** -0.5, name="odd_q_up")
    q_idx = _matmul([c_q], [w_iq], IDX_HEADS * IDX_DIM, BF16, epilogue="rope", extra=tables,
                    seq_len=L, scale=IDX_DIM ** -0.5, name="odd_q_idx")

    w_k = jnp.pad(w_uk, ((0, 0), (0, 0), (ROT_DIM, 0))).reshape(KV_RANK, C_WIDTH)
    k = _matmul([c_kv], [w_k], C_WIDTH, BF16, epilogue="add_head", extra=(k_rope,),
                name="odd_k_up")
    v = _matmul([c_kv], [w_uv.reshape(KV_RANK, C_WIDTH)], C_WIDTH, BF16, name="odd_v_up")

    topk = min(IDX_TOPK_MAX, L // 4)
    kc = min(512, L)
    bias = _indexer(q_idx.reshape(bsz, L, -1), k_idx.reshape(bsz, L, IDX_DIM),
                    w_idx.reshape(bsz, L, 128), topk, kc)
    o = _masked_attention(q.reshape(bsz, L, C_WIDTH), k.reshape(bsz, L, C_WIDTH),
                          v.reshape(bsz, L, C_WIDTH), bias, gate.reshape(bsz, L, C_WIDTH), kc)
    return _matmul([o.reshape(m, C_WIDTH)], [w_out], D_MODEL, F32, res=x2d, name="odd_out")


def kernel(x, even_norm, even_w_in, s5_lambda_re, s5_lambda_im, s5_log_step, s5_b_re, s5_b_im,
           s5_c_re, s5_c_im, s5_d, s5_glu_w, s5_glu_b, even_w_out, odd_norm, odd_w_in,
           mla_q_norm, mla_kv_norm, idx_k_norm, mla_w_uq, mla_w_uk, mla_w_uv, idx_w_q,
           odd_w_out, final_norm):
    bsz, L, d = x.shape
    depth = even_norm.shape[0] + odd_norm.shape[0]
    tables = _rope_tables(L)
    x2d = x.reshape(bsz * L, d)
    for layer in range(depth):
        i = layer // 2
        if layer % 2 == 0:
            x2d = _even_layer(x2d, bsz, L, even_norm[i], even_w_in[i], s5_lambda_re[i],
                              s5_lambda_im[i], s5_log_step[i], s5_b_re[i], s5_b_im[i],
                              s5_c_re[i], s5_c_im[i], s5_d[i], s5_glu_w[i], s5_glu_b[i],
                              even_w_out[i], tables)
        else:
            x2d = _odd_layer(x2d, bsz, L, odd_norm[i], odd_w_in[i], mla_q_norm[i],
                             mla_kv_norm[i], idx_k_norm[i], mla_w_uq[i], mla_w_uk[i],
                             mla_w_uv[i], idx_w_q[i], odd_w_out[i], tables)
    return _rmsnorm(x2d, final_norm, F32).reshape(bsz, L, d)
```

```python
import functools

import jax
import jax.numpy as jnp
from jax import lax
from jax.experimental import pallas as pl
from jax.experimental.pallas import tpu as pltpu

F32 = jnp.float32
BF16 = jnp.bfloat16

D_MODEL = 4096
EPS = 1e-6
ROPE_THETA = 500000.0
HEAD_DIM = 128
ROT_DIM = HEAD_DIM // 4
ROT_HALF = ROT_DIM // 2

S5_WIDTH = D_MODEL // 4
S5_GROUP = 16
S5_GROUPS = S5_WIDTH // S5_GROUP
S5_STATE = 64
S5_BLOCKS = S5_WIDTH // 128
S5_BLOCK_GROUPS = S5_GROUPS // S5_BLOCKS
S5_BLOCK_STATES = S5_BLOCK_GROUPS * S5_STATE

DIL_PATTERNS = ((128, 1), (512, 4), (2048, 16))
N_DIL = len(DIL_PATTERNS)
DIL_HEADS = D_MODEL // 512
DIL_WIDTH = DIL_HEADS * HEAD_DIM
QKV_WIDTH = 3 * N_DIL * DIL_WIDTH

MLA_HEADS = D_MODEL // HEAD_DIM
MLA_NOPE = HEAD_DIM - ROT_DIM
Q_RANK = 1536
KV_RANK = 512
IDX_HEADS = D_MODEL // 128
IDX_DIM = 128
IDX_TOPK_MAX = 256
C_WIDTH = MLA_HEADS * HEAD_DIM
ODD_SMALL = Q_RANK + KV_RANK + 3 * 128
ODD_SMALL_PAD = 2560

LANES = 128
SUBLANES = 8
Q_TILE = 128
MASK_NEG = -1e30
LOG2E = 1.4426950408889634
INT_MIN = -(2 ** 31)


def _params(n_grid, vmem_mb):
    return pltpu.CompilerParams(
        dimension_semantics=("arbitrary",) * n_grid,
        vmem_limit_bytes=vmem_mb * 1024 * 1024)


def _rope_tables(L):
    inv = ROPE_THETA ** (-jnp.arange(0, ROT_DIM, 2, dtype=F32) / ROT_DIM)
    ang = jnp.arange(L, dtype=F32)[:, None] * inv[None, :]
    cos, sin = jnp.cos(ang), jnp.sin(ang)
    ones = jnp.ones((L, HEAD_DIM - ROT_DIM), F32)
    zeros = jnp.zeros((L, HEAD_DIM - ROT_DIM), F32)
    zh = jnp.zeros((L, ROT_HALF), F32)
    c = jnp.concatenate([cos, cos, ones], axis=1)
    sa = jnp.concatenate([-sin, zh, zeros], axis=1)
    sb = jnp.concatenate([zh, sin, zeros], axis=1)
    return c, sa, sb


def _rope_head(x, c, sa, sb):
    return (x * c + pltpu.roll(x, HEAD_DIM - ROT_HALF, 1) * sa
            + pltpu.roll(x, ROT_HALF, 1) * sb)


def _rmsnorm_kernel(x_ref, g_ref, o_ref):
    x = x_ref[...]
    ms = jnp.mean(x * x, axis=-1, keepdims=True)
    o_ref[...] = (x * lax.rsqrt(ms + EPS) * g_ref[...]).astype(o_ref.dtype)


def _rmsnorm(x, g, out_dtype, tm=256):
    m, d = x.shape
    return pl.pallas_call(
        _rmsnorm_kernel,
        out_shape=jax.ShapeDtypeStruct((m, d), out_dtype),
        grid=(m // tm,),
        in_specs=[pl.BlockSpec((tm, d), lambda i: (i, 0)),
                  pl.BlockSpec((1, d), lambda i: (0, 0))],
        out_specs=pl.BlockSpec((tm, d), lambda i: (i, 0)),
        compiler_params=_params(1, 32),
        name="rmsnorm",
    )(x, g.reshape(1, d))


def _mm_kernel(*refs, n_a, has_res, epilogue, scale, tiles_per_kind):
    a_refs = refs[:n_a]
    w_refs = refs[n_a:2 * n_a]
    pos = 2 * n_a
    res_ref = None
    if has_res:
        res_ref = refs[pos]
        pos += 1
    extra = refs[pos:-1]
    o_ref = refs[-1]

    acc = None
    for a_ref, w_ref in zip(a_refs, w_refs):
        d = jnp.dot(a_ref[...], w_ref[...].astype(BF16), preferred_element_type=F32)
        acc = d if acc is None else acc + d
    if has_res:
        acc = acc + res_ref[...]
    n_heads = acc.shape[1] // HEAD_DIM

    def roped():
        c, sa, sb = extra[0][...], extra[1][...], extra[2][...]
        cols = []
        for h in range(n_heads):
            y = _rope_head(acc[:, h * HEAD_DIM:(h + 1) * HEAD_DIM], c, sa, sb)
            cols.append(y * scale if scale != 1.0 else y)
        return jnp.concatenate(cols, axis=1)

    if epilogue == "none":
        o_ref[...] = acc.astype(o_ref.dtype)
    elif epilogue == "rope":
        o_ref[...] = roped().astype(o_ref.dtype)
    elif epilogue == "rope_qk":
        kind = (pl.program_id(1) // tiles_per_kind) % 3

        @pl.when(kind == 2)
        def _():
            o_ref[...] = acc.astype(o_ref.dtype)

        @pl.when(kind != 2)
        def _():
            o_ref[...] = roped().astype(o_ref.dtype)
    elif epilogue == "add_head":
        kr = extra[0][...]
        cols = [acc[:, h * HEAD_DIM:(h + 1) * HEAD_DIM] + kr for h in range(n_heads)]
        o_ref[...] = jnp.concatenate(cols, axis=1).astype(o_ref.dtype)
    else:
        raise ValueError(epilogue)


def _matmul(a_list, w_list, n, out_dtype, *, w_col0=0, w_row_blocks=None, res=None,
            epilogue="none", extra=(), scale=1.0, seq_len=None, tm=2048, tn=None,
            vmem_mb=52, name="matmul"):
    m = a_list[0].shape[0]
    if tn is None:
        tn = 256 if sum(a.shape[1] for a in a_list) >= 4096 else 512
    assert m % tm == 0 and n % tn == 0 and w_col0 % tn == 0
    col_off = w_col0 // tn
    n_a = len(a_list)
    if w_row_blocks is None:
        w_row_blocks = [0] * n_a
    in_specs, args = [], []
    for a in a_list:
        k = a.shape[1]
        in_specs.append(pl.BlockSpec((tm, k), lambda i, j: (i, 0),
                                     pipeline_mode=pl.Buffered(1)))
        args.append(a)
    for a, w, rb in zip(a_list, w_list, w_row_blocks):
        k = a.shape[1]
        in_specs.append(pl.BlockSpec((k, tn), lambda i, j, rb=rb: (rb, j + col_off)))
        args.append(w)
    if res is not None:
        in_specs.append(pl.BlockSpec((tm, tn), lambda i, j: (i, j)))
        args.append(res)
    tiles_per_seq = None if seq_len is None else seq_len // tm
    for e in extra:
        if e.shape[0] == m:
            in_specs.append(pl.BlockSpec((tm, e.shape[1]), lambda i, j: (i, 0)))
        else:
            assert seq_len is not None and e.shape[0] == seq_len and seq_len % tm == 0
            in_specs.append(pl.BlockSpec((tm, e.shape[1]),
                                         lambda i, j: (i % tiles_per_seq, 0)))
        args.append(e)
    kern = functools.partial(_mm_kernel, n_a=n_a, has_res=res is not None,
                             epilogue=epilogue, scale=scale,
                             tiles_per_kind=DIL_WIDTH // tn)
    return pl.pallas_call(
        kern,
        out_shape=jax.ShapeDtypeStruct((m, n), out_dtype),
        grid=(m // tm, n // tn),
        in_specs=in_specs,
        out_specs=pl.BlockSpec((tm, tn), lambda i, j: (i, j)),
        compiler_params=_params(2, vmem_mb),
        name=name,
    )(*args)


def _s5_discretise(lam_re, lam_im, log_step, b_re, b_im, c_re, c_im):
    lr = jnp.minimum(lam_re.astype(F32), -1e-4)
    li = lam_im.astype(F32)
    dt = jnp.exp(log_step.astype(F32))[:, None]
    mag = jnp.exp(lr * dt)
    ar = mag * jnp.cos(li * dt)
    ai = mag * jnp.sin(li * dt)
    den = lr * lr + li * li
    nr, ni = ar - 1.0, ai
    cr = (nr * lr + ni * li) / den
    ci = (ni * lr - nr * li) / den
    br32, bi32 = b_re.astype(F32), b_im.astype(F32)
    bbr = cr[..., None] * br32 - ci[..., None] * bi32
    bbi = cr[..., None] * bi32 + ci[..., None] * br32

    eye = jnp.eye(S5_BLOCK_GROUPS, dtype=F32)

    def in_blockdiag(t):
        t = t.reshape(S5_BLOCKS, S5_BLOCK_GROUPS, S5_STATE, S5_GROUP)
        t = jnp.einsum("bgpj,gh->bgjhp", t, eye)
        return t.reshape(S5_BLOCKS, 128, S5_BLOCK_STATES)

    def out_blockdiag(t):
        t = t.reshape(S5_BLOCKS, S5_BLOCK_GROUPS, S5_GROUP, S5_STATE)
        t = jnp.einsum("bgjp,gh->bgphj", t, eye)
        return t.reshape(S5_BLOCKS, S5_BLOCK_STATES, 128)

    b_mat = jnp.concatenate([in_blockdiag(bbr), in_blockdiag(bbi)], axis=2).astype(BF16)
    c_mat = jnp.concatenate([out_blockdiag(c_re.astype(F32)),
                             -out_blockdiag(c_im.astype(F32))], axis=1).astype(BF16)

    a_r = ar.reshape(S5_BLOCKS, S5_BLOCK_STATES)
    a_i = ai.reshape(S5_BLOCKS, S5_BLOCK_STATES)
    pr, pi = [a_r], [a_i]
    for _ in range(SUBLANES - 1):
        pr.append(pr[-1] * a_r - pi[-1] * a_i)
        pi.append(pr[-2] * a_i + pi[-1] * a_r)
    pow_re = jnp.stack(pr, axis=1)
    pow_im = jnp.stack(pi, axis=1)
    return b_mat, c_mat, pow_re, pow_im


def _s5_kernel(u_ref, b_ref, c_ref, pre_ref, pim_ref, d_ref, y_ref, s_ref):
    L = u_ref.shape[1]
    ns = S5_BLOCK_STATES
    u = u_ref[0]
    s_ref[...] = jnp.dot(u.astype(BF16), b_ref[0], preferred_element_type=F32)

    row = lax.broadcasted_iota(jnp.int32, (SUBLANES, ns), 0)
    pre = pre_ref[0]
    pim = pim_ref[0]

    def bcast(v, r):
        return jnp.broadcast_to(v[r:r + 1, :], (SUBLANES, ns))

    steps = tuple((d, jnp.where(row >= d, bcast(pre, d - 1), 0.0),
                   jnp.where(row >= d, bcast(pim, d - 1), 0.0)) for d in (1, 2, 4))

    def block(i, carry):
        cr, ci = carry
        off = pl.multiple_of(i * SUBLANES, SUBLANES)
        xr = s_ref[pl.ds(off, SUBLANES), 0:ns]
        xi = s_ref[pl.ds(off, SUBLANES), ns:2 * ns]
        for d, ar, ai in steps:
            sr = pltpu.roll(xr, d, 0)
            si = pltpu.roll(xi, d, 0)
            xr, xi = xr + ar * sr - ai * si, xi + ar * si + ai * sr
        crb = jnp.broadcast_to(cr, (SUBLANES, ns))
        cib = jnp.broadcast_to(ci, (SUBLANES, ns))
        xr, xi = xr + pre * crb - pim * cib, xi + pre * cib + pim * crb
        s_ref[pl.ds(off, SUBLANES), 0:ns] = xr
        s_ref[pl.ds(off, SUBLANES), ns:2 * ns] = xi
        return xr[SUBLANES - 1:SUBLANES, :], xi[SUBLANES - 1:SUBLANES, :]

    zero = jnp.zeros((1, ns), F32)
    lax.fori_loop(0, L // SUBLANES, block, (zero, zero))

    y = jnp.dot(s_ref[...].astype(BF16), c_ref[0], preferred_element_type=F32)
    y_ref[0] = y + d_ref[0] * u


def _s5(proj_a, b_mat, c_mat, pow_re, pow_im, d_skip):
    bsz, L, _ = proj_a.shape
    ns = S5_BLOCK_STATES
    return pl.pallas_call(
        _s5_kernel,
        out_shape=jax.ShapeDtypeStruct((bsz, L, S5_WIDTH), F32),
        grid=(bsz, S5_BLOCKS),
        in_specs=[
            pl.BlockSpec((1, L, 128), lambda b, g: (b, 0, g)),
            pl.BlockSpec((1, 128, 2 * ns), lambda b, g: (g, 0, 0)),
            pl.BlockSpec((1, 2 * ns, 128), lambda b, g: (g, 0, 0)),
            pl.BlockSpec((1, SUBLANES, ns), lambda b, g: (g, 0, 0)),
            pl.BlockSpec((1, SUBLANES, ns), lambda b, g: (g, 0, 0)),
            pl.BlockSpec((1, 1, 128), lambda b, g: (g, 0, 0)),
        ],
        out_specs=pl.BlockSpec((1, L, 128), lambda b, g: (b, 0, g)),
        scratch_shapes=[pltpu.VMEM((L, 2 * ns), F32)],
        compiler_params=_params(2, 40),
        name="s5_scan",
    )(proj_a, b_mat, c_mat, pow_re, pow_im, d_skip.reshape(S5_BLOCKS, 1, 128))


def _glu_kernel(y_ref, z_ref, w_ref, b_ref, o_ref):
    g = jax.nn.gelu(y_ref[...])
    t = jnp.dot(g.astype(BF16), w_ref[...].astype(BF16), preferred_element_type=F32) + b_ref[...]
    z = z_ref[...]
    gate = z / (1.0 + jnp.exp(-z))
    o_ref[...] = (g / (1.0 + jnp.exp(-t)) * gate).astype(o_ref.dtype)


def _glu(y, proj_a, glu_w, glu_b, tm=512):
    m = y.shape[0]
    return pl.pallas_call(
        _glu_kernel,
        out_shape=jax.ShapeDtypeStruct((m, S5_WIDTH), BF16),
        grid=(m // tm,),
        in_specs=[pl.BlockSpec((tm, S5_WIDTH), lambda i: (i, 0)),
                  pl.BlockSpec((tm, S5_WIDTH), lambda i: (i, 1)),
                  pl.BlockSpec((S5_WIDTH, S5_WIDTH), lambda i: (0, 0)),
                  pl.BlockSpec((1, S5_WIDTH), lambda i: (0, 0))],
        out_specs=pl.BlockSpec((tm, S5_WIDTH), lambda i: (i, 0)),
        compiler_params=_params(1, 40),
        name="s5_glu",
    )(y, proj_a, glu_w, glu_b.reshape(1, S5_WIDTH))


DIL_UNROLL = 4
MERGE_ROWS = 256


def _dil_kernel(*refs, L):
    qkv_refs = refs[:3 * N_DIL]
    z_ref, out_ref, og_ref, lg_ref = refs[3 * N_DIL:]
    n_tiles = L // Q_TILE

    for g, (window, dil) in enumerate(DIL_PATTERNS):
        q_ref, k_ref, v_ref = qkv_refs[3 * g:3 * g + 3]
        nb = (L // dil) // Q_TILE
        cw = Q_TILE if nb == 1 else 2 * Q_TILE
        half = (window // dil) // 2

        def rows(start, size, dil=dil):
            return pl.ds(start, size) if dil == 1 else pl.ds(start, size, stride=dil)

        def block(it, g=g, q_ref=q_ref, k_ref=k_ref, v_ref=v_ref, nb=nb, cw=cw, half=half,
                  dil=dil, rows=rows):
            r = it // nb
            qb = it % nb
            kb = jnp.clip(qb - 1, 0, max(nb - 2, 0))
            q0 = r + qb * (Q_TILE * dil)
            k0 = r + kb * (Q_TILE * dil)
            q = q_ref[0, rows(q0, Q_TILE), :].astype(BF16)
            k = k_ref[0, rows(k0, cw), :].astype(BF16)
            v = v_ref[0, rows(k0, cw), :].astype(BF16)
            qpos = qb * Q_TILE + lax.broadcasted_iota(jnp.int32, (Q_TILE, cw), 0)
            kpos = kb * Q_TILE + lax.broadcasted_iota(jnp.int32, (Q_TILE, cw), 1)
            valid = jnp.abs(qpos - kpos - half) <= half
            s = lax.dot_general(q, k, (((1,), (1,)), ((), ())),
                                preferred_element_type=F32) * (HEAD_DIM ** -0.5)
            s = jnp.where(valid, s, MASK_NEG)
            m = jnp.max(s, axis=1, keepdims=True)
            p = jnp.exp(s - m)
            l = jnp.sum(p, axis=1, keepdims=True)
            o = jnp.dot(p.astype(BF16), v, preferred_element_type=F32) / l
            og_ref[g, rows(q0, Q_TILE), :] = o
            lg_ref[g, rows(q0, Q_TILE), :] = jnp.broadcast_to(m + jnp.log(l),
                                                             (Q_TILE, HEAD_DIM))

        def unrolled(j, _, block=block):
            for u in range(DIL_UNROLL):
                block(j * DIL_UNROLL + u)
            return 0

        lax.fori_loop(0, n_tiles // DIL_UNROLL, unrolled, 0)

    def merge(j, _):
        rs = pl.ds(pl.multiple_of(j * MERGE_ROWS, MERGE_ROWS), MERGE_ROWS)
        a = [lg_ref[g, rs, :] for g in range(N_DIL)]
        m = functools.reduce(jnp.maximum, a)
        e = [jnp.exp(x - m) for x in a]
        o = sum(e[g] * og_ref[g, rs, :] for g in range(N_DIL)) / sum(e)
        z = z_ref[0, rs, :]
        out_ref[0, rs, :] = (o * (z / (1.0 + jnp.exp(-z)))).astype(out_ref.dtype)
        return 0

    lax.fori_loop(0, L // MERGE_ROWS, merge, 0)


def _dilated_attention(qkv, proj_b):
    bsz, L, _ = qkv.shape
    for window, dil in DIL_PATTERNS:
        assert (L // dil) % Q_TILE == 0 and (window // dil) % 2 == 0
    assert (L // Q_TILE) % DIL_UNROLL == 0 and L % MERGE_ROWS == 0

    def in_spec(g, kind):
        return pl.BlockSpec((1, L, HEAD_DIM),
                            lambda b, h: (b, 0, (g * 3 + kind) * DIL_HEADS + h))

    head_spec = pl.BlockSpec((1, L, HEAD_DIM), lambda b, h: (b, 0, h))
    return pl.pallas_call(
        functools.partial(_dil_kernel, L=L),
        out_shape=jax.ShapeDtypeStruct((bsz, L, DIL_WIDTH), BF16),
        grid=(bsz, DIL_HEADS),
        in_specs=[in_spec(g, kind) for g in range(N_DIL) for kind in range(3)] + [head_spec],
        out_specs=head_spec,
        scratch_shapes=[pltpu.VMEM((N_DIL, L, HEAD_DIM), F32),
                        pltpu.VMEM((N_DIL, L, HEAD_DIM), F32)],
        compiler_params=_params(2, 48),
        name="dilated_attention",
    )(*([qkv] * (3 * N_DIL)), proj_b)


def _even_layer(x2d, bsz, L, norm, w_in, lam_re, lam_im, log_step, b_re, b_im, c_re, c_im,
                d_skip, glu_w, glu_b, w_out, tables):
    h = _rmsnorm(x2d, norm, BF16)
    proj_a = _matmul([h], [w_in], 2 * S5_WIDTH, F32, w_col0=0, name="even_in_a")
    qkv = _matmul([h], [w_in], QKV_WIDTH, F32, w_col0=2 * S5_WIDTH, epilogue="rope_qk",
                  extra=tables, seq_len=L, name="even_in_qkv")
    proj_b = _matmul([h], [w_in], DIL_WIDTH, F32, w_col0=2 * S5_WIDTH + QKV_WIDTH,
                     name="even_in_zb")

    b_mat, c_mat, pow_re, pow_im = _s5_discretise(lam_re, lam_im, log_step, b_re, b_im,
                                                  c_re, c_im)
    y = _s5(proj_a.reshape(bsz, L, 2 * S5_WIDTH), b_mat, c_mat, pow_re, pow_im, d_skip)
    a_out = _glu(y.reshape(bsz * L, S5_WIDTH), proj_a, glu_w, glu_b)

    b_out = _dilated_attention(qkv.reshape(bsz, L, QKV_WIDTH),
                               proj_b.reshape(bsz, L, DIL_WIDTH)).reshape(bsz * L, DIL_WIDTH)

    return _matmul([a_out, b_out], [w_out, w_out], D_MODEL, F32, w_row_blocks=[0, 1],
                   res=x2d, name="even_out")


def _odd_prep_kernel(p_ref, qn_ref, kvn_ref, kin_ref, c_ref, sa_ref, sb_ref,
                     cq_ref, ckv_ref, kr_ref, ki_ref, wi_ref):
    def norm(x, g):
        ms = jnp.mean(x * x, axis=-1, keepdims=True)
        return x * lax.rsqrt(ms + EPS) * g

    c, sa, sb = c_ref[...], sa_ref[...], sb_ref[...]
    o = 0
    cq_ref[...] = norm(p_ref[:, o:o + Q_RANK], qn_ref[...]).astype(cq_ref.dtype)
    o += Q_RANK
    ckv_ref[...] = norm(p_ref[:, o:o + KV_RANK], kvn_ref[...]).astype(ckv_ref.dtype)
    o += KV_RANK
    kr_ref[...] = _rope_head(p_ref[:, o:o + 128], c, sa, sb)
    o += 128
    ki_ref[...] = _rope_head(norm(p_ref[:, o:o + 128], kin_ref[...]), c, sa, sb
                             ).astype(ki_ref.dtype)
    o += 128
    wi_ref[...] = p_ref[:, o:o + 128] * (IDX_HEADS ** -0.5)


def _odd_prep(p_small, q_norm, kv_norm, k_idx_norm, tables, L, tm=256):
    m = p_small.shape[0]
    tps = L // tm
    row = lambda w: pl.BlockSpec((tm, w), lambda i: (i, 0))
    tab = pl.BlockSpec((tm, 128), lambda i: (i % tps, 0))
    vec = lambda w: pl.BlockSpec((1, w), lambda i: (0, 0))
    return pl.pallas_call(
        _odd_prep_kernel,
        out_shape=[jax.ShapeDtypeStruct((m, Q_RANK), BF16),
                   jax.ShapeDtypeStruct((m, KV_RANK), BF16),
                   jax.ShapeDtypeStruct((m, 128), F32),
                   jax.ShapeDtypeStruct((m, 128), BF16),
                   jax.ShapeDtypeStruct((m, 128), F32)],
        grid=(m // tm,),
        in_specs=[row(ODD_SMALL_PAD), vec(Q_RANK), vec(KV_RANK), vec(IDX_DIM), tab, tab, tab],
        out_specs=[row(Q_RANK), row(KV_RANK), row(128), row(128), row(128)],
        compiler_params=_params(1, 32),
        name="odd_prep",
    )(p_small, q_norm.reshape(1, -1), kv_norm.reshape(1, -1), k_idx_norm.reshape(1, -1),
      *tables)


def _float_sort_key(s):
    b = pltpu.bitcast(s, jnp.int32)
    return b ^ ((b >> 31) & 0x7FFFFFFF)


IDX_SUB = 256


def _indexer_kernel(q_ref, k_ref, w_ref, bias_ref, key_ref, jmax_ref, *, L, topk, kc):
    i = pl.program_id(1)
    nch = L // kc
    n_vis = ((i + 1) * Q_TILE + kc - 1) // kc
    qpos = i * Q_TILE + lax.broadcasted_iota(jnp.int32, (Q_TILE, 1), 0)
    lane = lax.broadcasted_iota(jnp.int32, (Q_TILE, kc), 1)
    lane_sub = lax.broadcasted_iota(jnp.int32, (Q_TILE, IDX_SUB), 1)

    def chunk(c, _):
        for part in range(kc // IDX_SUB):
            k0 = pl.multiple_of(c * kc + part * IDX_SUB, IDX_SUB)
            kblk = k_ref[0, pl.ds(k0, IDX_SUB), :]
            acc = jnp.zeros((Q_TILE, IDX_SUB), F32)
            for h in range(IDX_HEADS):
                qh = q_ref[0, :, h * IDX_DIM:(h + 1) * IDX_DIM]
                s = lax.dot_general(qh, kblk, (((1,), (1,)), ((), ())),
                                    preferred_element_type=F32)
                acc = acc + w_ref[0, :, h:h + 1] * jnp.maximum(s, 0.0)
            acc = jnp.where(acc == 0.0, 0.0, acc)
            key_ref[c, :, part * IDX_SUB:(part + 1) * IDX_SUB] = jnp.where(
                lane_sub + k0 <= qpos, _float_sort_key(acc), INT_MIN)
        return 0

    lax.fori_loop(0, n_vis, chunk, 0)

    def count(pred):
        def body(c, tot):
            hit = jnp.where(pred(key_ref[c], c), 1.0, 0.0)
            for j in range(kc // LANES):
                tot = tot + hit[:, j * LANES:(j + 1) * LANES]
            return tot

        tot = lax.fori_loop(0, n_vis, body, jnp.zeros((Q_TILE, LANES), F32))
        return jnp.sum(tot, axis=1, keepdims=True)

    def tbit(b, t_u):
        cand_u = t_u | jnp.left_shift(jnp.int32(1), 31 - b)
        cand_s = cand_u ^ INT_MIN
        cnt = count(lambda key, c: key >= cand_s)
        return jnp.where(cnt >= topk, cand_u, t_u)

    t_u = lax.fori_loop(0, 32, tbit, jnp.zeros((Q_TILE, 1), jnp.int32))
    thr = t_u ^ INT_MIN

    need = topk - count(lambda key, c: key > thr)
    n_eq = count(lambda key, c: key == thr)
    jbits = max(1, (L - 1).bit_length())
    jmax_ref[...] = jnp.full((Q_TILE, 1), (1 << jbits) - 1, jnp.int32)

    unsettled = jnp.where(thr == INT_MIN, 0.0, jnp.abs(n_eq - need))

    @pl.when(jnp.max(unsettled) > 0.0)
    def _():
        def jbit(b, j):
            cand = j | jnp.left_shift(jnp.int32(1), jbits - 1 - b)
            cnt = count(lambda key, c: jnp.where(lane + c * kc < cand, key, thr - 1) == thr)
            return jnp.where(cnt < need, cand, j)

        jmax_ref[...] = lax.fori_loop(0, jbits, jbit, jnp.zeros((Q_TILE, 1), jnp.int32))

    jlim = jnp.minimum(jmax_ref[...], qpos)
    for c in range(nch):
        @pl.when(c < n_vis)
        def _(c=c):
            key = key_ref[c]
            tie = jnp.where(lane + c * kc <= jlim, 0.0, MASK_NEG)
            bias = jnp.where(key > thr, 0.0, jnp.where(key == thr, tie, MASK_NEG))
            bias_ref[0, 0, c] = bias.astype(bias_ref.dtype)

        @pl.when(c >= n_vis)
        def _(c=c):
            bias_ref[0, 0, c] = jnp.full((Q_TILE, kc), MASK_NEG, bias_ref.dtype)


def _indexer(q_idx, k_idx, w_idx, topk, kc):
    bsz, L, _ = q_idx.shape
    nq, nch = L // Q_TILE, L // kc
    return pl.pallas_call(
        functools.partial(_indexer_kernel, L=L, topk=topk, kc=kc),
        out_shape=jax.ShapeDtypeStruct((bsz, nq, nch, Q_TILE, kc), BF16),
        grid=(bsz, nq),
        in_specs=[pl.BlockSpec((1, Q_TILE, IDX_HEADS * IDX_DIM), lambda b, i: (b, i, 0)),
                  pl.BlockSpec((1, L, IDX_DIM), lambda b, i: (b, 0, 0)),
                  pl.BlockSpec((1, Q_TILE, 128), lambda b, i: (b, i, 0))],
        out_specs=pl.BlockSpec((1, 1, nch, Q_TILE, kc), lambda b, i: (b, i, 0, 0, 0)),
        scratch_shapes=[pltpu.VMEM((nch, Q_TILE, kc), jnp.int32),
                        pltpu.VMEM((Q_TILE, 1), jnp.int32)],
        compiler_params=_params(2, 32),
        name="dsa_indexer",
    )(q_idx, k_idx, w_idx)


def _mla_kernel(q_ref, k_ref, v_ref, bias_ref, g_ref, o_ref, s_ref, mx_ref, ls_ref, acc_ref,
                *, kc, nh):
    i = pl.program_id(2)
    n_vis = ((i + 1) * Q_TILE + kc - 1) // kc
    slabs = kc // LANES

    def fold(x, op):
        t = x[:, 0:LANES]
        for j in range(1, slabs):
            t = op(t, x[:, j * LANES:(j + 1) * LANES])
        return t

    mx_ref[...] = jnp.full(mx_ref.shape, MASK_NEG, F32)

    def scores(c, _):
        k0 = pl.multiple_of(c * kc, kc)
        bias = bias_ref[0, 0, c].astype(F32)
        for h in range(nh):
            hs = slice(h * HEAD_DIM, (h + 1) * HEAD_DIM)
            s = lax.dot_general(q_ref[0, :, hs], k_ref[0, pl.ds(k0, kc), hs],
                                (((1,), (1,)), ((), ())), preferred_element_type=F32) + bias
            s_ref[c, h] = s
            mx_ref[h] = jnp.maximum(mx_ref[h], fold(s, jnp.maximum))
        return 0

    lax.fori_loop(0, n_vis, scores, 0)

    for h in range(nh):
        m = jnp.max(mx_ref[h], axis=1, keepdims=True)
        mx_ref[h] = jnp.broadcast_to(m, (Q_TILE, LANES))
    ls_ref[...] = jnp.zeros(ls_ref.shape, F32)
    acc_ref[...] = jnp.zeros(acc_ref.shape, F32)

    def weights(c, _):
        k0 = pl.multiple_of(c * kc, kc)
        for h in range(nh):
            hs = slice(h * HEAD_DIM, (h + 1) * HEAD_DIM)
            m = mx_ref[h]
            s = s_ref[c, h]
            p = jnp.concatenate(
                [jnp.exp2(s[:, j * LANES:(j + 1) * LANES] - m) for j in range(slabs)], axis=1)
            ls_ref[h] += fold(p, jnp.add)
            acc_ref[h] += jnp.dot(p.astype(BF16), v_ref[0, pl.ds(k0, kc), hs],
                                  preferred_element_type=F32)
        return 0

    lax.fori_loop(0, n_vis, weights, 0)

    for h in range(nh):
        hs = slice(h * HEAD_DIM, (h + 1) * HEAD_DIM)
        g = g_ref[0, :, hs].astype(F32)
        o = acc_ref[h] / jnp.sum(ls_ref[h], axis=1, keepdims=True)
        o_ref[0, :, hs] = (o * (g / (1.0 + jnp.exp(-g)))).astype(o_ref.dtype)


def _masked_attention(q, k, v, bias, gate, kc, nh=8):
    bsz, L, _ = q.shape
    nq, nch = L // Q_TILE, L // kc
    w = nh * HEAD_DIM
    qspec = pl.BlockSpec((1, Q_TILE, w), lambda b, g, i: (b, i, g))
    kvspec = pl.BlockSpec((1, L, w), lambda b, g, i: (b, 0, g))
    return pl.pallas_call(
        functools.partial(_mla_kernel, kc=kc, nh=nh),
        out_shape=jax.ShapeDtypeStruct((bsz, L, C_WIDTH), BF16),
        grid=(bsz, MLA_HEADS // nh, nq),
        in_specs=[qspec, kvspec, kvspec,
                  pl.BlockSpec((1, 1, nch, Q_TILE, kc), lambda b, g, i: (b, i, 0, 0, 0)),
                  qspec],
        out_specs=qspec,
        scratch_shapes=[pltpu.VMEM((nch, nh, Q_TILE, kc), F32),
                        pltpu.VMEM((nh, Q_TILE, LANES), F32),
                        pltpu.VMEM((nh, Q_TILE, LANES), F32),
                        pltpu.VMEM((nh, Q_TILE, HEAD_DIM), F32)],
        compiler_params=_params(3, 48),
        name="dsa_attention",
    )(q, k, v, bias, gate)


def _odd_layer(x2d, bsz, L, norm, w_in, q_norm, kv_norm, k_idx_norm, w_uq, w_uk, w_uv, w_iq,
               w_out, tables):
    m = bsz * L
    h = _rmsnorm(x2d, norm, BF16)

    c0 = Q_RANK + KV_RANK
    zpad = lambda w: jnp.zeros((D_MODEL, w), F32)
    w_small = jnp.concatenate([
        w_in[:, :c0],
        w_in[:, c0:c0 + ROT_DIM], zpad(128 - ROT_DIM),
        w_in[:, c0 + ROT_DIM:c0 + ROT_DIM + IDX_DIM],
        w_in[:, c0 + ROT_DIM + IDX_DIM:c0 + ROT_DIM + IDX_DIM + IDX_HEADS], zpad(128 - IDX_HEADS),
        zpad(ODD_SMALL_PAD - ODD_SMALL)], axis=1)
    w_gate = w_in[:, c0 + ROT_DIM + IDX_DIM + IDX_HEADS:]

    p_small = _matmul([h], [w_small], ODD_SMALL_PAD, F32, name="odd_in_small")
    gate = _matmul([h], [w_gate], C_WIDTH, BF16, name="odd_in_gate")
    c_q, c_kv, k_rope, k_idx, w_idx = _odd_prep(p_small, q_norm, kv_norm, k_idx_norm, tables, L)

    q = _matmul([c_q], [w_uq], C_WIDTH, BF16, epilogue="rope", extra=tables, seq_len=L,
                scale=LOG2E * HEAD_DIM ** -0.5, name="odd_q_up")
    q_idx = _matmul([c_q], [w_iq], IDX_HEADS * IDX_DIM, BF16, epilogue="rope", extra=tables,
                    seq_len=L, scale=IDX_DIM ** -0.5, name="odd_q_idx")

    w_k = jnp.pad(w_uk, ((0, 0), (0, 0), (ROT_DIM, 0))).reshape(KV_RANK, C_WIDTH)
    k = _matmul([c_kv], [w_k], C_WIDTH, BF16, epilogue="add_head", extra=(k_rope,),
                name="odd_k_up")
    v = _matmul([c_kv], [w_uv.reshape(KV_RANK, C_WIDTH)], C_WIDTH, BF16, name="odd_v_up")

    topk = min(IDX_TOPK_MAX, L // 4)
    kc = min(512, L)
    bias = _indexer(q_idx.reshape(bsz, L, -1), k_idx.reshape(bsz, L, IDX_DIM),
                    w_idx.reshape(bsz, L, 128), topk, kc)
    o = _masked_attention(q.reshape(bsz, L, C_WIDTH), k.reshape(bsz, L, C_WIDTH),
                          v.reshape(bsz, L, C_WIDTH), bias, gate.reshape(bsz, L, C_WIDTH), kc)
    return _matmul([o.reshape(m, C_WIDTH)], [w_out], D_MODEL, F32, res=x2d, name="odd_out")


def kernel(x, even_norm, even_w_in, s5_lambda_re, s5_lambda_im, s5_log_step, s5_b_re, s5_b_im,
           s5_c_re, s5_c_im, s5_d, s5_glu_w, s5_glu_b, even_w_out, odd_norm, odd_w_in,
           mla_q_norm, mla_kv_norm, idx_k_norm, mla_w_uq, mla_w_uk, mla_w_uv, idx_w_q,
           odd_w_out, final_norm):
    bsz, L, d = x.shape
    depth = even_norm.shape[0] + odd_norm.shape[0]
    tables = _rope_tables(L)
    x2d = x.reshape(bsz * L, d)
    for layer in range(depth):
        i = layer // 2
        if layer % 2 == 0:
            x2d = _even_layer(x2d, bsz, L, even_norm[i], even_w_in[i], s5_lambda_re[i],
                              s5_lambda_im[i], s5_log_step[i], s5_b_re[i], s5_b_im[i],
                              s5_c_re[i], s5_c_im[i], s5_d[i], s5_glu_w[i], s5_glu_b[i],
                              even_w_out[i], tables)
        else:
            x2d = _odd_layer(x2d, bsz, L, odd_norm[i], odd_w_in[i], mla_q_norm[i],
                             mla_kv_norm[i], idx_k_norm[i], mla_w_uq[i], mla_w_uk[i],
                             mla_w_uv[i], idx_w_q[i], odd_w_out[i], tables)
    return _rmsnorm(x2d, final_norm, F32).reshape(bsz, L, d)
```

```python
import functools
import math

import jax
import jax.numpy as jnp
from jax import lax
from jax.experimental import pallas as pl
from jax.experimental.pallas import tpu as pltpu

F32 = jnp.float32
BF16 = jnp.bfloat16

LANES = 128
SUBLANES = 8

D_MODEL = 4096
EPS = 1e-6
ROPE_THETA = 500000.0
HEAD_DIM = 128
ROT_DIM = HEAD_DIM // 4
ROT_HALF = ROT_DIM // 2

S5_WIDTH = D_MODEL // 4
S5_GROUP = 16
S5_GROUPS = S5_WIDTH // S5_GROUP
S5_STATE = 64
S5_BLOCKS = S5_WIDTH // 128
S5_BLOCK_GROUPS = S5_GROUPS // S5_BLOCKS
S5_BLOCK_STATES = S5_BLOCK_GROUPS * S5_STATE

DIL_PATTERNS = ((128, 1), (512, 4), (2048, 16))
N_DIL = len(DIL_PATTERNS)
DIL_HEADS = D_MODEL // 512
DIL_WIDTH = DIL_HEADS * HEAD_DIM
QKV_WIDTH = 3 * N_DIL * DIL_WIDTH

MLA_HEADS = D_MODEL // HEAD_DIM
MLA_NOPE = HEAD_DIM - ROT_DIM
Q_RANK = 1536
KV_RANK = 512
IDX_HEADS = D_MODEL // 128
IDX_DIM = 128
IDX_TOPK_MAX = 256
C_WIDTH = MLA_HEADS * HEAD_DIM
ODD_GATE0 = Q_RANK + KV_RANK + ROT_DIM + IDX_DIM + IDX_HEADS
ODD_SMALL = Q_RANK + KV_RANK + 2 * LANES
IDX_W_LANE0 = ROT_DIM + IDX_DIM - LANES

Q_TILE = 128
IDX_ROWS = 256
MASK_NEG = -1e30
LOG2E = 1.4426950408889634
INT_MIN = -(2 ** 31)


def _params(n_grid, vmem_mb):
    return pltpu.CompilerParams(
        dimension_semantics=("arbitrary",) * n_grid,
        vmem_limit_bytes=vmem_mb * 1024 * 1024)


def _rope_tables(L):
    inv = ROPE_THETA ** (-jnp.arange(0, ROT_DIM, 2, dtype=F32) / ROT_DIM)
    ang = jnp.arange(L, dtype=F32)[:, None] * inv[None, :]
    cos, sin = jnp.cos(ang), jnp.sin(ang)
    ones = jnp.ones((L, HEAD_DIM - ROT_DIM), F32)
    zeros = jnp.zeros((L, HEAD_DIM - ROT_DIM), F32)
    zh = jnp.zeros((L, ROT_HALF), F32)
    c = jnp.concatenate([cos, cos, ones], axis=1)
    sa = jnp.concatenate([-sin, zh, zeros], axis=1)
    sb = jnp.concatenate([zh, sin, zeros], axis=1)
    return c, sa, sb


def _rope_head(x, c, sa, sb):
    return (x * c + pltpu.roll(x, HEAD_DIM - ROT_HALF, 1) * sa
            + pltpu.roll(x, ROT_HALF, 1) * sb)


def _rmsnorm_kernel(x_ref, g_ref, o_ref):
    x = x_ref[...]
    ms = jnp.mean(x * x, axis=-1, keepdims=True)
    o_ref[...] = (x * lax.rsqrt(ms + EPS) * g_ref[...]).astype(o_ref.dtype)


def _rmsnorm(x, g, out_dtype, tm=256):
    m, d = x.shape
    return pl.pallas_call(
        _rmsnorm_kernel,
        out_shape=jax.ShapeDtypeStruct((m, d), out_dtype),
        grid=(m // tm,),
        in_specs=[pl.BlockSpec((tm, d), lambda i: (i, 0)),
                  pl.BlockSpec((1, d), lambda i: (0, 0))],
        out_specs=pl.BlockSpec((tm, d), lambda i: (i, 0)),
        compiler_params=_params(1, 32),
        name="rmsnorm",
    )(x, g.reshape(1, d))


def _mm_kernel(*refs, n_a, has_res, epilogue, scale, tiles_per_kind, w_transposed):
    a_refs = refs[:n_a]
    w_refs = refs[n_a:2 * n_a]
    pos = 2 * n_a
    res_ref = None
    if has_res:
        res_ref = refs[pos]
        pos += 1
    extra = refs[pos:-1]
    o_ref = refs[-1]

    acc = None
    for a_ref, w_ref in zip(a_refs, w_refs):
        contract = (((1,), (1 if w_transposed else 0,)), ((), ()))
        d = lax.dot_general(a_ref[...], w_ref[...].astype(BF16), contract,
                            preferred_element_type=F32)
        acc = d if acc is None else acc + d
    if has_res:
        acc = acc + res_ref[...]
    n_heads = acc.shape[1] // HEAD_DIM

    def roped():
        c, sa, sb = extra[0][...], extra[1][...], extra[2][...]
        cols = []
        for h in range(n_heads):
            y = _rope_head(acc[:, h * HEAD_DIM:(h + 1) * HEAD_DIM], c, sa, sb)
            cols.append(y * scale if scale != 1.0 else y)
        return jnp.concatenate(cols, axis=1)

    if epilogue == "none":
        o_ref[...] = acc.astype(o_ref.dtype)
    elif epilogue == "rope":
        o_ref[...] = roped().astype(o_ref.dtype)
    elif epilogue == "rope_qk":
        kind = (pl.program_id(1) // tiles_per_kind) % 3

        @pl.when(kind == 2)
        def _():
            o_ref[...] = acc.astype(o_ref.dtype)

        @pl.when(kind != 2)
        def _():
            o_ref[...] = roped().astype(o_ref.dtype)
    elif epilogue == "add_head":
        kr = extra[0][...]
        cols = [acc[:, h * HEAD_DIM:(h + 1) * HEAD_DIM] + kr for h in range(n_heads)]
        o_ref[...] = jnp.concatenate(cols, axis=1).astype(o_ref.dtype)
    else:
        raise ValueError(epilogue)


def _matmul(a_list, w_list, n, out_dtype, *, w_col0=0, w_row_blocks=None, res=None,
            epilogue="none", extra=(), scale=1.0, seq_len=None, w_transposed=False,
            tn=512, vmem_mb=52, name="matmul"):
    m = a_list[0].shape[0]
    deep = sum(a.shape[1] for a in a_list) >= D_MODEL
    tm = 1024 if deep else 2048
    tm = min(tm, m)
    assert m % tm == 0 and n % tn == 0
    n_a = len(a_list)
    if w_row_blocks is None:
        w_row_blocks = [0] * n_a
    in_specs, args = [], []
    for a in a_list:
        k = a.shape[1]
        mode = {} if deep else {"pipeline_mode": pl.Buffered(1)}
        in_specs.append(pl.BlockSpec((tm, k), lambda i, j: (i, 0), **mode))
        args.append(a)
    for a, w, rb in zip(a_list, w_list, w_row_blocks):
        k = a.shape[1]
        if not w_transposed:
            assert w_col0 % tn == 0
            in_specs.append(pl.BlockSpec((k, tn),
                                         lambda i, j, rb=rb: (rb, j + w_col0 // tn)))
        elif w_col0 % tn == 0:
            in_specs.append(pl.BlockSpec((tn, k), lambda i, j: (j + w_col0 // tn, 0)))
        else:
            g = math.gcd(w_col0, tn)
            assert g % SUBLANES == 0
            in_specs.append(pl.BlockSpec(
                (pl.Element(tn), pl.Element(k)),
                lambda i, j, g=g: ((w_col0 // g + j * (tn // g)) * g, 0)))
        args.append(w)
    if res is not None:
        in_specs.append(pl.BlockSpec((tm, tn), lambda i, j: (i, j)))
        args.append(res)
    tiles_per_seq = None if seq_len is None else seq_len // tm
    for e in extra:
        if e.shape[0] == m:
            in_specs.append(pl.BlockSpec((tm, e.shape[1]), lambda i, j: (i, 0)))
        else:
            assert seq_len is not None and e.shape[0] == seq_len and seq_len % tm == 0
            in_specs.append(pl.BlockSpec((tm, e.shape[1]),
                                         lambda i, j: (i % tiles_per_seq, 0)))
        args.append(e)
    kern = functools.partial(_mm_kernel, n_a=n_a, has_res=res is not None,
                             epilogue=epilogue, scale=scale,
                             tiles_per_kind=DIL_WIDTH // tn, w_transposed=w_transposed)
    return pl.pallas_call(
        kern,
        out_shape=jax.ShapeDtypeStruct((m, n), out_dtype),
        grid=(m // tm, n // tn),
        in_specs=in_specs,
        out_specs=pl.BlockSpec((tm, tn), lambda i, j: (i, j)),
        compiler_params=_params(2, vmem_mb),
        name=name,
    )(*args)


def _s5_discretise(lam_re, lam_im, log_step, b_re, b_im, c_re, c_im):
    lr = jnp.minimum(lam_re.astype(F32), -1e-4)
    li = lam_im.astype(F32)
    dt = jnp.exp(log_step.astype(F32))[:, None]
    mag = jnp.exp(lr * dt)
    ar = mag * jnp.cos(li * dt)
    ai = mag * jnp.sin(li * dt)
    den = lr * lr + li * li
    nr, ni = ar - 1.0, ai
    cr = (nr * lr + ni * li) / den
    ci = (ni * lr - nr * li) / den
    br32, bi32 = b_re.astype(F32), b_im.astype(F32)
    bbr = cr[..., None] * br32 - ci[..., None] * bi32
    bbi = cr[..., None] * bi32 + ci[..., None] * br32

    eye = jnp.eye(S5_BLOCK_GROUPS, dtype=F32)

    def in_blockdiag(t):
        t = t.reshape(S5_BLOCKS, S5_BLOCK_GROUPS, S5_STATE, S5_GROUP)
        t = jnp.einsum("bgpj,gh->bgjhp", t, eye)
        return t.reshape(S5_BLOCKS, 128, S5_BLOCK_STATES)

    def out_blockdiag(t):
        t = t.reshape(S5_BLOCKS, S5_BLOCK_GROUPS, S5_GROUP, S5_STATE)
        t = jnp.einsum("bgjp,gh->bgphj", t, eye)
        return t.reshape(S5_BLOCKS, S5_BLOCK_STATES, 128)

    b_mat = jnp.concatenate([in_blockdiag(bbr), in_blockdiag(bbi)], axis=2).astype(BF16)
    c_mat = jnp.concatenate([out_blockdiag(c_re.astype(F32)),
                             -out_blockdiag(c_im.astype(F32))], axis=1).astype(BF16)

    a_r = ar.reshape(S5_BLOCKS, S5_BLOCK_STATES)
    a_i = ai.reshape(S5_BLOCKS, S5_BLOCK_STATES)
    pr, pi = [a_r], [a_i]
    for _ in range(SUBLANES - 1):
        pr.append(pr[-1] * a_r - pi[-1] * a_i)
        pi.append(pr[-2] * a_i + pi[-1] * a_r)
    pow_re = jnp.stack(pr, axis=1)
    pow_im = jnp.stack(pi, axis=1)
    return b_mat, c_mat, pow_re, pow_im


def _s5_kernel(u_ref, b_ref, c_ref, pre_ref, pim_ref, d_ref, y_ref, s_ref):
    L = u_ref.shape[1]
    ns = S5_BLOCK_STATES
    u = u_ref[0]
    s_ref[...] = jnp.dot(u.astype(BF16), b_ref[0], preferred_element_type=F32)

    row = lax.broadcasted_iota(jnp.int32, (SUBLANES, ns), 0)
    pre = pre_ref[0]
    pim = pim_ref[0]

    def bcast(v, r):
        return jnp.broadcast_to(v[r:r + 1, :], (SUBLANES, ns))

    steps = tuple((d, jnp.where(row >= d, bcast(pre, d - 1), 0.0),
                   jnp.where(row >= d, bcast(pim, d - 1), 0.0)) for d in (1, 2, 4))

    def block(i, carry):
        cr, ci = carry
        off = pl.multiple_of(i * SUBLANES, SUBLANES)
        xr = s_ref[pl.ds(off, SUBLANES), 0:ns]
        xi = s_ref[pl.ds(off, SUBLANES), ns:2 * ns]
        for d, ar, ai in steps:
            sr = pltpu.roll(xr, d, 0)
            si = pltpu.roll(xi, d, 0)
            xr, xi = xr + ar * sr - ai * si, xi + ar * si + ai * sr
        crb = jnp.broadcast_to(cr, (SUBLANES, ns))
        cib = jnp.broadcast_to(ci, (SUBLANES, ns))
        xr, xi = xr + pre * crb - pim * cib, xi + pre * cib + pim * crb
        s_ref[pl.ds(off, SUBLANES), 0:ns] = xr
        s_ref[pl.ds(off, SUBLANES), ns:2 * ns] = xi
        return xr[SUBLANES - 1:SUBLANES, :], xi[SUBLANES - 1:SUBLANES, :]

    zero = jnp.zeros((1, ns), F32)
    lax.fori_loop(0, L // SUBLANES, block, (zero, zero))

    y = jnp.dot(s_ref[...].astype(BF16), c_ref[0], preferred_element_type=F32)
    y_ref[0] = y + d_ref[0] * u


def _s5(proj_a, b_mat, c_mat, pow_re, pow_im, d_skip):
    bsz, L, _ = proj_a.shape
    ns = S5_BLOCK_STATES
    return pl.pallas_call(
        _s5_kernel,
        out_shape=jax.ShapeDtypeStruct((bsz, L, S5_WIDTH), F32),
        grid=(bsz, S5_BLOCKS),
        in_specs=[
            pl.BlockSpec((1, L, 128), lambda b, g: (b, 0, g)),
            pl.BlockSpec((1, 128, 2 * ns), lambda b, g: (g, 0, 0)),
            pl.BlockSpec((1, 2 * ns, 128), lambda b, g: (g, 0, 0)),
            pl.BlockSpec((1, SUBLANES, ns), lambda b, g: (g, 0, 0)),
            pl.BlockSpec((1, SUBLANES, ns), lambda b, g: (g, 0, 0)),
            pl.BlockSpec((1, 1, 128), lambda b, g: (g, 0, 0)),
        ],
        out_specs=pl.BlockSpec((1, L, 128), lambda b, g: (b, 0, g)),
        scratch_shapes=[pltpu.VMEM((L, 2 * ns), F32)],
        compiler_params=_params(2, 40),
        name="s5_scan",
    )(proj_a, b_mat, c_mat, pow_re, pow_im, d_skip.reshape(S5_BLOCKS, 1, 128))


def _glu_kernel(y_ref, z_ref, w_ref, b_ref, o_ref):
    g = jax.nn.gelu(y_ref[...])
    t = jnp.dot(g.astype(BF16), w_ref[...].astype(BF16), preferred_element_type=F32) + b_ref[...]
    z = z_ref[...]
    gate = z / (1.0 + jnp.exp(-z))
    o_ref[...] = (g / (1.0 + jnp.exp(-t)) * gate).astype(o_ref.dtype)


def _glu(y, proj_a, glu_w, glu_b, tm=512):
    m = y.shape[0]
    return pl.pallas_call(
        _glu_kernel,
        out_shape=jax.ShapeDtypeStruct((m, S5_WIDTH), BF16),
        grid=(m // tm,),
        in_specs=[pl.BlockSpec((tm, S5_WIDTH), lambda i: (i, 0)),
                  pl.BlockSpec((tm, S5_WIDTH), lambda i: (i, 1)),
                  pl.BlockSpec((S5_WIDTH, S5_WIDTH), lambda i: (0, 0)),
                  pl.BlockSpec((1, S5_WIDTH), lambda i: (0, 0))],
        out_specs=pl.BlockSpec((tm, S5_WIDTH), lambda i: (i, 0)),
        compiler_params=_params(1, 40),
        name="s5_glu",
    )(y, proj_a, glu_w, glu_b.reshape(1, S5_WIDTH))


DIL_UNROLL = 4
MERGE_ROWS = 256


def _dil_kernel(*refs, L):
    qkv_refs = refs[:3 * N_DIL]
    z_ref, out_ref, og_ref, lg_ref = refs[3 * N_DIL:]
    n_tiles = L // Q_TILE

    for g, (window, dil) in enumerate(DIL_PATTERNS):
        q_ref, k_ref, v_ref = qkv_refs[3 * g:3 * g + 3]
        nb = (L // dil) // Q_TILE
        cw = Q_TILE if nb == 1 else 2 * Q_TILE
        half = (window // dil) // 2

        def rows(start, size, dil=dil):
            return pl.ds(start, size) if dil == 1 else pl.ds(start, size, stride=dil)

        def block(it, g=g, q_ref=q_ref, k_ref=k_ref, v_ref=v_ref, nb=nb, cw=cw, half=half,
                  dil=dil, rows=rows):
            r = it // nb
            qb = it % nb
            kb = jnp.clip(qb - 1, 0, max(nb - 2, 0))
            q0 = r + qb * (Q_TILE * dil)
            k0 = r + kb * (Q_TILE * dil)
            q = q_ref[0, rows(q0, Q_TILE), :].astype(BF16)
            k = k_ref[0, rows(k0, cw), :].astype(BF16)
            v = v_ref[0, rows(k0, cw), :].astype(BF16)
            qpos = qb * Q_TILE + lax.broadcasted_iota(jnp.int32, (Q_TILE, cw), 0)
            kpos = kb * Q_TILE + lax.broadcasted_iota(jnp.int32, (Q_TILE, cw), 1)
            valid = jnp.abs(qpos - kpos - half) <= half
            s = lax.dot_general(q, k, (((1,), (1,)), ((), ())),
                                preferred_element_type=F32) * (HEAD_DIM ** -0.5)
            s = jnp.where(valid, s, MASK_NEG)
            m = jnp.max(s, axis=1, keepdims=True)
            p = jnp.exp(s - m)
            l = jnp.sum(p, axis=1, keepdims=True)
            o = jnp.dot(p.astype(BF16), v, preferred_element_type=F32) / l
            og_ref[g, rows(q0, Q_TILE), :] = o
            lg_ref[g, rows(q0, Q_TILE), :] = jnp.broadcast_to(m + jnp.log(l),
                                                             (Q_TILE, HEAD_DIM))

        def unrolled(j, _, block=block):
            for u in range(DIL_UNROLL):
                block(j * DIL_UNROLL + u)
            return 0

        lax.fori_loop(0, n_tiles // DIL_UNROLL, unrolled, 0)

    def merge(j, _):
        rs = pl.ds(pl.multiple_of(j * MERGE_ROWS, MERGE_ROWS), MERGE_ROWS)
        a = [lg_ref[g, rs, :] for g in range(N_DIL)]
        m = functools.reduce(jnp.maximum, a)
        e = [jnp.exp(x - m) for x in a]
        o = sum(e[g] * og_ref[g, rs, :] for g in range(N_DIL)) / sum(e)
        z = z_ref[0, rs, :]
        out_ref[0, rs, :] = (o * (z / (1.0 + jnp.exp(-z)))).astype(out_ref.dtype)
        return 0

    lax.fori_loop(0, L // MERGE_ROWS, merge, 0)


def _dilated_attention(qkv, proj_b):
    bsz, L, _ = qkv.shape
    for window, dil in DIL_PATTERNS:
        assert (L // dil) % Q_TILE == 0 and (window // dil) % 2 == 0
    assert (L // Q_TILE) % DIL_UNROLL == 0 and L % MERGE_ROWS == 0

    def in_spec(g, kind):
        return pl.BlockSpec((1, L, HEAD_DIM),
                            lambda b, h: (b, 0, (g * 3 + kind) * DIL_HEADS + h))

    head_spec = pl.BlockSpec((1, L, HEAD_DIM), lambda b, h: (b, 0, h))
    return pl.pallas_call(
        functools.partial(_dil_kernel, L=L),
        out_shape=jax.ShapeDtypeStruct((bsz, L, DIL_WIDTH), BF16),
        grid=(bsz, DIL_HEADS),
        in_specs=[in_spec(g, kind) for g in range(N_DIL) for kind in range(3)] + [head_spec],
        out_specs=head_spec,
        scratch_shapes=[pltpu.VMEM((N_DIL, L, HEAD_DIM), F32),
                        pltpu.VMEM((N_DIL, L, HEAD_DIM), F32)],
        compiler_params=_params(2, 48),
        name="dilated_attention",
    )(*([qkv] * (3 * N_DIL)), proj_b)


def _even_layer(x2d, bsz, L, norm, w_in, lam_re, lam_im, log_step, b_re, b_im, c_re, c_im,
                d_skip, glu_w, glu_b, w_out, tables):
    h = _rmsnorm(x2d, norm, BF16)
    proj_a = _matmul([h], [w_in], 2 * S5_WIDTH, F32, w_col0=0, name="even_in_a")
    qkv = _matmul([h], [w_in], QKV_WIDTH, F32, w_col0=2 * S5_WIDTH, epilogue="rope_qk",
                  extra=tables, seq_len=L, name="even_in_qkv")
    proj_b = _matmul([h], [w_in], DIL_WIDTH, F32, w_col0=2 * S5_WIDTH + QKV_WIDTH,
                     name="even_in_zb")

    b_mat, c_mat, pow_re, pow_im = _s5_discretise(lam_re, lam_im, log_step, b_re, b_im,
                                                  c_re, c_im)
    y = _s5(proj_a.reshape(bsz, L, 2 * S5_WIDTH), b_mat, c_mat, pow_re, pow_im, d_skip)
    a_out = _glu(y.reshape(bsz * L, S5_WIDTH), proj_a, glu_w, glu_b)

    b_out = _dilated_attention(qkv.reshape(bsz, L, QKV_WIDTH),
                               proj_b.reshape(bsz, L, DIL_WIDTH)).reshape(bsz * L, DIL_WIDTH)

    return _matmul([a_out, b_out], [w_out, w_out], D_MODEL, F32, w_row_blocks=[0, 1],
                   res=x2d, name="even_out")


def _odd_prep_kernel(p_ref, qn_ref, kvn_ref, kin_ref, c_ref, sa_ref, sb_ref,
                     cq_ref, ckv_ref, kr_ref, ki_ref, wi_ref):
    def norm(x, g):
        ms = jnp.mean(x * x, axis=-1, keepdims=True)
        return x * lax.rsqrt(ms + EPS) * g

    c, sa, sb = c_ref[...], sa_ref[...], sb_ref[...]
    o = 0
    cq_ref[...] = norm(p_ref[:, o:o + Q_RANK], qn_ref[...]).astype(cq_ref.dtype)
    o += Q_RANK
    ckv_ref[...] = norm(p_ref[:, o:o + KV_RANK], kvn_ref[...]).astype(ckv_ref.dtype)
    o += KV_RANK
    lo = p_ref[:, o:o + LANES]
    hi = p_ref[:, o + LANES:o + 2 * LANES]
    lane = lax.broadcasted_iota(jnp.int32, lo.shape, 1)
    kr_ref[...] = _rope_head(jnp.where(lane < ROT_DIM, lo, 0.0), c, sa, sb)
    k_i = jnp.where(lane < LANES - ROT_DIM, pltpu.roll(lo, LANES - ROT_DIM, 1),
                    pltpu.roll(hi, LANES - ROT_DIM, 1))
    ki_ref[...] = _rope_head(norm(k_i, kin_ref[...]), c, sa, sb).astype(ki_ref.dtype)
    wi_ref[...] = hi * (IDX_HEADS ** -0.5)


def _odd_prep(p_small, q_norm, kv_norm, k_idx_norm, tables, L, tm=256):
    m = p_small.shape[0]
    tps = L // tm
    row = lambda w: pl.BlockSpec((tm, w), lambda i: (i, 0))
    tab = pl.BlockSpec((tm, 128), lambda i: (i % tps, 0))
    vec = lambda w: pl.BlockSpec((1, w), lambda i: (0, 0))
    return pl.pallas_call(
        _odd_prep_kernel,
        out_shape=[jax.ShapeDtypeStruct((m, Q_RANK), BF16),
                   jax.ShapeDtypeStruct((m, KV_RANK), BF16),
                   jax.ShapeDtypeStruct((m, 128), F32),
                   jax.ShapeDtypeStruct((m, 128), BF16),
                   jax.ShapeDtypeStruct((m, 128), F32)],
        grid=(m // tm,),
        in_specs=[row(ODD_SMALL), vec(Q_RANK), vec(KV_RANK), vec(IDX_DIM), tab, tab, tab],
        out_specs=[row(Q_RANK), row(KV_RANK), row(128), row(128), row(128)],
        compiler_params=_params(1, 32),
        name="odd_prep",
    )(p_small, q_norm.reshape(1, -1), kv_norm.reshape(1, -1), k_idx_norm.reshape(1, -1),
      *tables)


def _float_sort_key(s):
    b = pltpu.bitcast(s, jnp.int32)
    return b ^ ((b >> 31) & 0x7FFFFFFF)


def _indexer_kernel(q_ref, k_ref, w_ref, bias_ref, key_ref, jmax_ref, *, L, topk, kc, w_lane0):
    i = pl.program_id(1)
    nch = L // kc
    rows = IDX_ROWS
    n_vis = ((i + 1) * rows + kc - 1) // kc
    qpos = i * rows + lax.broadcasted_iota(jnp.int32, (rows, 1), 0)
    lane = lax.broadcasted_iota(jnp.int32, (rows, kc), 1)
    lane_t = lax.broadcasted_iota(jnp.int32, (Q_TILE, kc), 1)
    qpos_t = lax.broadcasted_iota(jnp.int32, (Q_TILE, 1), 0)

    def chunk(c, _):
        k0 = pl.multiple_of(c * kc, kc)
        kblk = k_ref[0, pl.ds(k0, kc), :]
        for t in range(rows // Q_TILE):
            rs = slice(t * Q_TILE, (t + 1) * Q_TILE)
            acc = jnp.zeros((Q_TILE, kc), F32)
            for h in range(IDX_HEADS):
                qh = q_ref[0, rs, h * IDX_DIM:(h + 1) * IDX_DIM]
                s = lax.dot_general(qh, kblk, (((1,), (1,)), ((), ())),
                                    preferred_element_type=F32)
                acc = acc + w_ref[0, rs, w_lane0 + h:w_lane0 + h + 1] * jnp.maximum(s, 0.0)
            acc = jnp.where(acc == 0.0, 0.0, acc)
            visible = lane_t + k0 <= qpos_t + (i * rows + t * Q_TILE)
            key_ref[c, rs, :] = jnp.where(visible, _float_sort_key(acc), INT_MIN)
        return 0

    lax.fori_loop(0, n_vis, chunk, 0)

    def select(nv):
        def count(pred):
            tot = jnp.zeros((rows, LANES), F32)
            for c in range(nv):
                hit = jnp.where(pred(key_ref[c], c), 1.0, 0.0)
                for j in range(kc // LANES):
                    tot = tot + hit[:, j * LANES:(j + 1) * LANES]
            return jnp.sum(tot, axis=1, keepdims=True)

        def tbit(b, t_u):
            cand_u = t_u | jnp.left_shift(jnp.int32(1), 31 - b)
            cand_s = cand_u ^ INT_MIN
            cnt = count(lambda key, c: key >= cand_s)
            return jnp.where(cnt >= topk, cand_u, t_u)

        t_u = lax.fori_loop(0, 32, tbit, jnp.zeros((rows, 1), jnp.int32))
        thr = t_u ^ INT_MIN

        need = topk - count(lambda key, c: key > thr)
        n_eq = count(lambda key, c: key == thr)
        jbits = max(1, (L - 1).bit_length())
        jmax_ref[...] = jnp.full((rows, 1), (1 << jbits) - 1, jnp.int32)

        unsettled = jnp.where(thr == INT_MIN, 0.0, jnp.abs(n_eq - need))

        @pl.when(jnp.max(unsettled) > 0.0)
        def _():
            def jbit(b, j):
                cand = j | jnp.left_shift(jnp.int32(1), jbits - 1 - b)
                cnt = count(lambda key, c: jnp.where(lane + c * kc < cand, key, thr - 1) == thr)
                return jnp.where(cnt < need, cand, j)

            jmax_ref[...] = lax.fori_loop(0, jbits, jbit, jnp.zeros((rows, 1), jnp.int32))

        jlim = jnp.minimum(jmax_ref[...], qpos)
        for c in range(nv):
            key = key_ref[c]
            tie = jnp.where(lane + c * kc <= jlim, 0.0, MASK_NEG)
            bias = jnp.where(key > thr, 0.0, jnp.where(key == thr, tie, MASK_NEG))
            for t in range(rows // Q_TILE):
                bias_ref[0, t, c] = bias[t * Q_TILE:(t + 1) * Q_TILE].astype(bias_ref.dtype)
        for c in range(nv, nch):
            for t in range(rows // Q_TILE):
                bias_ref[0, t, c] = jnp.full((Q_TILE, kc), MASK_NEG, bias_ref.dtype)

    for nv in range(1, nch + 1):
        pl.when(n_vis == nv)(functools.partial(select, nv))


def _indexer(q_idx, k_idx, w_idx, topk, kc, w_lane0):
    bsz, L, _ = q_idx.shape
    nq, nch = L // Q_TILE, L // kc
    rows = IDX_ROWS
    assert L % rows == 0 and rows % Q_TILE == 0
    return pl.pallas_call(
        functools.partial(_indexer_kernel, L=L, topk=topk, kc=kc, w_lane0=w_lane0),
        out_shape=jax.ShapeDtypeStruct((bsz, nq, nch, Q_TILE, kc), BF16),
        grid=(bsz, L // rows),
        in_specs=[pl.BlockSpec((1, rows, IDX_HEADS * IDX_DIM), lambda b, i: (b, i, 0)),
                  pl.BlockSpec((1, L, IDX_DIM), lambda b, i: (b, 0, 0)),
                  pl.BlockSpec((1, rows, LANES), lambda b, i: (b, i, 0))],
        out_specs=pl.BlockSpec((1, rows // Q_TILE, nch, Q_TILE, kc),
                               lambda b, i: (b, i, 0, 0, 0)),
        scratch_shapes=[pltpu.VMEM((nch, rows, kc), jnp.int32),
                        pltpu.VMEM((rows, 1), jnp.int32)],
        compiler_params=_params(2, 32),
        name="dsa_indexer",
    )(q_idx, k_idx, w_idx)


def _mla_kernel(q_ref, k_ref, v_ref, bias_ref, g_ref, o_ref, s_ref, mx_ref, ls_ref, acc_ref,
                *, kc, nh):
    i = pl.program_id(2)
    n_vis = ((i + 1) * Q_TILE + kc - 1) // kc
    slabs = kc // LANES

    def fold(x, op):
        t = x[:, 0:LANES]
        for j in range(1, slabs):
            t = op(t, x[:, j * LANES:(j + 1) * LANES])
        return t

    mx_ref[...] = jnp.full(mx_ref.shape, MASK_NEG, F32)

    def scores(c, _):
        k0 = pl.multiple_of(c * kc, kc)
        bias = bias_ref[0, 0, c].astype(F32)
        for h in range(nh):
            hs = slice(h * HEAD_DIM, (h + 1) * HEAD_DIM)
            s = lax.dot_general(q_ref[0, :, hs], k_ref[0, pl.ds(k0, kc), hs],
                                (((1,), (1,)), ((), ())), preferred_element_type=F32) + bias
            s_ref[c, h] = s
            mx_ref[h] = jnp.maximum(mx_ref[h], fold(s, jnp.maximum))
        return 0

    lax.fori_loop(0, n_vis, scores, 0)

    for h in range(nh):
        m = jnp.max(mx_ref[h], axis=1, keepdims=True)
        mx_ref[h] = jnp.broadcast_to(m, (Q_TILE, LANES))
    ls_ref[...] = jnp.zeros(ls_ref.shape, F32)
    acc_ref[...] = jnp.zeros(acc_ref.shape, F32)

    def weights(c, _):
        k0 = pl.multiple_of(c * kc, kc)
        for h in range(nh):
            hs = slice(h * HEAD_DIM, (h + 1) * HEAD_DIM)
            m = mx_ref[h]
            s = s_ref[c, h]
            p = jnp.concatenate(
                [jnp.exp2(s[:, j * LANES:(j + 1) * LANES] - m) for j in range(slabs)], axis=1)
            ls_ref[h] += fold(p, jnp.add)
            acc_ref[h] += jnp.dot(p.astype(BF16), v_ref[0, pl.ds(k0, kc), hs],
                                  preferred_element_type=F32)
        return 0

    lax.fori_loop(0, n_vis, weights, 0)

    for h in range(nh):
        hs = slice(h * HEAD_DIM, (h + 1) * HEAD_DIM)
        g = g_ref[0, :, hs].astype(F32)
        o = acc_ref[h] / jnp.sum(ls_ref[h], axis=1, keepdims=True)
        o_ref[0, :, hs] = (o * (g / (1.0 + jnp.exp(-g)))).astype(o_ref.dtype)


def _masked_attention(q, k, v, bias, gate, kc, nh=8):
    bsz, L, _ = q.shape
    nq, nch = L // Q_TILE, L // kc
    w = nh * HEAD_DIM
    qspec = pl.BlockSpec((1, Q_TILE, w), lambda b, g, i: (b, i, g))
    kvspec = pl.BlockSpec((1, L, w), lambda b, g, i: (b, 0, g))
    return pl.pallas_call(
        functools.partial(_mla_kernel, kc=kc, nh=nh),
        out_shape=jax.ShapeDtypeStruct((bsz, L, C_WIDTH), BF16),
        grid=(bsz, MLA_HEADS // nh, nq),
        in_specs=[qspec, kvspec, kvspec,
                  pl.BlockSpec((1, 1, nch, Q_TILE, kc), lambda b, g, i: (b, i, 0, 0, 0)),
                  qspec],
        out_specs=qspec,
        scratch_shapes=[pltpu.VMEM((nch, nh, Q_TILE, kc), F32),
                        pltpu.VMEM((nh, Q_TILE, LANES), F32),
                        pltpu.VMEM((nh, Q_TILE, LANES), F32),
                        pltpu.VMEM((nh, Q_TILE, HEAD_DIM), F32)],
        compiler_params=_params(3, 48),
        name="dsa_attention",
    )(q, k, v, bias, gate)


def _odd_layer(x2d, bsz, L, norm, w_in, q_norm, kv_norm, k_idx_norm, w_uq, w_uk, w_uv, w_iq,
               w_out, tables):
    m = bsz * L
    h = _rmsnorm(x2d, norm, BF16)

    w_in_t = w_in.T
    p_small = _matmul([h], [w_in_t], ODD_SMALL, F32, w_transposed=True, tn=256,
                      name="odd_in_small")
    gate = _matmul([h], [w_in_t], C_WIDTH, BF16, w_transposed=True, w_col0=ODD_GATE0,
                   name="odd_in_gate")
    c_q, c_kv, k_rope, k_idx, w_idx = _odd_prep(p_small, q_norm, kv_norm, k_idx_norm, tables, L)

    q = _matmul([c_q], [w_uq], C_WIDTH, BF16, epilogue="rope", extra=tables, seq_len=L,
                scale=LOG2E * HEAD_DIM ** -0.5, name="odd_q_up")
    q_idx = _matmul([c_q], [w_iq], IDX_HEADS * IDX_DIM, BF16, epilogue="rope", extra=tables,
                    seq_len=L, scale=IDX_DIM ** -0.5, name="odd_q_idx")

    w_k = jnp.pad(w_uk, ((0, 0), (0, 0), (ROT_DIM, 0))).reshape(KV_RANK, C_WIDTH)
    k = _matmul([c_kv], [w_k], C_WIDTH, BF16, epilogue="add_head", extra=(k_rope,),
                name="odd_k_up")
    v = _matmul([c_kv], [w_uv.reshape(KV_RANK, C_WIDTH)], C_WIDTH, BF16, name="odd_v_up")

    topk = min(IDX_TOPK_MAX, L // 4)
    kc = min(512, L)
    bias = _indexer(q_idx.reshape(bsz, L, -1), k_idx.reshape(bsz, L, IDX_DIM),
                    w_idx.reshape(bsz, L, LANES), topk, kc, IDX_W_LANE0)
    o = _masked_attention(q.reshape(bsz, L, C_WIDTH), k.reshape(bsz, L, C_WIDTH),
                          v.reshape(bsz, L, C_WIDTH), bias, gate.reshape(bsz, L, C_WIDTH), kc)
    return _matmul([o.reshape(m, C_WIDTH)], [w_out], D_MODEL, F32, res=x2d, name="odd_out")


def kernel(x, even_norm, even_w_in, s5_lambda_re, s5_lambda_im, s5_log_step, s5_b_re, s5_b_im,
           s5_c_re, s5_c_im, s5_d, s5_glu_w, s5_glu_b, even_w_out, odd_norm, odd_w_in,
           mla_q_norm, mla_kv_norm, idx_k_norm, mla_w_uq, mla_w_uk, mla_w_uv, idx_w_q,
           odd_w_out, final_norm):
    bsz, L, d = x.shape
    depth = even_norm.shape[0] + odd_norm.shape[0]
    tables = _rope_tables(L)
    x2d = x.reshape(bsz * L, d)
    for layer in range(depth):
        i = layer // 2
        if layer % 2 == 0:
            x2d = _even_layer(x2d, bsz, L, even_norm[i], even_w_in[i], s5_lambda_re[i],
                              s5_lambda_im[i], s5_log_step[i], s5_b_re[i], s5_b_im[i],
                              s5_c_re[i], s5_c_im[i], s5_d[i], s5_glu_w[i], s5_glu_b[i],
                              even_w_out[i], tables)
        else:
            x2d = _odd_layer(x2d, bsz, L, odd_norm[i], odd_w_in[i], mla_q_norm[i],
                             mla_kv_norm[i], idx_k_norm[i], mla_w_uq[i], mla_w_uk[i],
                             mla_w_uv[i], idx_w_q[i], odd_w_out[i], tables)
    return _rmsnorm(x2d, final_norm, F32).reshape(bsz, L, d)
```

```python
import functools
import math

import jax
import jax.numpy as jnp
from jax import lax
from jax.experimental import pallas as pl
from jax.experimental.pallas import tpu as pltpu

F32 = jnp.float32
BF16 = jnp.bfloat16

LANES = 128
SUBLANES = 8

D_MODEL = 4096
EPS = 1e-6
ROPE_THETA = 500000.0
HEAD_DIM = 128
ROT_DIM = HEAD_DIM // 4
ROT_HALF = ROT_DIM // 2

S5_WIDTH = D_MODEL // 4
S5_GROUP = 16
S5_GROUPS = S5_WIDTH // S5_GROUP
S5_STATE = 64
S5_BLOCKS = S5_WIDTH // 128
S5_BLOCK_GROUPS = S5_GROUPS // S5_BLOCKS
S5_BLOCK_STATES = S5_BLOCK_GROUPS * S5_STATE

DIL_PATTERNS = ((128, 1), (512, 4), (2048, 16))
N_DIL = len(DIL_PATTERNS)
DIL_HEADS = D_MODEL // 512
DIL_WIDTH = DIL_HEADS * HEAD_DIM
QKV_WIDTH = 3 * N_DIL * DIL_WIDTH

MLA_HEADS = D_MODEL // HEAD_DIM
MLA_NOPE = HEAD_DIM - ROT_DIM
Q_RANK = 1536
KV_RANK = 512
IDX_HEADS = D_MODEL // 128
IDX_DIM = 128
IDX_TOPK_MAX = 256
C_WIDTH = MLA_HEADS * HEAD_DIM
ODD_GATE0 = Q_RANK + KV_RANK + ROT_DIM + IDX_DIM + IDX_HEADS
ODD_SMALL = Q_RANK + KV_RANK + 2 * LANES
IDX_W_LANE0 = ROT_DIM + IDX_DIM - LANES

Q_TILE = 128
IDX_ROWS = 256
MASK_NEG = -1e30
LOG2E = 1.4426950408889634
INT_MIN = -(2 ** 31)


def _params(n_grid, vmem_mb):
    return pltpu.CompilerParams(
        dimension_semantics=("arbitrary",) * n_grid,
        vmem_limit_bytes=vmem_mb * 1024 * 1024)


def _rope_tables(L):
    inv = ROPE_THETA ** (-jnp.arange(0, ROT_DIM, 2, dtype=F32) / ROT_DIM)
    ang = jnp.arange(L, dtype=F32)[:, None] * inv[None, :]
    cos, sin = jnp.cos(ang), jnp.sin(ang)
    ones = jnp.ones((L, HEAD_DIM - ROT_DIM), F32)
    zeros = jnp.zeros((L, HEAD_DIM - ROT_DIM), F32)
    zh = jnp.zeros((L, ROT_HALF), F32)
    c = jnp.concatenate([cos, cos, ones], axis=1)
    sa = jnp.concatenate([-sin, zh, zeros], axis=1)
    sb = jnp.concatenate([zh, sin, zeros], axis=1)
    return c, sa, sb


def _rope_swap_tables(tables, width):
    c, sa, sb = tables
    col = jnp.arange(width)
    lane, head = col % HEAD_DIM, col // HEAD_DIM
    src = jnp.where(lane < ROT_HALF, col + ROT_HALF, col - ROT_HALF)
    sign = jnp.where(lane < ROT_HALF, -1.0, 1.0)
    perm = jnp.where((col[:, None] == src[None, :]) & (lane[None, :] < ROT_DIM)
                     & (head[:, None] == head[None, :]), sign[None, :], 0.0).astype(BF16)
    return c, sb - sa, perm


def _rope_head(x, c, sa, sb):
    return (x * c + pltpu.roll(x, HEAD_DIM - ROT_HALF, 1) * sa
            + pltpu.roll(x, ROT_HALF, 1) * sb)


def _rmsnorm_kernel(x_ref, g_ref, o_ref):
    x = x_ref[...]
    ms = jnp.mean(x * x, axis=-1, keepdims=True)
    o_ref[...] = (x * lax.rsqrt(ms + EPS) * g_ref[...]).astype(o_ref.dtype)


def _rmsnorm(x, g, out_dtype, tm=256):
    m, d = x.shape
    return pl.pallas_call(
        _rmsnorm_kernel,
        out_shape=jax.ShapeDtypeStruct((m, d), out_dtype),
        grid=(m // tm,),
        in_specs=[pl.BlockSpec((tm, d), lambda i: (i, 0)),
                  pl.BlockSpec((1, d), lambda i: (0, 0))],
        out_specs=pl.BlockSpec((tm, d), lambda i: (i, 0)),
        compiler_params=_params(1, 32),
        name="rmsnorm",
    )(x, g.reshape(1, d))


def _mm_kernel(*refs, n_a, has_res, epilogue, scale, tiles_per_kind, w_transposed):
    a_refs = refs[:n_a]
    w_refs = refs[n_a:2 * n_a]
    pos = 2 * n_a
    res_ref = None
    if has_res:
        res_ref = refs[pos]
        pos += 1
    extra = refs[pos:-1]
    o_ref = refs[-1]

    acc = None
    for a_ref, w_ref in zip(a_refs, w_refs):
        contract = (((1,), (1 if w_transposed else 0,)), ((), ()))
        d = lax.dot_general(a_ref[...], w_ref[...].astype(BF16), contract,
                            preferred_element_type=F32)
        acc = d if acc is None else acc + d
    if has_res:
        acc = acc + res_ref[...]
    n_heads = acc.shape[1] // HEAD_DIM

    def store_roped():
        c_ref, s_ref, perm_ref = extra
        swapped = jnp.dot(acc.astype(BF16), perm_ref[...], preferred_element_type=F32)
        for h in range(n_heads):
            hs = slice(h * HEAD_DIM, (h + 1) * HEAD_DIM)
            y = acc[:, hs] * c_ref[...] + swapped[:, hs] * s_ref[...]
            o_ref[:, hs] = (y * scale if scale != 1.0 else y).astype(o_ref.dtype)

    if epilogue == "none":
        o_ref[...] = acc.astype(o_ref.dtype)
    elif epilogue == "rope":
        store_roped()
    elif epilogue == "rope_qk":
        kind = (pl.program_id(1) // tiles_per_kind) % 3

        @pl.when(kind == 2)
        def _():
            o_ref[...] = acc.astype(o_ref.dtype)

        pl.when(kind != 2)(store_roped)
    elif epilogue == "add_head":
        kr = extra[0][...]
        cols = [acc[:, h * HEAD_DIM:(h + 1) * HEAD_DIM] + kr for h in range(n_heads)]
        o_ref[...] = jnp.concatenate(cols, axis=1).astype(o_ref.dtype)
    else:
        raise ValueError(epilogue)


def _matmul(a_list, w_list, n, out_dtype, *, w_col0=0, w_row_blocks=None, res=None,
            epilogue="none", extra=(), scale=1.0, seq_len=None, w_transposed=False,
            tn=512, vmem_mb=52, name="matmul"):
    m = a_list[0].shape[0]
    deep = sum(a.shape[1] for a in a_list) >= D_MODEL
    tm = 1024 if deep else 2048
    tm = min(tm, m)
    assert m % tm == 0 and n % tn == 0
    n_a = len(a_list)
    if w_row_blocks is None:
        w_row_blocks = [0] * n_a
    in_specs, args = [], []
    for a in a_list:
        k = a.shape[1]
        mode = {} if deep else {"pipeline_mode": pl.Buffered(1)}
        in_specs.append(pl.BlockSpec((tm, k), lambda i, j: (i, 0), **mode))
        args.append(a)
    for a, w, rb in zip(a_list, w_list, w_row_blocks):
        k = a.shape[1]
        if not w_transposed:
            assert w_col0 % tn == 0
            in_specs.append(pl.BlockSpec((k, tn),
                                         lambda i, j, rb=rb: (rb, j + w_col0 // tn)))
        elif w_col0 % tn == 0:
            in_specs.append(pl.BlockSpec((tn, k), lambda i, j: (j + w_col0 // tn, 0)))
        else:
            g = math.gcd(w_col0, tn)
            assert g % SUBLANES == 0
            in_specs.append(pl.BlockSpec(
                (pl.Element(tn), pl.Element(k)),
                lambda i, j, g=g: ((w_col0 // g + j * (tn // g)) * g, 0)))
        args.append(w)
    if res is not None:
        in_specs.append(pl.BlockSpec((tm, tn), lambda i, j: (i, j)))
        args.append(res)
    tiles_per_seq = None if seq_len is None else seq_len // tm
    if epilogue in ("rope", "rope_qk"):
        extra = _rope_swap_tables(extra, tn)
    for e in extra:
        if e.shape == (tn, tn):
            in_specs.append(pl.BlockSpec((tn, tn), lambda i, j: (0, 0)))
        elif e.shape[0] == m:
            in_specs.append(pl.BlockSpec((tm, e.shape[1]), lambda i, j: (i, 0)))
        else:
            assert seq_len is not None and e.shape[0] == seq_len and seq_len % tm == 0
            in_specs.append(pl.BlockSpec((tm, e.shape[1]),
                                         lambda i, j: (i % tiles_per_seq, 0)))
        args.append(e)
    kern = functools.partial(_mm_kernel, n_a=n_a, has_res=res is not None,
                             epilogue=epilogue, scale=scale,
                             tiles_per_kind=DIL_WIDTH // tn, w_transposed=w_transposed)
    return pl.pallas_call(
        kern,
        out_shape=jax.ShapeDtypeStruct((m, n), out_dtype),
        grid=(m // tm, n // tn),
        in_specs=in_specs,
        out_specs=pl.BlockSpec((tm, tn), lambda i, j: (i, j)),
        compiler_params=_params(2, vmem_mb),
        name=name,
    )(*args)


def _s5_discretise(lam_re, lam_im, log_step, b_re, b_im, c_re, c_im):
    lr = jnp.minimum(lam_re.astype(F32), -1e-4)
    li = lam_im.astype(F32)
    dt = jnp.exp(log_step.astype(F32))[:, None]
    mag = jnp.exp(lr * dt)
    ar = mag * jnp.cos(li * dt)
    ai = mag * jnp.sin(li * dt)
    den = lr * lr + li * li
    nr, ni = ar - 1.0, ai
    cr = (nr * lr + ni * li) / den
    ci = (ni * lr - nr * li) / den
    br32, bi32 = b_re.astype(F32), b_im.astype(F32)
    bbr = cr[..., None] * br32 - ci[..., None] * bi32
    bbi = cr[..., None] * bi32 + ci[..., None] * br32

    eye = jnp.eye(S5_BLOCK_GROUPS, dtype=F32)

    def in_blockdiag(t):
        t = t.reshape(S5_BLOCKS, S5_BLOCK_GROUPS, S5_STATE, S5_GROUP)
        t = jnp.einsum("bgpj,gh->bgjhp", t, eye)
        return t.reshape(S5_BLOCKS, 128, S5_BLOCK_STATES)

    def out_blockdiag(t):
        t = t.reshape(S5_BLOCKS, S5_BLOCK_GROUPS, S5_GROUP, S5_STATE)
        t = jnp.einsum("bgjp,gh->bgphj", t, eye)
        return t.reshape(S5_BLOCKS, S5_BLOCK_STATES, 128)

    b_mat = jnp.concatenate([in_blockdiag(bbr), in_blockdiag(bbi)], axis=2).astype(BF16)
    c_mat = jnp.concatenate([out_blockdiag(c_re.astype(F32)),
                             -out_blockdiag(c_im.astype(F32))], axis=1).astype(BF16)

    a_r = ar.reshape(S5_BLOCKS, S5_BLOCK_STATES)
    a_i = ai.reshape(S5_BLOCKS, S5_BLOCK_STATES)
    pr, pi = [a_r], [a_i]
    for _ in range(SUBLANES - 1):
        pr.append(pr[-1] * a_r - pi[-1] * a_i)
        pi.append(pr[-2] * a_i + pi[-1] * a_r)
    pow_re = jnp.stack(pr, axis=1)
    pow_im = jnp.stack(pi, axis=1)
    return b_mat, c_mat, pow_re, pow_im


def _s5_kernel(u_ref, b_ref, c_ref, pre_ref, pim_ref, d_ref, y_ref, s_ref):
    L = u_ref.shape[1]
    ns = S5_BLOCK_STATES
    u = u_ref[0]
    s_ref[...] = jnp.dot(u.astype(BF16), b_ref[0], preferred_element_type=F32)

    row = lax.broadcasted_iota(jnp.int32, (SUBLANES, ns), 0)
    pre = pre_ref[0]
    pim = pim_ref[0]

    def bcast(v, r):
        return jnp.broadcast_to(v[r:r + 1, :], (SUBLANES, ns))

    steps = tuple((d, jnp.where(row >= d, bcast(pre, d - 1), 0.0),
                   jnp.where(row >= d, bcast(pim, d - 1), 0.0)) for d in (1, 2, 4))

    def block(i, carry):
        cr, ci = carry
        off = pl.multiple_of(i * SUBLANES, SUBLANES)
        xr = s_ref[pl.ds(off, SUBLANES), 0:ns]
        xi = s_ref[pl.ds(off, SUBLANES), ns:2 * ns]
        for d, ar, ai in steps:
            sr = pltpu.roll(xr, d, 0)
            si = pltpu.roll(xi, d, 0)
            xr, xi = xr + ar * sr - ai * si, xi + ar * si + ai * sr
        crb = jnp.broadcast_to(cr, (SUBLANES, ns))
        cib = jnp.broadcast_to(ci, (SUBLANES, ns))
        xr, xi = xr + pre * crb - pim * cib, xi + pre * cib + pim * crb
        s_ref[pl.ds(off, SUBLANES), 0:ns] = xr
        s_ref[pl.ds(off, SUBLANES), ns:2 * ns] = xi
        return xr[SUBLANES - 1:SUBLANES, :], xi[SUBLANES - 1:SUBLANES, :]

    zero = jnp.zeros((1, ns), F32)
    lax.fori_loop(0, L // SUBLANES, block, (zero, zero))

    y = jnp.dot(s_ref[...].astype(BF16), c_ref[0], preferred_element_type=F32)
    y_ref[0] = y + d_ref[0] * u


def _s5(proj_a, b_mat, c_mat, pow_re, pow_im, d_skip):
    bsz, L, _ = proj_a.shape
    ns = S5_BLOCK_STATES
    return pl.pallas_call(
        _s5_kernel,
        out_shape=jax.ShapeDtypeStruct((bsz, L, S5_WIDTH), F32),
        grid=(bsz, S5_BLOCKS),
        in_specs=[
            pl.BlockSpec((1, L, 128), lambda b, g: (b, 0, g)),
            pl.BlockSpec((1, 128, 2 * ns), lambda b, g: (g, 0, 0)),
            pl.BlockSpec((1, 2 * ns, 128), lambda b, g: (g, 0, 0)),
            pl.BlockSpec((1, SUBLANES, ns), lambda b, g: (g, 0, 0)),
            pl.BlockSpec((1, SUBLANES, ns), lambda b, g: (g, 0, 0)),
            pl.BlockSpec((1, 1, 128), lambda b, g: (g, 0, 0)),
        ],
        out_specs=pl.BlockSpec((1, L, 128), lambda b, g: (b, 0, g)),
        scratch_shapes=[pltpu.VMEM((L, 2 * ns), F32)],
        compiler_params=_params(2, 40),
        name="s5_scan",
    )(proj_a, b_mat, c_mat, pow_re, pow_im, d_skip.reshape(S5_BLOCKS, 1, 128))


def _glu_kernel(y_ref, z_ref, w_ref, b_ref, o_ref):
    g = jax.nn.gelu(y_ref[...])
    t = jnp.dot(g.astype(BF16), w_ref[...].astype(BF16), preferred_element_type=F32) + b_ref[...]
    z = z_ref[...]
    gate = z / (1.0 + jnp.exp(-z))
    o_ref[...] = (g / (1.0 + jnp.exp(-t)) * gate).astype(o_ref.dtype)


def _glu(y, proj_a, glu_w, glu_b, tm=512):
    m = y.shape[0]
    return pl.pallas_call(
        _glu_kernel,
        out_shape=jax.ShapeDtypeStruct((m, S5_WIDTH), BF16),
        grid=(m // tm,),
        in_specs=[pl.BlockSpec((tm, S5_WIDTH), lambda i: (i, 0)),
                  pl.BlockSpec((tm, S5_WIDTH), lambda i: (i, 1)),
                  pl.BlockSpec((S5_WIDTH, S5_WIDTH), lambda i: (0, 0)),
                  pl.BlockSpec((1, S5_WIDTH), lambda i: (0, 0))],
        out_specs=pl.BlockSpec((tm, S5_WIDTH), lambda i: (i, 0)),
        compiler_params=_params(1, 40),
        name="s5_glu",
    )(y, proj_a, glu_w, glu_b.reshape(1, S5_WIDTH))


DIL_UNROLL = 8
MERGE_ROWS = 256


def _dil_kernel(*refs, L):
    qkv_refs = refs[:3 * N_DIL]
    z_ref, out_ref, og_ref, lg_ref, cls_ref = refs[3 * N_DIL:]
    n_tiles = L // Q_TILE

    for g, (window, dil) in enumerate(DIL_PATTERNS):
        n = L // dil
        nb = n // Q_TILE
        cw = Q_TILE if nb == 1 else 2 * Q_TILE
        half = (window // dil) // 2

        def rows(start, size, dil=dil):
            return pl.ds(start, size) if dil == 1 else pl.ds(start, size, stride=dil)

        for kind in range(3):
            for r in range(dil):
                cls_ref[kind, r * n:(r + 1) * n, :] = (
                    qkv_refs[3 * g + kind][0, rows(r, n), :].astype(BF16))

        def scores(it, n=n, nb=nb, cw=cw, half=half, dil=dil):
            r = it // nb
            qb = it % nb
            kb = jnp.clip(qb - 1, 0, max(nb - 2, 0))
            qc = pl.multiple_of(r * n + qb * Q_TILE, Q_TILE)
            kc = pl.multiple_of(r * n + kb * Q_TILE, Q_TILE)
            qpos = qb * Q_TILE + lax.broadcasted_iota(jnp.int32, (Q_TILE, cw), 0)
            kpos = kb * Q_TILE + lax.broadcasted_iota(jnp.int32, (Q_TILE, cw), 1)
            valid = jnp.abs(qpos - kpos - half) <= half
            s = lax.dot_general(cls_ref[0, pl.ds(qc, Q_TILE), :], cls_ref[1, pl.ds(kc, cw), :],
                                (((1,), (1,)), ((), ())),
                                preferred_element_type=F32) * (HEAD_DIM ** -0.5)
            return jnp.where(valid, s, MASK_NEG), kc, r + qb * (Q_TILE * dil)

        def softmax(s):
            m = jnp.max(s, axis=1, keepdims=True)
            p = jnp.exp(s - m)
            l = jnp.sum(p, axis=1, keepdims=True)
            return p.astype(BF16), l, m + jnp.log(l)

        def unrolled(j, _, g=g, cw=cw, rows=rows, scores=scores, softmax=softmax):
            blocks = [scores(j * DIL_UNROLL + u) for u in range(DIL_UNROLL)]
            probs = [softmax(s) for s, _, _ in blocks]
            for (_, kc, q0), (p, l, lse) in zip(blocks, probs):
                o = jnp.dot(p, cls_ref[2, pl.ds(kc, cw), :], preferred_element_type=F32) / l
                og_ref[g, rows(q0, Q_TILE), :] = o
                lg_ref[g, rows(q0, Q_TILE), :] = jnp.broadcast_to(lse, (Q_TILE, HEAD_DIM))
            return 0

        lax.fori_loop(0, n_tiles // DIL_UNROLL, unrolled, 0)

    def merge(j, _):
        rs = pl.ds(pl.multiple_of(j * MERGE_ROWS, MERGE_ROWS), MERGE_ROWS)
        a = [lg_ref[g, rs, :] for g in range(N_DIL)]
        m = functools.reduce(jnp.maximum, a)
        e = [jnp.exp(x - m) for x in a]
        o = sum(e[g] * og_ref[g, rs, :] for g in range(N_DIL)) / sum(e)
        z = z_ref[0, rs, :]
        out_ref[0, rs, :] = (o * (z / (1.0 + jnp.exp(-z)))).astype(out_ref.dtype)
        return 0

    lax.fori_loop(0, L // MERGE_ROWS, merge, 0)


def _dilated_attention(qkv, proj_b):
    bsz, L, _ = qkv.shape
    for window, dil in DIL_PATTERNS:
        assert (L // dil) % Q_TILE == 0 and (window // dil) % 2 == 0
    assert (L // Q_TILE) % DIL_UNROLL == 0 and L % MERGE_ROWS == 0

    def in_spec(g, kind):
        return pl.BlockSpec((1, L, HEAD_DIM),
                            lambda b, h: (b, 0, (g * 3 + kind) * DIL_HEADS + h))

    head_spec = pl.BlockSpec((1, L, HEAD_DIM), lambda b, h: (b, 0, h))
    return pl.pallas_call(
        functools.partial(_dil_kernel, L=L),
        out_shape=jax.ShapeDtypeStruct((bsz, L, DIL_WIDTH), BF16),
        grid=(bsz, DIL_HEADS),
        in_specs=[in_spec(g, kind) for g in range(N_DIL) for kind in range(3)] + [head_spec],
        out_specs=head_spec,
        scratch_shapes=[pltpu.VMEM((N_DIL, L, HEAD_DIM), F32),
                        pltpu.VMEM((N_DIL, L, HEAD_DIM), F32),
                        pltpu.VMEM((3, L, HEAD_DIM), BF16)],
        compiler_params=_params(2, 48),
        name="dilated_attention",
    )(*([qkv] * (3 * N_DIL)), proj_b)


def _even_layer(x2d, bsz, L, norm, w_in, lam_re, lam_im, log_step, b_re, b_im, c_re, c_im,
                d_skip, glu_w, glu_b, w_out, tables):
    h = _rmsnorm(x2d, norm, BF16)
    proj_a = _matmul([h], [w_in], 2 * S5_WIDTH, F32, w_col0=0, name="even_in_a")
    qkv = _matmul([h], [w_in], QKV_WIDTH, F32, w_col0=2 * S5_WIDTH, epilogue="rope_qk",
                  extra=tables, seq_len=L, name="even_in_qkv")
    proj_b = _matmul([h], [w_in], DIL_WIDTH, F32, w_col0=2 * S5_WIDTH + QKV_WIDTH,
                     name="even_in_zb")

    b_mat, c_mat, pow_re, pow_im = _s5_discretise(lam_re, lam_im, log_step, b_re, b_im,
                                                  c_re, c_im)
    y = _s5(proj_a.reshape(bsz, L, 2 * S5_WIDTH), b_mat, c_mat, pow_re, pow_im, d_skip)
    a_out = _glu(y.reshape(bsz * L, S5_WIDTH), proj_a, glu_w, glu_b)

    b_out = _dilated_attention(qkv.reshape(bsz, L, QKV_WIDTH),
                               proj_b.reshape(bsz, L, DIL_WIDTH)).reshape(bsz * L, DIL_WIDTH)

    return _matmul([a_out, b_out], [w_out, w_out], D_MODEL, F32, w_row_blocks=[0, 1],
                   res=x2d, name="even_out")


def _odd_prep_kernel(p_ref, qn_ref, kvn_ref, kin_ref, c_ref, sa_ref, sb_ref,
                     cq_ref, ckv_ref, kr_ref, ki_ref, wi_ref):
    def norm(x, g):
        ms = jnp.mean(x * x, axis=-1, keepdims=True)
        return x * lax.rsqrt(ms + EPS) * g

    c, sa, sb = c_ref[...], sa_ref[...], sb_ref[...]
    o = 0
    cq_ref[...] = norm(p_ref[:, o:o + Q_RANK], qn_ref[...]).astype(cq_ref.dtype)
    o += Q_RANK
    ckv_ref[...] = norm(p_ref[:, o:o + KV_RANK], kvn_ref[...]).astype(ckv_ref.dtype)
    o += KV_RANK
    lo = p_ref[:, o:o + LANES]
    hi = p_ref[:, o + LANES:o + 2 * LANES]
    lane = lax.broadcasted_iota(jnp.int32, lo.shape, 1)
    kr_ref[...] = _rope_head(jnp.where(lane < ROT_DIM, lo, 0.0), c, sa, sb)
    k_i = jnp.where(lane < LANES - ROT_DIM, pltpu.roll(lo, LANES - ROT_DIM, 1),
                    pltpu.roll(hi, LANES - ROT_DIM, 1))
    ki_ref[...] = _rope_head(norm(k_i, kin_ref[...]), c, sa, sb).astype(ki_ref.dtype)
    wi_ref[...] = hi * (IDX_HEADS ** -0.5)


def _odd_prep(p_small, q_norm, kv_norm, k_idx_norm, tables, L, tm=256):
    m = p_small.shape[0]
    tps = L // tm
    row = lambda w: pl.BlockSpec((tm, w), lambda i: (i, 0))
    tab = pl.BlockSpec((tm, 128), lambda i: (i % tps, 0))
    vec = lambda w: pl.BlockSpec((1, w), lambda i: (0, 0))
    return pl.pallas_call(
        _odd_prep_kernel,
        out_shape=[jax.ShapeDtypeStruct((m, Q_RANK), BF16),
                   jax.ShapeDtypeStruct((m, KV_RANK), BF16),
                   jax.ShapeDtypeStruct((m, 128), F32),
                   jax.ShapeDtypeStruct((m, 128), BF16),
                   jax.ShapeDtypeStruct((m, 128), F32)],
        grid=(m // tm,),
        in_specs=[row(ODD_SMALL), vec(Q_RANK), vec(KV_RANK), vec(IDX_DIM), tab, tab, tab],
        out_specs=[row(Q_RANK), row(KV_RANK), row(128), row(128), row(128)],
        compiler_params=_params(1, 32),
        name="odd_prep",
    )(p_small, q_norm.reshape(1, -1), kv_norm.reshape(1, -1), k_idx_norm.reshape(1, -1),
      *tables)


def _float_sort_key(s):
    b = pltpu.bitcast(s, jnp.int32)
    return b ^ ((b >> 31) & 0x7FFFFFFF)


def _indexer_kernel(q_ref, k_ref, w_ref, bias_ref, key_ref, jmax_ref, *, L, topk, kc, w_lane0):
    i = pl.program_id(1)
    nch = L // kc
    rows = IDX_ROWS
    n_vis = ((i + 1) * rows + kc - 1) // kc
    qpos = i * rows + lax.broadcasted_iota(jnp.int32, (rows, 1), 0)
    lane = lax.broadcasted_iota(jnp.int32, (rows, kc), 1)
    lane_t = lax.broadcasted_iota(jnp.int32, (Q_TILE, kc), 1)
    qpos_t = lax.broadcasted_iota(jnp.int32, (Q_TILE, 1), 0)

    def chunk(c, _):
        k0 = pl.multiple_of(c * kc, kc)
        kblk = k_ref[0, pl.ds(k0, kc), :]
        for t in range(rows // Q_TILE):
            rs = slice(t * Q_TILE, (t + 1) * Q_TILE)
            acc = jnp.zeros((Q_TILE, kc), F32)
            for h in range(IDX_HEADS):
                qh = q_ref[0, rs, h * IDX_DIM:(h + 1) * IDX_DIM]
                s = lax.dot_general(qh, kblk, (((1,), (1,)), ((), ())),
                                    preferred_element_type=F32)
                acc = acc + w_ref[0, rs, w_lane0 + h:w_lane0 + h + 1] * jnp.maximum(s, 0.0)
            acc = jnp.where(acc == 0.0, 0.0, acc)
            visible = lane_t + k0 <= qpos_t + (i * rows + t * Q_TILE)
            key_ref[c, rs, :] = jnp.where(visible, _float_sort_key(acc), INT_MIN)
        return 0

    lax.fori_loop(0, n_vis, chunk, 0)

    def select(nv):
        def count(pred):
            tot = jnp.zeros((rows, LANES), F32)
            for c in range(nv):
                hit = jnp.where(pred(key_ref[c], c), 1.0, 0.0)
                for j in range(kc // LANES):
                    tot = tot + hit[:, j * LANES:(j + 1) * LANES]
            return jnp.sum(tot, axis=1, keepdims=True)

        def tbit(b, t_u):
            cand_u = t_u | jnp.left_shift(jnp.int32(1), 31 - b)
            cand_s = cand_u ^ INT_MIN
            cnt = count(lambda key, c: key >= cand_s)
            return jnp.where(cnt >= topk, cand_u, t_u)

        t_u = lax.fori_loop(0, 32, tbit, jnp.zeros((rows, 1), jnp.int32))
        thr = t_u ^ INT_MIN

        need = topk - count(lambda key, c: key > thr)
        n_eq = count(lambda key, c: key == thr)
        jbits = max(1, (L - 1).bit_length())
        jmax_ref[...] = jnp.full((rows, 1), (1 << jbits) - 1, jnp.int32)

        unsettled = jnp.where(thr == INT_MIN, 0.0, jnp.abs(n_eq - need))

        @pl.when(jnp.max(unsettled) > 0.0)
        def _():
            def jbit(b, j):
                cand = j | jnp.left_shift(jnp.int32(1), jbits - 1 - b)
                cnt = count(lambda key, c: jnp.where(lane + c * kc < cand, key, thr - 1) == thr)
                return jnp.where(cnt < need, cand, j)

            jmax_ref[...] = lax.fori_loop(0, jbits, jbit, jnp.zeros((rows, 1), jnp.int32))

        jlim = jnp.minimum(jmax_ref[...], qpos)
        for c in range(nv):
            key = key_ref[c]
            tie = jnp.where(lane + c * kc <= jlim, 0.0, MASK_NEG)
            bias = jnp.where(key > thr, 0.0, jnp.where(key == thr, tie, MASK_NEG))
            for t in range(rows // Q_TILE):
                bias_ref[0, t, c] = bias[t * Q_TILE:(t + 1) * Q_TILE].astype(bias_ref.dtype)
        for c in range(nv, nch):
            for t in range(rows // Q_TILE):
                bias_ref[0, t, c] = jnp.full((Q_TILE, kc), MASK_NEG, bias_ref.dtype)

    for nv in range(1, nch + 1):
        pl.when(n_vis == nv)(functools.partial(select, nv))


def _indexer(q_idx, k_idx, w_idx, topk, kc, w_lane0):
    bsz, L, _ = q_idx.shape
    nq, nch = L // Q_TILE, L // kc
    rows = IDX_ROWS
    assert L % rows == 0 and rows % Q_TILE == 0
    return pl.pallas_call(
        functools.partial(_indexer_kernel, L=L, topk=topk, kc=kc, w_lane0=w_lane0),
        out_shape=jax.ShapeDtypeStruct((bsz, nq, nch, Q_TILE, kc), BF16),
        grid=(bsz, L // rows),
        in_specs=[pl.BlockSpec((1, rows, IDX_HEADS * IDX_DIM), lambda b, i: (b, i, 0)),
                  pl.BlockSpec((1, L, IDX_DIM), lambda b, i: (b, 0, 0)),
                  pl.BlockSpec((1, rows, LANES), lambda b, i: (b, i, 0))],
        out_specs=pl.BlockSpec((1, rows // Q_TILE, nch, Q_TILE, kc),
                               lambda b, i: (b, i, 0, 0, 0)),
        scratch_shapes=[pltpu.VMEM((nch, rows, kc), jnp.int32),
                        pltpu.VMEM((rows, 1), jnp.int32)],
        compiler_params=_params(2, 32),
        name="dsa_indexer",
    )(q_idx, k_idx, w_idx)


def _mla_kernel(q_ref, k_ref, v_ref, bias_ref, g_ref, o_ref, s_ref, mx_ref, acc_ref, *, kc, nh):
    i = pl.program_id(2)
    n_vis = ((i + 1) * Q_TILE + kc - 1) // kc
    slabs = kc // LANES

    def fold(x, op):
        t = x[:, 0:LANES]
        for j in range(1, slabs):
            t = op(t, x[:, j * LANES:(j + 1) * LANES])
        return t

    mx_ref[...] = jnp.full(mx_ref.shape, MASK_NEG, F32)

    def scores(c, _):
        k0 = pl.multiple_of(c * kc, kc)
        bias = bias_ref[0, 0, c].astype(F32)
        for h in range(nh):
            hs = slice(h * HEAD_DIM, (h + 1) * HEAD_DIM)
            s = lax.dot_general(q_ref[0, :, hs], k_ref[0, pl.ds(k0, kc), hs],
                                (((1,), (1,)), ((), ())), preferred_element_type=F32) + bias
            s_ref[c, h] = s
            mx_ref[h] = jnp.maximum(mx_ref[h], fold(s, jnp.maximum))
        return 0

    lax.fori_loop(0, n_vis, scores, 0)

    for h in range(nh):
        m = jnp.max(mx_ref[h], axis=1, keepdims=True)
        mx_ref[h] = jnp.broadcast_to(m, (Q_TILE, LANES))
    acc_ref[...] = jnp.zeros(acc_ref.shape, F32)

    ones = jnp.ones((kc, LANES), BF16)

    def weights(c, _):
        k0 = pl.multiple_of(c * kc, kc)
        for h in range(nh):
            hs = slice(h * HEAD_DIM, (h + 1) * HEAD_DIM)
            m = mx_ref[h]
            s = s_ref[c, h]
            p = jnp.concatenate(
                [jnp.exp2(s[:, j * LANES:(j + 1) * LANES] - m) for j in range(slabs)], axis=1)
            v1 = jnp.concatenate([v_ref[0, pl.ds(k0, kc), hs], ones], axis=1)
            acc_ref[h] += jnp.dot(p.astype(BF16), v1, preferred_element_type=F32)
        return 0

    lax.fori_loop(0, n_vis, weights, 0)

    for h in range(nh):
        hs = slice(h * HEAD_DIM, (h + 1) * HEAD_DIM)
        g = g_ref[0, :, hs].astype(F32)
        o = acc_ref[h, :, 0:HEAD_DIM] / acc_ref[h, :, HEAD_DIM:HEAD_DIM + LANES]
        o_ref[0, :, hs] = (o * (g / (1.0 + jnp.exp(-g)))).astype(o_ref.dtype)


def _masked_attention(q, k, v, bias, gate, kc, nh=8):
    bsz, L, _ = q.shape
    nq, nch = L // Q_TILE, L // kc
    w = nh * HEAD_DIM
    qspec = pl.BlockSpec((1, Q_TILE, w), lambda b, g, i: (b, i, g))
    kvspec = pl.BlockSpec((1, L, w), lambda b, g, i: (b, 0, g))
    return pl.pallas_call(
        functools.partial(_mla_kernel, kc=kc, nh=nh),
        out_shape=jax.ShapeDtypeStruct((bsz, L, C_WIDTH), BF16),
        grid=(bsz, MLA_HEADS // nh, nq),
        in_specs=[qspec, kvspec, kvspec,
                  pl.BlockSpec((1, 1, nch, Q_TILE, kc), lambda b, g, i: (b, i, 0, 0, 0)),
                  qspec],
        out_specs=qspec,
        scratch_shapes=[pltpu.VMEM((nch, nh, Q_TILE, kc), F32),
                        pltpu.VMEM((nh, Q_TILE, LANES), F32),
                        pltpu.VMEM((nh, Q_TILE, HEAD_DIM + LANES), F32)],
        compiler_params=_params(3, 48),
        name="dsa_attention",
    )(q, k, v, bias, gate)


def _odd_layer(x2d, bsz, L, norm, w_in, q_norm, kv_norm, k_idx_norm, w_uq, w_uk, w_uv, w_iq,
               w_out, tables):
    m = bsz * L
    h = _rmsnorm(x2d, norm, BF16)

    w_in_t = w_in.T
    p_small = _matmul([h], [w_in_t], ODD_SMALL, F32, w_transposed=True, tn=256,
                      name="odd_in_small")
    gate = _matmul([h], [w_in_t], C_WIDTH, BF16, w_transposed=True, w_col0=ODD_GATE0,
                   name="odd_in_gate")
    c_q, c_kv, k_rope, k_idx, w_idx = _odd_prep(p_small, q_norm, kv_norm, k_idx_norm, tables, L)

    q = _matmul([c_q], [w_uq], C_WIDTH, BF16, epilogue="rope", extra=tables, seq_len=L,
                scale=LOG2E * HEAD_DIM ** -0.5, name="odd_q_up")
    q_idx = _matmul([c_q], [w_iq], IDX_HEADS * IDX_DIM, BF16, epilogue="rope", extra=tables,
                    seq_len=L, scale=IDX_DIM ** -0.5, name="odd_q_idx")

    w_k = jnp.pad(w_uk, ((0, 0), (0, 0), (ROT_DIM, 0))).reshape(KV_RANK, C_WIDTH)
    k = _matmul([c_kv], [w_k], C_WIDTH, BF16, epilogue="add_head", extra=(k_rope,),
                name="odd_k_up")
    v = _matmul([c_kv], [w_uv.reshape(KV_RANK, C_WIDTH)], C_WIDTH, BF16, name="odd_v_up")

    topk = min(IDX_TOPK_MAX, L // 4)
    kc = min(512, L)
    bias = _indexer(q_idx.reshape(bsz, L, -1), k_idx.reshape(bsz, L, IDX_DIM),
                    w_idx.reshape(bsz, L, LANES), topk, kc, IDX_W_LANE0)
    o = _masked_attention(q.reshape(bsz, L, C_WIDTH), k.reshape(bsz, L, C_WIDTH),
                          v.reshape(bsz, L, C_WIDTH), bias, gate.reshape(bsz, L, C_WIDTH), kc)
    return _matmul([o.reshape(m, C_WIDTH)], [w_out], D_MODEL, F32, res=x2d, name="odd_out")


def kernel(x, even_norm, even_w_in, s5_lambda_re, s5_lambda_im, s5_log_step, s5_b_re, s5_b_im,
           s5_c_re, s5_c_im, s5_d, s5_glu_w, s5_glu_b, even_w_out, odd_norm, odd_w_in,
           mla_q_norm, mla_kv_norm, idx_k_norm, mla_w_uq, mla_w_uk, mla_w_uv, idx_w_q,
           odd_w_out, final_norm):
    bsz, L, d = x.shape
    depth = even_norm.shape[0] + odd_norm.shape[0]
    tables = _rope_tables(L)
    x2d = x.reshape(bsz * L, d)
    for layer in range(depth):
        i = layer // 2
        if layer % 2 == 0:
            x2d = _even_layer(x2d, bsz, L, even_norm[i], even_w_in[i], s5_lambda_re[i],
                              s5_lambda_im[i], s5_log_step[i], s5_b_re[i], s5_b_im[i],
                              s5_c_re[i], s5_c_im[i], s5_d[i], s5_glu_w[i], s5_glu_b[i],
                              even_w_out[i], tables)
        else:
            x2d = _odd_layer(x2d, bsz, L, odd_norm[i], odd_w_in[i], mla_q_norm[i],
                             mla_kv_norm[i], idx_k_norm[i], mla_w_uq[i], mla_w_uk[i],
                             mla_w_uv[i], idx_w_q[i], odd_w_out[i], tables)
    return _rmsnorm(x2d, final_norm, F32).reshape(bsz, L, d)
```

```python
import functools
import math

import jax
import jax.numpy as jnp
from jax import lax
from jax.experimental import pallas as pl
from jax.experimental.pallas import tpu as pltpu

F32 = jnp.float32
BF16 = jnp.bfloat16

LANES = 128
SUBLANES = 8

D_MODEL = 4096
EPS = 1e-6
ROPE_THETA = 500000.0
HEAD_DIM = 128
ROT_DIM = HEAD_DIM // 4
ROT_HALF = ROT_DIM // 2

S5_WIDTH = D_MODEL // 4
S5_GROUP = 16
S5_GROUPS = S5_WIDTH // S5_GROUP
S5_STATE = 64
S5_BLOCKS = S5_WIDTH // 128
S5_BLOCK_GROUPS = S5_GROUPS // S5_BLOCKS
S5_BLOCK_STATES = S5_BLOCK_GROUPS * S5_STATE

DIL_PATTERNS = ((128, 1), (512, 4), (2048, 16))
N_DIL = len(DIL_PATTERNS)
DIL_HEADS = D_MODEL // 512
DIL_WIDTH = DIL_HEADS * HEAD_DIM
QKV_WIDTH = 3 * N_DIL * DIL_WIDTH

MLA_HEADS = D_MODEL // HEAD_DIM
MLA_NOPE = HEAD_DIM - ROT_DIM
Q_RANK = 1536
KV_RANK = 512
IDX_HEADS = D_MODEL // 128
IDX_DIM = 128
IDX_TOPK_MAX = 256
C_WIDTH = MLA_HEADS * HEAD_DIM
ODD_GATE0 = Q_RANK + KV_RANK + ROT_DIM + IDX_DIM + IDX_HEADS
ODD_SMALL = Q_RANK + KV_RANK + 2 * LANES
IDX_W_LANE0 = ROT_DIM + IDX_DIM - LANES

Q_TILE = 128
ATT_ROWS = 256
ATT_HEADS = 4
IDX_ROWS = 256
MASK_NEG = -1e30
LOG2E = 1.4426950408889634
INT_MIN = -(2 ** 31)


def _params(n_grid, vmem_mb):
    return pltpu.CompilerParams(
        dimension_semantics=("arbitrary",) * n_grid,
        vmem_limit_bytes=vmem_mb * 1024 * 1024)


def _rope_tables(L):
    inv = ROPE_THETA ** (-jnp.arange(0, ROT_DIM, 2, dtype=F32) / ROT_DIM)
    ang = jnp.arange(L, dtype=F32)[:, None] * inv[None, :]
    cos, sin = jnp.cos(ang), jnp.sin(ang)
    ones = jnp.ones((L, HEAD_DIM - ROT_DIM), F32)
    zeros = jnp.zeros((L, HEAD_DIM - ROT_DIM), F32)
    zh = jnp.zeros((L, ROT_HALF), F32)
    c = jnp.concatenate([cos, cos, ones], axis=1)
    sa = jnp.concatenate([-sin, zh, zeros], axis=1)
    sb = jnp.concatenate([zh, sin, zeros], axis=1)
    return c, sa, sb


def _rope_swap_tables(tables, width):
    c, sa, sb = tables
    col = jnp.arange(width)
    lane, head = col % HEAD_DIM, col // HEAD_DIM
    src = jnp.where(lane < ROT_HALF, col + ROT_HALF, col - ROT_HALF)
    sign = jnp.where(lane < ROT_HALF, -1.0, 1.0)
    perm = jnp.where((col[:, None] == src[None, :]) & (lane[None, :] < ROT_DIM)
                     & (head[:, None] == head[None, :]), sign[None, :], 0.0).astype(BF16)
    return c, sb - sa, perm


def _rope_head(x, c, sa, sb):
    return (x * c + pltpu.roll(x, HEAD_DIM - ROT_HALF, 1) * sa
            + pltpu.roll(x, ROT_HALF, 1) * sb)


def _rmsnorm_kernel(x_ref, g_ref, o_ref):
    x = x_ref[...]
    ms = jnp.mean(x * x, axis=-1, keepdims=True)
    o_ref[...] = (x * lax.rsqrt(ms + EPS) * g_ref[...]).astype(o_ref.dtype)


def _rmsnorm(x, g, out_dtype, tm=256):
    m, d = x.shape
    return pl.pallas_call(
        _rmsnorm_kernel,
        out_shape=jax.ShapeDtypeStruct((m, d), out_dtype),
        grid=(m // tm,),
        in_specs=[pl.BlockSpec((tm, d), lambda i: (i, 0)),
                  pl.BlockSpec((1, d), lambda i: (0, 0))],
        out_specs=pl.BlockSpec((tm, d), lambda i: (i, 0)),
        compiler_params=_params(1, 32),
        name="rmsnorm",
    )(x, g.reshape(1, d))


def _mm_kernel(*refs, n_a, has_res, epilogue, scale, tiles_per_kind, w_transposed):
    a_refs = refs[:n_a]
    w_refs = refs[n_a:2 * n_a]
    pos = 2 * n_a
    res_ref = None
    if has_res:
        res_ref = refs[pos]
        pos += 1
    extra = refs[pos:-1]
    o_ref = refs[-1]

    acc = None
    for a_ref, w_ref in zip(a_refs, w_refs):
        contract = (((1,), (1 if w_transposed else 0,)), ((), ()))
        d = lax.dot_general(a_ref[...], w_ref[...].astype(BF16), contract,
                            preferred_element_type=F32)
        acc = d if acc is None else acc + d
    if has_res:
        acc = acc + res_ref[...]
    n_heads = acc.shape[1] // HEAD_DIM

    def store_roped():
        c_ref, s_ref, perm_ref = extra
        swapped = jnp.dot(acc.astype(BF16), perm_ref[...], preferred_element_type=F32)
        for h in range(n_heads):
            hs = slice(h * HEAD_DIM, (h + 1) * HEAD_DIM)
            y = acc[:, hs] * c_ref[...] + swapped[:, hs] * s_ref[...]
            o_ref[:, hs] = (y * scale if scale != 1.0 else y).astype(o_ref.dtype)

    if epilogue == "none":
        o_ref[...] = acc.astype(o_ref.dtype)
    elif epilogue == "rope":
        store_roped()
    elif epilogue == "rope_qk":
        kind = (pl.program_id(1) // tiles_per_kind) % 3

        @pl.when(kind == 2)
        def _():
            o_ref[...] = acc.astype(o_ref.dtype)

        pl.when(kind != 2)(store_roped)
    elif epilogue == "add_head":
        kr = extra[0][...]
        cols = [acc[:, h * HEAD_DIM:(h + 1) * HEAD_DIM] + kr for h in range(n_heads)]
        o_ref[...] = jnp.concatenate(cols, axis=1).astype(o_ref.dtype)
    else:
        raise ValueError(epilogue)


def _matmul(a_list, w_list, n, out_dtype, *, w_col0=0, w_row_blocks=None, res=None,
            epilogue="none", extra=(), scale=1.0, seq_len=None, w_transposed=False,
            tn=512, vmem_mb=52, name="matmul"):
    m = a_list[0].shape[0]
    deep = sum(a.shape[1] for a in a_list) >= D_MODEL
    tm = 1024 if deep else 2048
    tm = min(tm, m)
    assert m % tm == 0 and n % tn == 0
    n_a = len(a_list)
    if w_row_blocks is None:
        w_row_blocks = [0] * n_a
    in_specs, args = [], []
    for a in a_list:
        k = a.shape[1]
        mode = {} if deep else {"pipeline_mode": pl.Buffered(1)}
        in_specs.append(pl.BlockSpec((tm, k), lambda i, j: (i, 0), **mode))
        args.append(a)
    for a, w, rb in zip(a_list, w_list, w_row_blocks):
        k = a.shape[1]
        if not w_transposed:
            assert w_col0 % tn == 0
            in_specs.append(pl.BlockSpec((k, tn),
                                         lambda i, j, rb=rb: (rb, j + w_col0 // tn)))
        elif w_col0 % tn == 0:
            in_specs.append(pl.BlockSpec((tn, k), lambda i, j: (j + w_col0 // tn, 0)))
        else:
            g = math.gcd(w_col0, tn)
            assert g % SUBLANES == 0
            in_specs.append(pl.BlockSpec(
                (pl.Element(tn), pl.Element(k)),
                lambda i, j, g=g: ((w_col0 // g + j * (tn // g)) * g, 0)))
        args.append(w)
    if res is not None:
        in_specs.append(pl.BlockSpec((tm, tn), lambda i, j: (i, j)))
        args.append(res)
    tiles_per_seq = None if seq_len is None else seq_len // tm
    if epilogue in ("rope", "rope_qk"):
        extra = _rope_swap_tables(extra, tn)
    for e in extra:
        if e.shape == (tn, tn):
            in_specs.append(pl.BlockSpec((tn, tn), lambda i, j: (0, 0)))
        elif e.shape[0] == m:
            in_specs.append(pl.BlockSpec((tm, e.shape[1]), lambda i, j: (i, 0)))
        else:
            assert seq_len is not None and e.shape[0] == seq_len and seq_len % tm == 0
            in_specs.append(pl.BlockSpec((tm, e.shape[1]),
                                         lambda i, j: (i % tiles_per_seq, 0)))
        args.append(e)
    kern = functools.partial(_mm_kernel, n_a=n_a, has_res=res is not None,
                             epilogue=epilogue, scale=scale,
                             tiles_per_kind=DIL_WIDTH // tn, w_transposed=w_transposed)
    return pl.pallas_call(
        kern,
        out_shape=jax.ShapeDtypeStruct((m, n), out_dtype),
        grid=(m // tm, n // tn),
        in_specs=in_specs,
        out_specs=pl.BlockSpec((tm, tn), lambda i, j: (i, j)),
        compiler_params=_params(2, vmem_mb),
        name=name,
    )(*args)


S5_UNROLL = 4


def _s5_discretise(lam_re, lam_im, log_step, b_re, b_im, c_re, c_im, seg_len):
    lr = jnp.minimum(lam_re.astype(F32), -1e-4)
    li = lam_im.astype(F32)
    dt = jnp.exp(log_step.astype(F32))[:, None]
    mag = jnp.exp(lr * dt)
    ar = mag * jnp.cos(li * dt)
    ai = mag * jnp.sin(li * dt)
    den = lr * lr + li * li
    nr, ni = ar - 1.0, ai
    cr = (nr * lr + ni * li) / den
    ci = (ni * lr - nr * li) / den
    br32, bi32 = b_re.astype(F32), b_im.astype(F32)
    bbr = cr[..., None] * br32 - ci[..., None] * bi32
    bbi = cr[..., None] * bi32 + ci[..., None] * br32

    eye = jnp.eye(S5_BLOCK_GROUPS, dtype=F32)

    def in_blockdiag(t):
        t = t.reshape(S5_BLOCKS, S5_BLOCK_GROUPS, S5_STATE, S5_GROUP)
        t = jnp.einsum("bgpj,gh->bgjhp", t, eye)
        return t.reshape(S5_BLOCKS, 128, S5_BLOCK_STATES)

    def out_blockdiag(t):
        t = t.reshape(S5_BLOCKS, S5_BLOCK_GROUPS, S5_GROUP, S5_STATE)
        t = jnp.einsum("bgjp,gh->bgphj", t, eye)
        return t.reshape(S5_BLOCKS, S5_BLOCK_STATES, 128)

    b_mat = jnp.concatenate([in_blockdiag(bbr), in_blockdiag(bbi)], axis=2).astype(BF16)
    c_mat = jnp.concatenate([out_blockdiag(c_re.astype(F32)),
                             -out_blockdiag(c_im.astype(F32))], axis=1).astype(BF16)

    a_r = ar.reshape(S5_BLOCKS, S5_BLOCK_STATES)
    a_i = ai.reshape(S5_BLOCKS, S5_BLOCK_STATES)
    s_r, s_i = jnp.ones_like(a_r), jnp.zeros_like(a_i)
    b_r, b_i, e = a_r, a_i, seg_len
    while e:
        if e & 1:
            s_r, s_i = s_r * b_r - s_i * b_i, s_r * b_i + s_i * b_r
        b_r, b_i = b_r * b_r - b_i * b_i, 2.0 * b_r * b_i
        e >>= 1
    pad = jnp.zeros((S5_BLOCKS, SUBLANES - 2, S5_BLOCK_STATES), F32)
    pow_re = jnp.concatenate([a_r[:, None], s_r[:, None], pad], axis=1)
    pow_im = jnp.concatenate([a_i[:, None], s_i[:, None], pad], axis=1)
    return b_mat, c_mat, pow_re, pow_im


def _s5_kernel(u_ref, b_ref, c_ref, pre_ref, pim_ref, d_ref, y_ref, up_ref, s_ref, yp_ref):
    L = u_ref.shape[1]
    ns = S5_BLOCK_STATES
    seg = L // SUBLANES

    def gather(r, _):
        up_ref[pl.ds(pl.multiple_of(r * SUBLANES, SUBLANES), SUBLANES), :] = (
            u_ref[0, pl.ds(r, SUBLANES, stride=seg), :])
        return 0

    lax.fori_loop(0, seg, gather, 0, unroll=S5_UNROLL)
    u = up_ref[...]
    s_ref[...] = jnp.dot(u.astype(BF16), b_ref[0], preferred_element_type=F32)

    pre, pim = pre_ref[0], pim_ref[0]
    ar = jnp.broadcast_to(pre[0:1, :], (SUBLANES, ns))
    ai = jnp.broadcast_to(pim[0:1, :], (SUBLANES, ns))

    def load(r, half):
        return s_ref[pl.ds(pl.multiple_of(r * SUBLANES, SUBLANES), SUBLANES),
                     half * ns:(half + 1) * ns]

    def store(r, half, x):
        s_ref[pl.ds(pl.multiple_of(r * SUBLANES, SUBLANES), SUBLANES),
              half * ns:(half + 1) * ns] = x

    def local(r, carry):
        sr, si = carry
        sr, si = ar * sr - ai * si + load(r, 0), ar * si + ai * sr + load(r, 1)
        store(r, 0, sr)
        store(r, 1, si)
        return sr, si

    zeros = jnp.zeros((SUBLANES, ns), F32)
    er, ei = lax.fori_loop(0, seg, local, (zeros, zeros), unroll=S5_UNROLL)

    gr, gi = pre[1:2, :], pim[1:2, :]
    cr, ci = [jnp.zeros((1, ns), F32)], [jnp.zeros((1, ns), F32)]
    for j in range(1, SUBLANES):
        pr, pi = cr[-1], ci[-1]
        cr.append(er[j - 1:j, :] + gr * pr - gi * pi)
        ci.append(ei[j - 1:j, :] + gr * pi + gi * pr)
    dr, di = jnp.concatenate(cr, axis=0), jnp.concatenate(ci, axis=0)

    def carried(r, carry):
        dr, di = carry
        dr, di = ar * dr - ai * di, ar * di + ai * dr
        store(r, 0, load(r, 0) + dr)
        store(r, 1, load(r, 1) + di)
        return dr, di

    lax.fori_loop(0, seg, carried, (dr, di), unroll=S5_UNROLL)

    y = jnp.dot(s_ref[...].astype(BF16), c_ref[0], preferred_element_type=F32)
    yp_ref[...] = y + d_ref[0] * u
    for j in range(SUBLANES):
        y_ref[0, j * seg:(j + 1) * seg, :] = yp_ref[pl.ds(j, seg, stride=SUBLANES), :]


def _s5(proj_a, b_mat, c_mat, pow_re, pow_im, d_skip):
    bsz, L, _ = proj_a.shape
    ns = S5_BLOCK_STATES
    return pl.pallas_call(
        _s5_kernel,
        out_shape=jax.ShapeDtypeStruct((bsz, L, S5_WIDTH), F32),
        grid=(bsz, S5_BLOCKS),
        in_specs=[
            pl.BlockSpec((1, L, 128), lambda b, g: (b, 0, g)),
            pl.BlockSpec((1, 128, 2 * ns), lambda b, g: (g, 0, 0)),
            pl.BlockSpec((1, 2 * ns, 128), lambda b, g: (g, 0, 0)),
            pl.BlockSpec((1, SUBLANES, ns), lambda b, g: (g, 0, 0)),
            pl.BlockSpec((1, SUBLANES, ns), lambda b, g: (g, 0, 0)),
            pl.BlockSpec((1, 1, 128), lambda b, g: (g, 0, 0)),
        ],
        out_specs=pl.BlockSpec((1, L, 128), lambda b, g: (b, 0, g)),
        scratch_shapes=[pltpu.VMEM((L, LANES), F32),
                        pltpu.VMEM((L, 2 * ns), F32),
                        pltpu.VMEM((L, LANES), F32)],
        compiler_params=_params(2, 40),
        name="s5_scan",
    )(proj_a, b_mat, c_mat, pow_re, pow_im, d_skip.reshape(S5_BLOCKS, 1, 128))


def _glu_kernel(y_ref, z_ref, w_ref, b_ref, o_ref):
    g = jax.nn.gelu(y_ref[...])
    t = jnp.dot(g.astype(BF16), w_ref[...].astype(BF16), preferred_element_type=F32) + b_ref[...]
    z = z_ref[...]
    gate = z / (1.0 + jnp.exp(-z))
    o_ref[...] = (g / (1.0 + jnp.exp(-t)) * gate).astype(o_ref.dtype)


def _glu(y, proj_a, glu_w, glu_b, tm=512):
    m = y.shape[0]
    return pl.pallas_call(
        _glu_kernel,
        out_shape=jax.ShapeDtypeStruct((m, S5_WIDTH), BF16),
        grid=(m // tm,),
        in_specs=[pl.BlockSpec((tm, S5_WIDTH), lambda i: (i, 0)),
                  pl.BlockSpec((tm, S5_WIDTH), lambda i: (i, 1)),
                  pl.BlockSpec((S5_WIDTH, S5_WIDTH), lambda i: (0, 0)),
                  pl.BlockSpec((1, S5_WIDTH), lambda i: (0, 0))],
        out_specs=pl.BlockSpec((tm, S5_WIDTH), lambda i: (i, 0)),
        compiler_params=_params(1, 40),
        name="s5_glu",
    )(y, proj_a, glu_w, glu_b.reshape(1, S5_WIDTH))


DIL_UNROLL = 8
MERGE_ROWS = 256


def _dil_kernel(*refs, L):
    qkv_refs = refs[:3 * N_DIL]
    z_ref, out_ref, og_ref, lg_ref, cls_ref = refs[3 * N_DIL:]
    n_tiles = L // Q_TILE

    for g, (window, dil) in enumerate(DIL_PATTERNS):
        n = L // dil
        nb = n // Q_TILE
        cw = Q_TILE if nb == 1 else 2 * Q_TILE
        half = (window // dil) // 2

        def rows(start, size, dil=dil):
            return pl.ds(start, size) if dil == 1 else pl.ds(start, size, stride=dil)

        for kind in range(3):
            for r in range(dil):
                cls_ref[kind, r * n:(r + 1) * n, :] = (
                    qkv_refs[3 * g + kind][0, rows(r, n), :].astype(BF16))

        def scores(it, n=n, nb=nb, cw=cw, half=half, dil=dil):
            r = it // nb
            qb = it % nb
            kb = jnp.clip(qb - 1, 0, max(nb - 2, 0))
            qc = pl.multiple_of(r * n + qb * Q_TILE, Q_TILE)
            kc = pl.multiple_of(r * n + kb * Q_TILE, Q_TILE)
            qpos = qb * Q_TILE + lax.broadcasted_iota(jnp.int32, (Q_TILE, cw), 0)
            kpos = kb * Q_TILE + lax.broadcasted_iota(jnp.int32, (Q_TILE, cw), 1)
            valid = jnp.abs(qpos - kpos - half) <= half
            s = lax.dot_general(cls_ref[0, pl.ds(qc, Q_TILE), :], cls_ref[1, pl.ds(kc, cw), :],
                                (((1,), (1,)), ((), ())),
                                preferred_element_type=F32) * (HEAD_DIM ** -0.5)
            return jnp.where(valid, s, MASK_NEG), kc, r + qb * (Q_TILE * dil)

        def softmax(s):
            m = jnp.max(s, axis=1, keepdims=True)
            p = jnp.exp(s - m)
            l = jnp.sum(p, axis=1, keepdims=True)
            return p.astype(BF16), l, m + jnp.log(l)

        def unrolled(j, _, g=g, cw=cw, rows=rows, scores=scores, softmax=softmax):
            blocks = [scores(j * DIL_UNROLL + u) for u in range(DIL_UNROLL)]
            probs = [softmax(s) for s, _, _ in blocks]
            for (_, kc, q0), (p, l, lse) in zip(blocks, probs):
                o = jnp.dot(p, cls_ref[2, pl.ds(kc, cw), :], preferred_element_type=F32) / l
                og_ref[g, rows(q0, Q_TILE), :] = o
                lg_ref[g, rows(q0, Q_TILE), :] = jnp.broadcast_to(lse, (Q_TILE, HEAD_DIM))
            return 0

        lax.fori_loop(0, n_tiles // DIL_UNROLL, unrolled, 0)

    def merge(j, _):
        rs = pl.ds(pl.multiple_of(j * MERGE_ROWS, MERGE_ROWS), MERGE_ROWS)
        a = [lg_ref[g, rs, :] for g in range(N_DIL)]
        m = functools.reduce(jnp.maximum, a)
        e = [jnp.exp(x - m) for x in a]
        o = sum(e[g] * og_ref[g, rs, :] for g in range(N_DIL)) / sum(e)
        z = z_ref[0, rs, :]
        out_ref[0, rs, :] = (o * (z / (1.0 + jnp.exp(-z)))).astype(out_ref.dtype)
        return 0

    lax.fori_loop(0, L // MERGE_ROWS, merge, 0)


def _dilated_attention(qkv, proj_b):
    bsz, L, _ = qkv.shape
    for window, dil in DIL_PATTERNS:
        assert (L // dil) % Q_TILE == 0 and (window // dil) % 2 == 0
    assert (L // Q_TILE) % DIL_UNROLL == 0 and L % MERGE_ROWS == 0

    def in_spec(g, kind):
        return pl.BlockSpec((1, L, HEAD_DIM),
                            lambda b, h: (b, 0, (g * 3 + kind) * DIL_HEADS + h))

    head_spec = pl.BlockSpec((1, L, HEAD_DIM), lambda b, h: (b, 0, h))
    return pl.pallas_call(
        functools.partial(_dil_kernel, L=L),
        out_shape=jax.ShapeDtypeStruct((bsz, L, DIL_WIDTH), BF16),
        grid=(bsz, DIL_HEADS),
        in_specs=[in_spec(g, kind) for g in range(N_DIL) for kind in range(3)] + [head_spec],
        out_specs=head_spec,
        scratch_shapes=[pltpu.VMEM((N_DIL, L, HEAD_DIM), F32),
                        pltpu.VMEM((N_DIL, L, HEAD_DIM), F32),
                        pltpu.VMEM((3, L, HEAD_DIM), BF16)],
        compiler_params=_params(2, 48),
        name="dilated_attention",
    )(*([qkv] * (3 * N_DIL)), proj_b)


def _even_layer(x2d, bsz, L, norm, w_in, lam_re, lam_im, log_step, b_re, b_im, c_re, c_im,
                d_skip, glu_w, glu_b, w_out, tables):
    h = _rmsnorm(x2d, norm, BF16)
    proj_a = _matmul([h], [w_in], 2 * S5_WIDTH, F32, w_col0=0, name="even_in_a")
    qkv = _matmul([h], [w_in], QKV_WIDTH, F32, w_col0=2 * S5_WIDTH, epilogue="rope_qk",
                  extra=tables, seq_len=L, name="even_in_qkv")
    proj_b = _matmul([h], [w_in], DIL_WIDTH, F32, w_col0=2 * S5_WIDTH + QKV_WIDTH,
                     name="even_in_zb")

    b_mat, c_mat, pow_re, pow_im = _s5_discretise(lam_re, lam_im, log_step, b_re, b_im,
                                                  c_re, c_im, L // SUBLANES)
    y = _s5(proj_a.reshape(bsz, L, 2 * S5_WIDTH), b_mat, c_mat, pow_re, pow_im, d_skip)
    a_out = _glu(y.reshape(bsz * L, S5_WIDTH), proj_a, glu_w, glu_b)

    b_out = _dilated_attention(qkv.reshape(bsz, L, QKV_WIDTH),
                               proj_b.reshape(bsz, L, DIL_WIDTH)).reshape(bsz * L, DIL_WIDTH)

    return _matmul([a_out, b_out], [w_out, w_out], D_MODEL, F32, w_row_blocks=[0, 1],
                   res=x2d, name="even_out")


def _odd_prep_kernel(p_ref, qn_ref, kvn_ref, kin_ref, c_ref, sa_ref, sb_ref,
                     cq_ref, ckv_ref, kr_ref, ki_ref, wi_ref):
    def norm(x, g):
        ms = jnp.mean(x * x, axis=-1, keepdims=True)
        return x * lax.rsqrt(ms + EPS) * g

    c, sa, sb = c_ref[...], sa_ref[...], sb_ref[...]
    o = 0
    cq_ref[...] = norm(p_ref[:, o:o + Q_RANK], qn_ref[...]).astype(cq_ref.dtype)
    o += Q_RANK
    ckv_ref[...] = norm(p_ref[:, o:o + KV_RANK], kvn_ref[...]).astype(ckv_ref.dtype)
    o += KV_RANK
    lo = p_ref[:, o:o + LANES]
    hi = p_ref[:, o + LANES:o + 2 * LANES]
    lane = lax.broadcasted_iota(jnp.int32, lo.shape, 1)
    kr_ref[...] = _rope_head(jnp.where(lane < ROT_DIM, lo, 0.0), c, sa, sb)
    k_i = jnp.where(lane < LANES - ROT_DIM, pltpu.roll(lo, LANES - ROT_DIM, 1),
                    pltpu.roll(hi, LANES - ROT_DIM, 1))
    ki_ref[...] = _rope_head(norm(k_i, kin_ref[...]), c, sa, sb).astype(ki_ref.dtype)
    wi_ref[...] = hi * (IDX_HEADS ** -0.5)


def _odd_prep(p_small, q_norm, kv_norm, k_idx_norm, tables, L, tm=256):
    m = p_small.shape[0]
    tps = L // tm
    row = lambda w: pl.BlockSpec((tm, w), lambda i: (i, 0))
    tab = pl.BlockSpec((tm, 128), lambda i: (i % tps, 0))
    vec = lambda w: pl.BlockSpec((1, w), lambda i: (0, 0))
    return pl.pallas_call(
        _odd_prep_kernel,
        out_shape=[jax.ShapeDtypeStruct((m, Q_RANK), BF16),
                   jax.ShapeDtypeStruct((m, KV_RANK), BF16),
                   jax.ShapeDtypeStruct((m, 128), F32),
                   jax.ShapeDtypeStruct((m, 128), BF16),
                   jax.ShapeDtypeStruct((m, 128), F32)],
        grid=(m // tm,),
        in_specs=[row(ODD_SMALL), vec(Q_RANK), vec(KV_RANK), vec(IDX_DIM), tab, tab, tab],
        out_specs=[row(Q_RANK), row(KV_RANK), row(128), row(128), row(128)],
        compiler_params=_params(1, 32),
        name="odd_prep",
    )(p_small, q_norm.reshape(1, -1), kv_norm.reshape(1, -1), k_idx_norm.reshape(1, -1),
      *tables)


def _float_sort_key(s):
    b = pltpu.bitcast(s, jnp.int32)
    return b ^ ((b >> 31) & 0x7FFFFFFF)


def _indexer_kernel(q_ref, k_ref, w_ref, bias_ref, key_ref, jmax_ref, *, L, topk, kc, w_lane0):
    i = pl.program_id(1)
    nch = L // kc
    rows = IDX_ROWS
    n_vis = ((i + 1) * rows + kc - 1) // kc
    qpos = i * rows + lax.broadcasted_iota(jnp.int32, (rows, 1), 0)
    lane = lax.broadcasted_iota(jnp.int32, (rows, kc), 1)
    lane_t = lax.broadcasted_iota(jnp.int32, (Q_TILE, kc), 1)
    qpos_t = lax.broadcasted_iota(jnp.int32, (Q_TILE, 1), 0)

    def chunk(c, _):
        k0 = pl.multiple_of(c * kc, kc)
        kblk = k_ref[0, pl.ds(k0, kc), :]
        for t in range(rows // Q_TILE):
            rs = slice(t * Q_TILE, (t + 1) * Q_TILE)
            acc = jnp.zeros((Q_TILE, kc), F32)
            for h in range(IDX_HEADS):
                qh = q_ref[0, rs, h * IDX_DIM:(h + 1) * IDX_DIM]
                s = lax.dot_general(qh, kblk, (((1,), (1,)), ((), ())),
                                    preferred_element_type=F32)
                acc = acc + w_ref[0, rs, w_lane0 + h:w_lane0 + h + 1] * jnp.maximum(s, 0.0)
            acc = jnp.where(acc == 0.0, 0.0, acc)
            visible = lane_t + k0 <= qpos_t + (i * rows + t * Q_TILE)
            key_ref[c, rs, :] = jnp.where(visible, _float_sort_key(acc), INT_MIN)
        return 0

    lax.fori_loop(0, n_vis, chunk, 0)

    def select(nv):
        def count(pred):
            tot = jnp.zeros((rows, LANES), F32)
            for c in range(nv):
                hit = jnp.where(pred(key_ref[c], c), 1.0, 0.0)
                for j in range(kc // LANES):
                    tot = tot + hit[:, j * LANES:(j + 1) * LANES]
            return jnp.sum(tot, axis=1, keepdims=True)

        def tbit(b, t_u):
            cand_u = t_u | jnp.left_shift(jnp.int32(1), 31 - b)
            cand_s = cand_u ^ INT_MIN
            cnt = count(lambda key, c: key >= cand_s)
            return jnp.where(cnt >= topk, cand_u, t_u)

        t_u = lax.fori_loop(0, 32, tbit, jnp.zeros((rows, 1), jnp.int32))
        thr = t_u ^ INT_MIN

        need = topk - count(lambda key, c: key > thr)
        n_eq = count(lambda key, c: key == thr)
        jbits = max(1, (L - 1).bit_length())
        jmax_ref[...] = jnp.full((rows, 1), (1 << jbits) - 1, jnp.int32)

        unsettled = jnp.where(thr == INT_MIN, 0.0, jnp.abs(n_eq - need))

        @pl.when(jnp.max(unsettled) > 0.0)
        def _():
            def jbit(b, j):
                cand = j | jnp.left_shift(jnp.int32(1), jbits - 1 - b)
                cnt = count(lambda key, c: jnp.where(lane + c * kc < cand, key, thr - 1) == thr)
                return jnp.where(cnt < need, cand, j)

            jmax_ref[...] = lax.fori_loop(0, jbits, jbit, jnp.zeros((rows, 1), jnp.int32))

        jlim = jnp.minimum(jmax_ref[...], qpos)
        for c in range(nv):
            key = key_ref[c]
            tie = jnp.where(lane + c * kc <= jlim, 0.0, MASK_NEG)
            bias = jnp.where(key > thr, 0.0, jnp.where(key == thr, tie, MASK_NEG))
            for t in range(rows // Q_TILE):
                bias_ref[0, t, c] = bias[t * Q_TILE:(t + 1) * Q_TILE].astype(bias_ref.dtype)
        for c in range(nv, nch):
            for t in range(rows // Q_TILE):
                bias_ref[0, t, c] = jnp.full((Q_TILE, kc), MASK_NEG, bias_ref.dtype)

    for nv in range(1, nch + 1):
        pl.when(n_vis == nv)(functools.partial(select, nv))


def _indexer(q_idx, k_idx, w_idx, topk, kc, w_lane0):
    bsz, L, _ = q_idx.shape
    nq, nch = L // Q_TILE, L // kc
    rows = IDX_ROWS
    assert L % rows == 0 and rows % Q_TILE == 0
    return pl.pallas_call(
        functools.partial(_indexer_kernel, L=L, topk=topk, kc=kc, w_lane0=w_lane0),
        out_shape=jax.ShapeDtypeStruct((bsz, nq, nch, Q_TILE, kc), BF16),
        grid=(bsz, L // rows),
        in_specs=[pl.BlockSpec((1, rows, IDX_HEADS * IDX_DIM), lambda b, i: (b, i, 0)),
                  pl.BlockSpec((1, L, IDX_DIM), lambda b, i: (b, 0, 0)),
                  pl.BlockSpec((1, rows, LANES), lambda b, i: (b, i, 0))],
        out_specs=pl.BlockSpec((1, rows // Q_TILE, nch, Q_TILE, kc),
                               lambda b, i: (b, i, 0, 0, 0)),
        scratch_shapes=[pltpu.VMEM((nch, rows, kc), jnp.int32),
                        pltpu.VMEM((rows, 1), jnp.int32)],
        compiler_params=_params(2, 32),
        name="dsa_indexer",
    )(q_idx, k_idx, w_idx)


def _mla_kernel(q_ref, k_ref, v_ref, bias_ref, g_ref, o_ref, s_ref, mx_ref, acc_ref, *, kc, nh):
    i = pl.program_id(2)
    rows = ATT_ROWS
    n_vis = ((i + 1) * rows + kc - 1) // kc
    slabs = kc // LANES

    def fold(x, op):
        t = x[:, 0:LANES]
        for j in range(1, slabs):
            t = op(t, x[:, j * LANES:(j + 1) * LANES])
        return t

    mx_ref[...] = jnp.full(mx_ref.shape, MASK_NEG, F32)

    def scores(c, _):
        k0 = pl.multiple_of(c * kc, kc)
        bias = jnp.concatenate([bias_ref[0, t, c] for t in range(rows // Q_TILE)],
                               axis=0).astype(F32)
        raw = [lax.dot_general(q_ref[0, :, h * HEAD_DIM:(h + 1) * HEAD_DIM],
                               k_ref[0, pl.ds(k0, kc), h * HEAD_DIM:(h + 1) * HEAD_DIM],
                               (((1,), (1,)), ((), ())), preferred_element_type=F32)
               for h in range(nh)]
        for h in range(nh):
            s = raw[h] + bias
            s_ref[c, h] = s
            mx_ref[h] = jnp.maximum(mx_ref[h], fold(s, jnp.maximum))
        return 0

    lax.fori_loop(0, n_vis, scores, 0)

    for h in range(nh):
        m = jnp.max(mx_ref[h], axis=1, keepdims=True)
        mx_ref[h] = jnp.broadcast_to(m, (rows, LANES))
    acc_ref[...] = jnp.zeros(acc_ref.shape, F32)

    ones = jnp.ones((kc, LANES), BF16)

    def weights(c, _):
        k0 = pl.multiple_of(c * kc, kc)
        for h in range(nh):
            hs = slice(h * HEAD_DIM, (h + 1) * HEAD_DIM)
            m = mx_ref[h]
            s = s_ref[c, h]
            p = jnp.concatenate(
                [jnp.exp2(s[:, j * LANES:(j + 1) * LANES] - m) for j in range(slabs)], axis=1)
            v1 = jnp.concatenate([v_ref[0, pl.ds(k0, kc), hs], ones], axis=1)
            acc_ref[h] += jnp.dot(p.astype(BF16), v1, preferred_element_type=F32)
        return 0

    lax.fori_loop(0, n_vis, weights, 0)

    for h in range(nh):
        hs = slice(h * HEAD_DIM, (h + 1) * HEAD_DIM)
        g = g_ref[0, :, hs].astype(F32)
        o = acc_ref[h, :, 0:HEAD_DIM] / acc_ref[h, :, HEAD_DIM:HEAD_DIM + LANES]
        o_ref[0, :, hs] = (o * (g / (1.0 + jnp.exp(-g)))).astype(o_ref.dtype)


def _masked_attention(q, k, v, bias, gate, kc, nh=ATT_HEADS):
    bsz, L, _ = q.shape
    rows, nch = ATT_ROWS, L // kc
    assert L % rows == 0 and rows % Q_TILE == 0
    w = nh * HEAD_DIM
    qspec = pl.BlockSpec((1, rows, w), lambda b, g, i: (b, i, g))
    kvspec = pl.BlockSpec((1, L, w), lambda b, g, i: (b, 0, g))
    return pl.pallas_call(
        functools.partial(_mla_kernel, kc=kc, nh=nh),
        out_shape=jax.ShapeDtypeStruct((bsz, L, C_WIDTH), BF16),
        grid=(bsz, MLA_HEADS // nh, L // rows),
        in_specs=[qspec, kvspec, kvspec,
                  pl.BlockSpec((1, rows // Q_TILE, nch, Q_TILE, kc),
                               lambda b, g, i: (b, i, 0, 0, 0)),
                  qspec],
        out_specs=qspec,
        scratch_shapes=[pltpu.VMEM((nch, nh, rows, kc), F32),
                        pltpu.VMEM((nh, rows, LANES), F32),
                        pltpu.VMEM((nh, rows, HEAD_DIM + LANES), F32)],
        compiler_params=_params(3, 48),
        name="dsa_attention",
    )(q, k, v, bias, gate)


def _odd_layer(x2d, bsz, L, norm, w_in, q_norm, kv_norm, k_idx_norm, w_uq, w_uk, w_uv, w_iq,
               w_out, tables):
    m = bsz * L
    h = _rmsnorm(x2d, norm, BF16)

    w_in_t = w_in.T
    p_small = _matmul([h], [w_in_t], ODD_SMALL, F32, w_transposed=True, tn=256,
                      name="odd_in_small")
    gate = _matmul([h], [w_in_t], C_WIDTH, BF16, w_transposed=True, w_col0=ODD_GATE0,
                   name="odd_in_gate")
    c_q, c_kv, k_rope, k_idx, w_idx = _odd_prep(p_small, q_norm, kv_norm, k_idx_norm, tables, L)

    q = _matmul([c_q], [w_uq], C_WIDTH, BF16, epilogue="rope", extra=tables, seq_len=L,
                scale=LOG2E * HEAD_DIM ** -0.5, name="odd_q_up")
    q_idx = _matmul([c_q], [w_iq], IDX_HEADS * IDX_DIM, BF16, epilogue="rope", extra=tables,
                    seq_len=L, scale=IDX_DIM ** -0.5, name="odd_q_idx")

    w_k = jnp.pad(w_uk, ((0, 0), (0, 0), (ROT_DIM, 0))).reshape(KV_RANK, C_WIDTH)
    k = _matmul([c_kv], [w_k], C_WIDTH, BF16, epilogue="add_head", extra=(k_rope,),
                name="odd_k_up")
    v = _matmul([c_kv], [w_uv.reshape(KV_RANK, C_WIDTH)], C_WIDTH, BF16, name="odd_v_up")

    topk = min(IDX_TOPK_MAX, L // 4)
    kc = min(512, L)
    bias = _indexer(q_idx.reshape(bsz, L, -1), k_idx.reshape(bsz, L, IDX_DIM),
                    w_idx.reshape(bsz, L, LANES), topk, kc, IDX_W_LANE0)
    o = _masked_attention(q.reshape(bsz, L, C_WIDTH), k.reshape(bsz, L, C_WIDTH),
                          v.reshape(bsz, L, C_WIDTH), bias, gate.reshape(bsz, L, C_WIDTH), kc)
    return _matmul([o.reshape(m, C_WIDTH)], [w_out], D_MODEL, F32, res=x2d, name="odd_out")


def kernel(x, even_norm, even_w_in, s5_lambda_re, s5_lambda_im, s5_log_step, s5_b_re, s5_b_im,
           s5_c_re, s5_c_im, s5_d, s5_glu_w, s5_glu_b, even_w_out, odd_norm, odd_w_in,
           mla_q_norm, mla_kv_norm, idx_k_norm, mla_w_uq, mla_w_uk, mla_w_uv, idx_w_q,
           odd_w_out, final_norm):
    bsz, L, d = x.shape
    depth = even_norm.shape[0] + odd_norm.shape[0]
    tables = _rope_tables(L)
    x2d = x.reshape(bsz * L, d)
    for layer in range(depth):
        i = layer // 2
        if layer % 2 == 0:
            x2d = _even_layer(x2d, bsz, L, even_norm[i], even_w_in[i], s5_lambda_re[i],
                              s5_lambda_im[i], s5_log_step[i], s5_b_re[i], s5_b_im[i],
                              s5_c_re[i], s5_c_im[i], s5_d[i], s5_glu_w[i], s5_glu_b[i],
                              even_w_out[i], tables)
        else:
            x2d = _odd_layer(x2d, bsz, L, odd_norm[i], odd_w_in[i], mla_q_norm[i],
                             mla_kv_norm[i], idx_k_norm[i], mla_w_uq[i], mla_w_uk[i],
                             mla_w_uv[i], idx_w_q[i], odd_w_out[i], tables)
    return _rmsnorm(x2d, final_norm, F32).reshape(bsz, L, d)
```

```python
import functools
import math

import jax
import jax.numpy as jnp
from jax import lax
from jax.experimental import pallas as pl
from jax.experimental.pallas import tpu as pltpu

F32 = jnp.float32
BF16 = jnp.bfloat16

LANES = 128
SUBLANES = 8

D_MODEL = 4096
EPS = 1e-6
ROPE_THETA = 500000.0
HEAD_DIM = 128
ROT_DIM = HEAD_DIM // 4
ROT_HALF = ROT_DIM // 2

S5_WIDTH = D_MODEL // 4
S5_GROUP = 16
S5_GROUPS = S5_WIDTH // S5_GROUP
S5_STATE = 64
S5_BLOCKS = S5_WIDTH // 128
S5_BLOCK_GROUPS = S5_GROUPS // S5_BLOCKS
S5_BLOCK_STATES = S5_BLOCK_GROUPS * S5_STATE

DIL_PATTERNS = ((128, 1), (512, 4), (2048, 16))
N_DIL = len(DIL_PATTERNS)
DIL_HEADS = D_MODEL // 512
DIL_WIDTH = DIL_HEADS * HEAD_DIM
QKV_WIDTH = 3 * N_DIL * DIL_WIDTH

MLA_HEADS = D_MODEL // HEAD_DIM
MLA_NOPE = HEAD_DIM - ROT_DIM
Q_RANK = 1536
KV_RANK = 512
IDX_HEADS = D_MODEL // 128
IDX_DIM = 128
IDX_TOPK_MAX = 256
C_WIDTH = MLA_HEADS * HEAD_DIM
ODD_GATE0 = Q_RANK + KV_RANK + ROT_DIM + IDX_DIM + IDX_HEADS
ODD_SMALL = Q_RANK + KV_RANK + 2 * LANES
IDX_W_LANE0 = ROT_DIM + IDX_DIM - LANES

Q_TILE = 128
ATT_ROWS = 256
ATT_HEADS = 4
IDX_ROWS = 256
MASK_NEG = -1e30
LOG2E = 1.4426950408889634
INT_MIN = -(2 ** 31)


def _params(n_grid, vmem_mb):
    return pltpu.CompilerParams(
        dimension_semantics=("arbitrary",) * n_grid,
        vmem_limit_bytes=vmem_mb * 1024 * 1024)


def _rope_tables(L):
    inv = ROPE_THETA ** (-jnp.arange(0, ROT_DIM, 2, dtype=F32) / ROT_DIM)
    ang = jnp.arange(L, dtype=F32)[:, None] * inv[None, :]
    cos, sin = jnp.cos(ang), jnp.sin(ang)
    ones = jnp.ones((L, HEAD_DIM - ROT_DIM), F32)
    zeros = jnp.zeros((L, HEAD_DIM - ROT_DIM), F32)
    zh = jnp.zeros((L, ROT_HALF), F32)
    c = jnp.concatenate([cos, cos, ones], axis=1)
    sa = jnp.concatenate([-sin, zh, zeros], axis=1)
    sb = jnp.concatenate([zh, sin, zeros], axis=1)
    return c, sa, sb


def _rope_swap_tables(tables, width):
    c, sa, sb = tables
    col = jnp.arange(width)
    lane, head = col % HEAD_DIM, col // HEAD_DIM
    src = jnp.where(lane < ROT_HALF, col + ROT_HALF, col - ROT_HALF)
    sign = jnp.where(lane < ROT_HALF, -1.0, 1.0)
    perm = jnp.where((col[:, None] == src[None, :]) & (lane[None, :] < ROT_DIM)
                     & (head[:, None] == head[None, :]), sign[None, :], 0.0).astype(BF16)
    return c, sb - sa, perm


def _rope_head(x, c, sa, sb):
    return (x * c + pltpu.roll(x, HEAD_DIM - ROT_HALF, 1) * sa
            + pltpu.roll(x, ROT_HALF, 1) * sb)


def _rmsnorm_kernel(x_ref, g_ref, o_ref):
    x = x_ref[...]
    ms = jnp.mean(x * x, axis=-1, keepdims=True)
    o_ref[...] = (x * lax.rsqrt(ms + EPS) * g_ref[...]).astype(o_ref.dtype)


def _rmsnorm(x, g, out_dtype, tm=256):
    m, d = x.shape
    return pl.pallas_call(
        _rmsnorm_kernel,
        out_shape=jax.ShapeDtypeStruct((m, d), out_dtype),
        grid=(m // tm,),
        in_specs=[pl.BlockSpec((tm, d), lambda i: (i, 0)),
                  pl.BlockSpec((1, d), lambda i: (0, 0))],
        out_specs=pl.BlockSpec((tm, d), lambda i: (i, 0)),
        compiler_params=_params(1, 32),
        name="rmsnorm",
    )(x, g.reshape(1, d))


def _mm_kernel(*refs, n_a, has_res, epilogue, scale, tiles_per_kind, w_transposed):
    a_refs = refs[:n_a]
    w_refs = refs[n_a:2 * n_a]
    pos = 2 * n_a
    res_ref = None
    if has_res:
        res_ref = refs[pos]
        pos += 1
    extra = refs[pos:-1]
    o_ref = refs[-1]

    acc = None
    for a_ref, w_ref in zip(a_refs, w_refs):
        contract = (((1,), (1 if w_transposed else 0,)), ((), ()))
        d = lax.dot_general(a_ref[...], w_ref[...].astype(BF16), contract,
                            preferred_element_type=F32)
        acc = d if acc is None else acc + d
    if has_res:
        acc = acc + res_ref[...]
    n_heads = acc.shape[1] // HEAD_DIM

    def store_roped():
        c_ref, s_ref, perm_ref = extra
        for h in range(n_heads):
            hs = slice(h * HEAD_DIM, (h + 1) * HEAD_DIM)
            swapped = jnp.dot(acc[:, hs].astype(BF16), perm_ref[...],
                              preferred_element_type=F32)
            y = acc[:, hs] * c_ref[...] + swapped * s_ref[...]
            o_ref[:, hs] = (y * scale if scale != 1.0 else y).astype(o_ref.dtype)

    if epilogue == "none":
        o_ref[...] = acc.astype(o_ref.dtype)
    elif epilogue == "rope":
        store_roped()
    elif epilogue == "rope_qk":
        kind = (pl.program_id(1) // tiles_per_kind) % 3

        @pl.when(kind == 2)
        def _():
            o_ref[...] = acc.astype(o_ref.dtype)

        pl.when(kind != 2)(store_roped)
    elif epilogue == "add_head":
        kr = extra[0][...]
        cols = [acc[:, h * HEAD_DIM:(h + 1) * HEAD_DIM] + kr for h in range(n_heads)]
        o_ref[...] = jnp.concatenate(cols, axis=1).astype(o_ref.dtype)
    else:
        raise ValueError(epilogue)


def _matmul(a_list, w_list, n, out_dtype, *, w_col0=0, w_row_blocks=None, res=None,
            epilogue="none", extra=(), scale=1.0, seq_len=None, w_transposed=False,
            tn=512, vmem_mb=52, name="matmul"):
    m = a_list[0].shape[0]
    deep = sum(a.shape[1] for a in a_list) >= D_MODEL
    tm = 1024 if deep else 2048
    tm = min(tm, m)
    assert m % tm == 0 and n % tn == 0
    n_a = len(a_list)
    if w_row_blocks is None:
        w_row_blocks = [0] * n_a
    in_specs, args = [], []
    for a in a_list:
        k = a.shape[1]
        mode = {} if deep else {"pipeline_mode": pl.Buffered(1)}
        in_specs.append(pl.BlockSpec((tm, k), lambda i, j: (i, 0), **mode))
        args.append(a)
    for a, w, rb in zip(a_list, w_list, w_row_blocks):
        k = a.shape[1]
        if not w_transposed:
            assert w_col0 % tn == 0
            in_specs.append(pl.BlockSpec((k, tn),
                                         lambda i, j, rb=rb: (rb, j + w_col0 // tn)))
        elif w_col0 % tn == 0:
            in_specs.append(pl.BlockSpec((tn, k), lambda i, j: (j + w_col0 // tn, 0)))
        else:
            g = math.gcd(w_col0, tn)
            assert g % SUBLANES == 0
            in_specs.append(pl.BlockSpec(
                (pl.Element(tn), pl.Element(k)),
                lambda i, j, g=g: ((w_col0 // g + j * (tn // g)) * g, 0)))
        args.append(w)
    if res is not None:
        in_specs.append(pl.BlockSpec((tm, tn), lambda i, j: (i, j)))
        args.append(res)
    tiles_per_seq = None if seq_len is None else seq_len // tm
    if epilogue in ("rope", "rope_qk"):
        extra = _rope_swap_tables(extra, HEAD_DIM)
    for e in extra:
        if e.shape == (HEAD_DIM, HEAD_DIM):
            in_specs.append(pl.BlockSpec((HEAD_DIM, HEAD_DIM), lambda i, j: (0, 0)))
        elif e.shape[0] == m:
            in_specs.append(pl.BlockSpec((tm, e.shape[1]), lambda i, j: (i, 0)))
        else:
            assert seq_len is not None and e.shape[0] == seq_len and seq_len % tm == 0
            in_specs.append(pl.BlockSpec((tm, e.shape[1]),
                                         lambda i, j: (i % tiles_per_seq, 0)))
        args.append(e)
    kern = functools.partial(_mm_kernel, n_a=n_a, has_res=res is not None,
                             epilogue=epilogue, scale=scale,
                             tiles_per_kind=DIL_WIDTH // tn, w_transposed=w_transposed)
    return pl.pallas_call(
        kern,
        out_shape=jax.ShapeDtypeStruct((m, n), out_dtype),
        grid=(m // tm, n // tn),
        in_specs=in_specs,
        out_specs=pl.BlockSpec((tm, tn), lambda i, j: (i, j)),
        compiler_params=_params(2, vmem_mb),
        name=name,
    )(*args)


S5_UNROLL = 4


def _s5_discretise(lam_re, lam_im, log_step, b_re, b_im, c_re, c_im, seg_len):
    lr = jnp.minimum(lam_re.astype(F32), -1e-4)
    li = lam_im.astype(F32)
    dt = jnp.exp(log_step.astype(F32))[:, None]
    mag = jnp.exp(lr * dt)
    ar = mag * jnp.cos(li * dt)
    ai = mag * jnp.sin(li * dt)
    den = lr * lr + li * li
    nr, ni = ar - 1.0, ai
    cr = (nr * lr + ni * li) / den
    ci = (ni * lr - nr * li) / den
    br32, bi32 = b_re.astype(F32), b_im.astype(F32)
    bbr = cr[..., None] * br32 - ci[..., None] * bi32
    bbi = cr[..., None] * bi32 + ci[..., None] * br32

    eye = jnp.eye(S5_BLOCK_GROUPS, dtype=F32)

    def in_blockdiag(t):
        t = t.reshape(S5_BLOCKS, S5_BLOCK_GROUPS, S5_STATE, S5_GROUP)
        t = jnp.einsum("bgpj,gh->bgjhp", t, eye)
        return t.reshape(S5_BLOCKS, 128, S5_BLOCK_STATES)

    def out_blockdiag(t):
        t = t.reshape(S5_BLOCKS, S5_BLOCK_GROUPS, S5_GROUP, S5_STATE)
        t = jnp.einsum("bgjp,gh->bgphj", t, eye)
        return t.reshape(S5_BLOCKS, S5_BLOCK_STATES, 128)

    b_mat = jnp.concatenate([in_blockdiag(bbr), in_blockdiag(bbi)], axis=2).astype(BF16)
    c_mat = jnp.concatenate([out_blockdiag(c_re.astype(F32)),
                             -out_blockdiag(c_im.astype(F32))], axis=1).astype(BF16)

    a_r = ar.reshape(S5_BLOCKS, S5_BLOCK_STATES)
    a_i = ai.reshape(S5_BLOCKS, S5_BLOCK_STATES)
    s_r, s_i = jnp.ones_like(a_r), jnp.zeros_like(a_i)
    b_r, b_i, e = a_r, a_i, seg_len
    while e:
        if e & 1:
            s_r, s_i = s_r * b_r - s_i * b_i, s_r * b_i + s_i * b_r
        b_r, b_i = b_r * b_r - b_i * b_i, 2.0 * b_r * b_i
        e >>= 1
    pad = jnp.zeros((S5_BLOCKS, SUBLANES - 2, S5_BLOCK_STATES), F32)
    pow_re = jnp.concatenate([a_r[:, None], s_r[:, None], pad], axis=1)
    pow_im = jnp.concatenate([a_i[:, None], s_i[:, None], pad], axis=1)
    return b_mat, c_mat, pow_re, pow_im


def _s5_kernel(u_ref, b_ref, c_ref, pre_ref, pim_ref, d_ref, y_ref, up_ref, s_ref, yp_ref):
    L = u_ref.shape[1]
    ns = S5_BLOCK_STATES
    seg = L // SUBLANES

    def gather(r, _):
        up_ref[pl.ds(pl.multiple_of(r * SUBLANES, SUBLANES), SUBLANES), :] = (
            u_ref[0, pl.ds(r, SUBLANES, stride=seg), :])
        return 0

    lax.fori_loop(0, seg, gather, 0, unroll=S5_UNROLL)
    u = up_ref[...]
    s_ref[...] = jnp.dot(u.astype(BF16), b_ref[0], preferred_element_type=F32)

    pre, pim = pre_ref[0], pim_ref[0]
    ar = jnp.broadcast_to(pre[0:1, :], (SUBLANES, ns))
    ai = jnp.broadcast_to(pim[0:1, :], (SUBLANES, ns))

    def load(r, half):
        return s_ref[pl.ds(pl.multiple_of(r * SUBLANES, SUBLANES), SUBLANES),
                     half * ns:(half + 1) * ns]

    def store(r, half, x):
        s_ref[pl.ds(pl.multiple_of(r * SUBLANES, SUBLANES), SUBLANES),
              half * ns:(half + 1) * ns] = x

    def local(r, carry):
        sr, si = carry
        sr, si = ar * sr - ai * si + load(r, 0), ar * si + ai * sr + load(r, 1)
        store(r, 0, sr)
        store(r, 1, si)
        return sr, si

    zeros = jnp.zeros((SUBLANES, ns), F32)
    er, ei = lax.fori_loop(0, seg, local, (zeros, zeros), unroll=S5_UNROLL)

    gr, gi = pre[1:2, :], pim[1:2, :]
    cr, ci = [jnp.zeros((1, ns), F32)], [jnp.zeros((1, ns), F32)]
    for j in range(1, SUBLANES):
        pr, pi = cr[-1], ci[-1]
        cr.append(er[j - 1:j, :] + gr * pr - gi * pi)
        ci.append(ei[j - 1:j, :] + gr * pi + gi * pr)
    dr, di = jnp.concatenate(cr, axis=0), jnp.concatenate(ci, axis=0)

    def carried(r, carry):
        dr, di = carry
        dr, di = ar * dr - ai * di, ar * di + ai * dr
        store(r, 0, load(r, 0) + dr)
        store(r, 1, load(r, 1) + di)
        return dr, di

    lax.fori_loop(0, seg, carried, (dr, di), unroll=S5_UNROLL)

    y = jnp.dot(s_ref[...].astype(BF16), c_ref[0], preferred_element_type=F32)
    yp_ref[...] = y + d_ref[0] * u
    for j in range(SUBLANES):
        y_ref[0, j * seg:(j + 1) * seg, :] = yp_ref[pl.ds(j, seg, stride=SUBLANES), :]


def _s5(proj_a, b_mat, c_mat, pow_re, pow_im, d_skip):
    bsz, L, _ = proj_a.shape
    ns = S5_BLOCK_STATES
    return pl.pallas_call(
        _s5_kernel,
        out_shape=jax.ShapeDtypeStruct((bsz, L, S5_WIDTH), F32),
        grid=(bsz, S5_BLOCKS),
        in_specs=[
            pl.BlockSpec((1, L, 128), lambda b, g: (b, 0, g)),
            pl.BlockSpec((1, 128, 2 * ns), lambda b, g: (g, 0, 0)),
            pl.BlockSpec((1, 2 * ns, 128), lambda b, g: (g, 0, 0)),
            pl.BlockSpec((1, SUBLANES, ns), lambda b, g: (g, 0, 0)),
            pl.BlockSpec((1, SUBLANES, ns), lambda b, g: (g, 0, 0)),
            pl.BlockSpec((1, 1, 128), lambda b, g: (g, 0, 0)),
        ],
        out_specs=pl.BlockSpec((1, L, 128), lambda b, g: (b, 0, g)),
        scratch_shapes=[pltpu.VMEM((L, LANES), F32),
                        pltpu.VMEM((L, 2 * ns), F32),
                        pltpu.VMEM((L, LANES), F32)],
        compiler_params=_params(2, 40),
        name="s5_scan",
    )(proj_a, b_mat, c_mat, pow_re, pow_im, d_skip.reshape(S5_BLOCKS, 1, 128))


def _glu_kernel(y_ref, z_ref, w_ref, b_ref, o_ref):
    g = jax.nn.gelu(y_ref[...])
    t = jnp.dot(g.astype(BF16), w_ref[...].astype(BF16), preferred_element_type=F32) + b_ref[...]
    z = z_ref[...]
    gate = z / (1.0 + jnp.exp(-z))
    o_ref[...] = (g / (1.0 + jnp.exp(-t)) * gate).astype(o_ref.dtype)


def _glu(y, proj_a, glu_w, glu_b, tm=512):
    m = y.shape[0]
    return pl.pallas_call(
        _glu_kernel,
        out_shape=jax.ShapeDtypeStruct((m, S5_WIDTH), BF16),
        grid=(m // tm,),
        in_specs=[pl.BlockSpec((tm, S5_WIDTH), lambda i: (i, 0)),
                  pl.BlockSpec((tm, S5_WIDTH), lambda i: (i, 1)),
                  pl.BlockSpec((S5_WIDTH, S5_WIDTH), lambda i: (0, 0)),
                  pl.BlockSpec((1, S5_WIDTH), lambda i: (0, 0))],
        out_specs=pl.BlockSpec((tm, S5_WIDTH), lambda i: (i, 0)),
        compiler_params=_params(1, 40),
        name="s5_glu",
    )(y, proj_a, glu_w, glu_b.reshape(1, S5_WIDTH))


DIL_UNROLL = 8
MERGE_ROWS = 256


def _dil_kernel(*refs, L):
    qkv_refs = refs[:3 * N_DIL]
    z_ref, out_ref, og_ref, lg_ref, cls_ref = refs[3 * N_DIL:]
    n_tiles = L // Q_TILE

    for g, (window, dil) in enumerate(DIL_PATTERNS):
        n = L // dil
        nb = n // Q_TILE
        cw = Q_TILE if nb == 1 else 2 * Q_TILE
        half = (window // dil) // 2

        def rows(start, size, dil=dil):
            return pl.ds(start, size) if dil == 1 else pl.ds(start, size, stride=dil)

        for kind in range(3):
            for r in range(dil):
                cls_ref[kind, r * n:(r + 1) * n, :] = (
                    qkv_refs[3 * g + kind][0, rows(r, n), :].astype(BF16))

        def scores(it, n=n, nb=nb, cw=cw, half=half, dil=dil):
            r = it // nb
            qb = it % nb
            kb = jnp.clip(qb - 1, 0, max(nb - 2, 0))
            qc = pl.multiple_of(r * n + qb * Q_TILE, Q_TILE)
            kc = pl.multiple_of(r * n + kb * Q_TILE, Q_TILE)
            qpos = qb * Q_TILE + lax.broadcasted_iota(jnp.int32, (Q_TILE, cw), 0)
            kpos = kb * Q_TILE + lax.broadcasted_iota(jnp.int32, (Q_TILE, cw), 1)
            valid = jnp.abs(qpos - kpos - half) <= half
            s = lax.dot_general(cls_ref[0, pl.ds(qc, Q_TILE), :], cls_ref[1, pl.ds(kc, cw), :],
                                (((1,), (1,)), ((), ())),
                                preferred_element_type=F32) * (HEAD_DIM ** -0.5)
            return jnp.where(valid, s, MASK_NEG), kc, r + qb * (Q_TILE * dil)

        def softmax(s):
            m = jnp.max(s, axis=1, keepdims=True)
            p = jnp.exp(s - m)
            l = jnp.sum(p, axis=1, keepdims=True)
            return p.astype(BF16), l, m + jnp.log(l)

        def unrolled(j, _, g=g, cw=cw, rows=rows, scores=scores, softmax=softmax):
            blocks = [scores(j * DIL_UNROLL + u) for u in range(DIL_UNROLL)]
            probs = [softmax(s) for s, _, _ in blocks]
            for (_, kc, q0), (p, l, lse) in zip(blocks, probs):
                o = jnp.dot(p, cls_ref[2, pl.ds(kc, cw), :], preferred_element_type=F32) / l
                og_ref[g, rows(q0, Q_TILE), :] = o
                lg_ref[g, rows(q0, Q_TILE), :] = jnp.broadcast_to(lse, (Q_TILE, HEAD_DIM))
            return 0

        lax.fori_loop(0, n_tiles // DIL_UNROLL, unrolled, 0)

    def merge(j, _):
        rs = pl.ds(pl.multiple_of(j * MERGE_ROWS, MERGE_ROWS), MERGE_ROWS)
        a = [lg_ref[g, rs, :] for g in range(N_DIL)]
        m = functools.reduce(jnp.maximum, a)
        e = [jnp.exp(x - m) for x in a]
        o = sum(e[g] * og_ref[g, rs, :] for g in range(N_DIL)) / sum(e)
        z = z_ref[0, rs, :]
        out_ref[0, rs, :] = (o * (z / (1.0 + jnp.exp(-z)))).astype(out_ref.dtype)
        return 0

    lax.fori_loop(0, L // MERGE_ROWS, merge, 0)


def _dilated_attention(qkv, proj_b):
    bsz, L, _ = qkv.shape
    for window, dil in DIL_PATTERNS:
        assert (L // dil) % Q_TILE == 0 and (window // dil) % 2 == 0
    assert (L // Q_TILE) % DIL_UNROLL == 0 and L % MERGE_ROWS == 0

    def in_spec(g, kind):
        return pl.BlockSpec((1, L, HEAD_DIM),
                            lambda b, h: (b, 0, (g * 3 + kind) * DIL_HEADS + h))

    head_spec = pl.BlockSpec((1, L, HEAD_DIM), lambda b, h: (b, 0, h))
    return pl.pallas_call(
        functools.partial(_dil_kernel, L=L),
        out_shape=jax.ShapeDtypeStruct((bsz, L, DIL_WIDTH), BF16),
        grid=(bsz, DIL_HEADS),
        in_specs=[in_spec(g, kind) for g in range(N_DIL) for kind in range(3)] + [head_spec],
        out_specs=head_spec,
        scratch_shapes=[pltpu.VMEM((N_DIL, L, HEAD_DIM), F32),
                        pltpu.VMEM((N_DIL, L, HEAD_DIM), F32),
                        pltpu.VMEM((3, L, HEAD_DIM), BF16)],
        compiler_params=_params(2, 48),
        name="dilated_attention",
    )(*([qkv] * (3 * N_DIL)), proj_b)


def _out_norm_kernel(a0_ref, a1_ref, w0_ref, w1_ref, res_ref, g_ref, x_ref, h_ref):
    x = (jnp.dot(a0_ref[...], w0_ref[...], preferred_element_type=F32)
         + jnp.dot(a1_ref[...], w1_ref[...], preferred_element_type=F32) + res_ref[...])
    x_ref[...] = x
    ms = jnp.mean(x * x, axis=-1, keepdims=True)
    h_ref[...] = (x * lax.rsqrt(ms + EPS) * g_ref[...]).astype(h_ref.dtype)


def _even_out_norm(a_out, b_out, w_out, res, next_norm, tm=256):
    m, d = res.shape
    ka, kb = a_out.shape[1], b_out.shape[1]
    assert ka == kb and m % tm == 0
    w = w_out.astype(BF16)
    row = lambda k: pl.BlockSpec((tm, k), lambda i: (i, 0))
    wspec = lambda rb: pl.BlockSpec((ka, d), lambda i, rb=rb: (rb, 0),
                                    pipeline_mode=pl.Buffered(1))
    return pl.pallas_call(
        _out_norm_kernel,
        out_shape=[jax.ShapeDtypeStruct((m, d), F32), jax.ShapeDtypeStruct((m, d), BF16)],
        grid=(m // tm,),
        in_specs=[row(ka), row(kb), wspec(0), wspec(1), row(d),
                  pl.BlockSpec((1, d), lambda i: (0, 0))],
        out_specs=[row(d), row(d)],
        compiler_params=_params(1, 52),
        name="even_out_norm",
    )(a_out, b_out, w, w, res, next_norm.reshape(1, d))


def _even_layer(x2d, bsz, L, norm, w_in, lam_re, lam_im, log_step, b_re, b_im, c_re, c_im,
                d_skip, glu_w, glu_b, w_out, tables, h=None, next_norm=None):
    if h is None:
        h = _rmsnorm(x2d, norm, BF16)
    proj_a = _matmul([h], [w_in], 2 * S5_WIDTH, F32, w_col0=0, name="even_in_a")
    qkv = _matmul([h], [w_in], QKV_WIDTH, F32, w_col0=2 * S5_WIDTH, epilogue="rope_qk",
                  extra=tables, seq_len=L, name="even_in_qkv")
    proj_b = _matmul([h], [w_in], DIL_WIDTH, F32, w_col0=2 * S5_WIDTH + QKV_WIDTH,
                     name="even_in_zb")

    b_mat, c_mat, pow_re, pow_im = _s5_discretise(lam_re, lam_im, log_step, b_re, b_im,
                                                  c_re, c_im, L // SUBLANES)
    y = _s5(proj_a.reshape(bsz, L, 2 * S5_WIDTH), b_mat, c_mat, pow_re, pow_im, d_skip)
    a_out = _glu(y.reshape(bsz * L, S5_WIDTH), proj_a, glu_w, glu_b)

    b_out = _dilated_attention(qkv.reshape(bsz, L, QKV_WIDTH),
                               proj_b.reshape(bsz, L, DIL_WIDTH)).reshape(bsz * L, DIL_WIDTH)

    if next_norm is not None:
        return _even_out_norm(a_out, b_out, w_out, x2d, next_norm)
    return _matmul([a_out, b_out], [w_out, w_out], D_MODEL, F32, w_row_blocks=[0, 1],
                   res=x2d, name="even_out"), None


def _odd_prep_kernel(p_ref, qn_ref, kvn_ref, kin_ref, c_ref, sa_ref, sb_ref,
                     cq_ref, ckv_ref, kr_ref, ki_ref, wi_ref):
    def norm(x, g):
        ms = jnp.mean(x * x, axis=-1, keepdims=True)
        return x * lax.rsqrt(ms + EPS) * g

    c, sa, sb = c_ref[...], sa_ref[...], sb_ref[...]
    o = 0
    cq_ref[...] = norm(p_ref[:, o:o + Q_RANK], qn_ref[...]).astype(cq_ref.dtype)
    o += Q_RANK
    ckv_ref[...] = norm(p_ref[:, o:o + KV_RANK], kvn_ref[...]).astype(ckv_ref.dtype)
    o += KV_RANK
    lo = p_ref[:, o:o + LANES]
    hi = p_ref[:, o + LANES:o + 2 * LANES]
    lane = lax.broadcasted_iota(jnp.int32, lo.shape, 1)
    kr_ref[...] = _rope_head(jnp.where(lane < ROT_DIM, lo, 0.0), c, sa, sb)
    k_i = jnp.where(lane < LANES - ROT_DIM, pltpu.roll(lo, LANES - ROT_DIM, 1),
                    pltpu.roll(hi, LANES - ROT_DIM, 1))
    ki_ref[...] = _rope_head(norm(k_i, kin_ref[...]), c, sa, sb).astype(ki_ref.dtype)
    wi_ref[...] = hi * (IDX_HEADS ** -0.5)


def _odd_prep(p_small, q_norm, kv_norm, k_idx_norm, tables, L, tm=256):
    m = p_small.shape[0]
    tps = L // tm
    row = lambda w: pl.BlockSpec((tm, w), lambda i: (i, 0))
    tab = pl.BlockSpec((tm, 128), lambda i: (i % tps, 0))
    vec = lambda w: pl.BlockSpec((1, w), lambda i: (0, 0))
    return pl.pallas_call(
        _odd_prep_kernel,
        out_shape=[jax.ShapeDtypeStruct((m, Q_RANK), BF16),
                   jax.ShapeDtypeStruct((m, KV_RANK), BF16),
                   jax.ShapeDtypeStruct((m, 128), F32),
                   jax.ShapeDtypeStruct((m, 128), BF16),
                   jax.ShapeDtypeStruct((m, 128), F32)],
        grid=(m // tm,),
        in_specs=[row(ODD_SMALL), vec(Q_RANK), vec(KV_RANK), vec(IDX_DIM), tab, tab, tab],
        out_specs=[row(Q_RANK), row(KV_RANK), row(128), row(128), row(128)],
        compiler_params=_params(1, 32),
        name="odd_prep",
    )(p_small, q_norm.reshape(1, -1), kv_norm.reshape(1, -1), k_idx_norm.reshape(1, -1),
      *tables)


def _float_sort_key(s):
    b = pltpu.bitcast(s, jnp.int32)
    return b ^ ((b >> 31) & 0x7FFFFFFF)


def _indexer_kernel(q_ref, k_ref, w_ref, bias_ref, key_ref, jmax_ref, *, L, topk, kc, w_lane0):
    i = pl.program_id(1)
    nch = L // kc
    rows = IDX_ROWS
    n_vis = ((i + 1) * rows + kc - 1) // kc
    qpos = i * rows + lax.broadcasted_iota(jnp.int32, (rows, 1), 0)
    lane = lax.broadcasted_iota(jnp.int32, (rows, kc), 1)
    lane_t = lax.broadcasted_iota(jnp.int32, (Q_TILE, kc), 1)
    qpos_t = lax.broadcasted_iota(jnp.int32, (Q_TILE, 1), 0)

    def chunk(c, _):
        k0 = pl.multiple_of(c * kc, kc)
        kblk = k_ref[0, pl.ds(k0, kc), :]
        for t in range(rows // Q_TILE):
            rs = slice(t * Q_TILE, (t + 1) * Q_TILE)
            acc = jnp.zeros((Q_TILE, kc), F32)
            for h in range(IDX_HEADS):
                qh = q_ref[0, rs, h * IDX_DIM:(h + 1) * IDX_DIM]
                s = lax.dot_general(qh, kblk, (((1,), (1,)), ((), ())),
                                    preferred_element_type=F32)
                acc = acc + w_ref[0, rs, w_lane0 + h:w_lane0 + h + 1] * jnp.maximum(s, 0.0)
            acc = jnp.where(acc == 0.0, 0.0, acc)
            visible = lane_t + k0 <= qpos_t + (i * rows + t * Q_TILE)
            key_ref[c, rs, :] = jnp.where(visible, _float_sort_key(acc), INT_MIN)
        return 0

    lax.fori_loop(0, n_vis, chunk, 0)

    def select(nv):
        def count(pred):
            tot = jnp.zeros((rows, LANES), F32)
            for c in range(nv):
                hit = jnp.where(pred(key_ref[c], c), 1.0, 0.0)
                for j in range(kc // LANES):
                    tot = tot + hit[:, j * LANES:(j + 1) * LANES]
            return jnp.sum(tot, axis=1, keepdims=True)

        def tbit(b, t_u):
            cand_u = t_u | jnp.left_shift(jnp.int32(1), 31 - b)
            cand_s = cand_u ^ INT_MIN
            cnt = count(lambda key, c: key >= cand_s)
            return jnp.where(cnt >= topk, cand_u, t_u)

        t_u = lax.fori_loop(0, 32, tbit, jnp.zeros((rows, 1), jnp.int32))
        thr = t_u ^ INT_MIN

        need = topk - count(lambda key, c: key > thr)
        n_eq = count(lambda key, c: key == thr)
        jbits = max(1, (L - 1).bit_length())
        jmax_ref[...] = jnp.full((rows, 1), (1 << jbits) - 1, jnp.int32)

        unsettled = jnp.where(thr == INT_MIN, 0.0, jnp.abs(n_eq - need))

        @pl.when(jnp.max(unsettled) > 0.0)
        def _():
            def jbit(b, j):
                cand = j | jnp.left_shift(jnp.int32(1), jbits - 1 - b)
                cnt = count(lambda key, c: jnp.where(lane + c * kc < cand, key, thr - 1) == thr)
                return jnp.where(cnt < need, cand, j)

            jmax_ref[...] = lax.fori_loop(0, jbits, jbit, jnp.zeros((rows, 1), jnp.int32))

        jlim = jnp.minimum(jmax_ref[...], qpos)
        for c in range(nv):
            key = key_ref[c]
            tie = jnp.where(lane + c * kc <= jlim, 0.0, MASK_NEG)
            bias = jnp.where(key > thr, 0.0, jnp.where(key == thr, tie, MASK_NEG))
            for t in range(rows // Q_TILE):
                bias_ref[0, t, c] = bias[t * Q_TILE:(t + 1) * Q_TILE].astype(bias_ref.dtype)
        for c in range(nv, nch):
            for t in range(rows // Q_TILE):
                bias_ref[0, t, c] = jnp.full((Q_TILE, kc), MASK_NEG, bias_ref.dtype)

    for nv in range(1, nch + 1):
        pl.when(n_vis == nv)(functools.partial(select, nv))


def _indexer(q_idx, k_idx, w_idx, topk, kc, w_lane0):
    bsz, L, _ = q_idx.shape
    nq, nch = L // Q_TILE, L // kc
    rows = IDX_ROWS
    assert L % rows == 0 and rows % Q_TILE == 0
    return pl.pallas_call(
        functools.partial(_indexer_kernel, L=L, topk=topk, kc=kc, w_lane0=w_lane0),
        out_shape=jax.ShapeDtypeStruct((bsz, nq, nch, Q_TILE, kc), BF16),
        grid=(bsz, L // rows),
        in_specs=[pl.BlockSpec((1, rows, IDX_HEADS * IDX_DIM), lambda b, i: (b, i, 0)),
                  pl.BlockSpec((1, L, IDX_DIM), lambda b, i: (b, 0, 0)),
                  pl.BlockSpec((1, rows, LANES), lambda b, i: (b, i, 0))],
        out_specs=pl.BlockSpec((1, rows // Q_TILE, nch, Q_TILE, kc),
                               lambda b, i: (b, i, 0, 0, 0)),
        scratch_shapes=[pltpu.VMEM((nch, rows, kc), jnp.int32),
                        pltpu.VMEM((rows, 1), jnp.int32)],
        compiler_params=_params(2, 32),
        name="dsa_indexer",
    )(q_idx, k_idx, w_idx)


def _mla_kernel(q_ref, k_ref, v_ref, bias_ref, g_ref, o_ref, s_ref, mx_ref, acc_ref, *, kc, nh):
    i = pl.program_id(2)
    rows = ATT_ROWS
    n_vis = ((i + 1) * rows + kc - 1) // kc
    slabs = kc // LANES

    def fold(x, op):
        t = x[:, 0:LANES]
        for j in range(1, slabs):
            t = op(t, x[:, j * LANES:(j + 1) * LANES])
        return t

    mx_ref[...] = jnp.full(mx_ref.shape, MASK_NEG, F32)

    def scores(c, _):
        k0 = pl.multiple_of(c * kc, kc)
        bias = jnp.concatenate([bias_ref[0, t, c] for t in range(rows // Q_TILE)],
                               axis=0).astype(F32)
        raw = [lax.dot_general(q_ref[0, :, h * HEAD_DIM:(h + 1) * HEAD_DIM],
                               k_ref[0, pl.ds(k0, kc), h * HEAD_DIM:(h + 1) * HEAD_DIM],
                               (((1,), (1,)), ((), ())), preferred_element_type=F32)
               for h in range(nh)]
        for h in range(nh):
            s = raw[h] + bias
            s_ref[c, h] = s
            mx_ref[h] = jnp.maximum(mx_ref[h], fold(s, jnp.maximum))
        return 0

    lax.fori_loop(0, n_vis, scores, 0)

    for h in range(nh):
        m = jnp.max(mx_ref[h], axis=1, keepdims=True)
        mx_ref[h] = jnp.broadcast_to(m, (rows, LANES))
    acc_ref[...] = jnp.zeros(acc_ref.shape, F32)

    ones = jnp.ones((kc, LANES), BF16)

    def weights(c, _):
        k0 = pl.multiple_of(c * kc, kc)
        for h in range(nh):
            hs = slice(h * HEAD_DIM, (h + 1) * HEAD_DIM)
            m = mx_ref[h]
            s = s_ref[c, h]
            p = jnp.concatenate(
                [jnp.exp2(s[:, j * LANES:(j + 1) * LANES] - m) for j in range(slabs)], axis=1)
            v1 = jnp.concatenate([v_ref[0, pl.ds(k0, kc), hs], ones], axis=1)
            acc_ref[h] += jnp.dot(p.astype(BF16), v1, preferred_element_type=F32)
        return 0

    lax.fori_loop(0, n_vis, weights, 0)

    for h in range(nh):
        hs = slice(h * HEAD_DIM, (h + 1) * HEAD_DIM)
        g = g_ref[0, :, hs].astype(F32)
        o = acc_ref[h, :, 0:HEAD_DIM] / acc_ref[h, :, HEAD_DIM:HEAD_DIM + LANES]
        o_ref[0, :, hs] = (o * (g / (1.0 + jnp.exp(-g)))).astype(o_ref.dtype)


def _masked_attention(q, k, v, bias, gate, kc, nh=ATT_HEADS):
    bsz, L, _ = q.shape
    rows, nch = ATT_ROWS, L // kc
    assert L % rows == 0 and rows % Q_TILE == 0
    w = nh * HEAD_DIM
    qspec = pl.BlockSpec((1, rows, w), lambda b, g, i: (b, i, g))
    kvspec = pl.BlockSpec((1, L, w), lambda b, g, i: (b, 0, g))
    return pl.pallas_call(
        functools.partial(_mla_kernel, kc=kc, nh=nh),
        out_shape=jax.ShapeDtypeStruct((bsz, L, C_WIDTH), BF16),
        grid=(bsz, MLA_HEADS // nh, L // rows),
        in_specs=[qspec, kvspec, kvspec,
                  pl.BlockSpec((1, rows // Q_TILE, nch, Q_TILE, kc),
                               lambda b, g, i: (b, i, 0, 0, 0)),
                  qspec],
        out_specs=qspec,
        scratch_shapes=[pltpu.VMEM((nch, nh, rows, kc), F32),
                        pltpu.VMEM((nh, rows, LANES), F32),
                        pltpu.VMEM((nh, rows, HEAD_DIM + LANES), F32)],
        compiler_params=_params(3, 48),
        name="dsa_attention",
    )(q, k, v, bias, gate)


def _odd_layer(x2d, bsz, L, norm, w_in, q_norm, kv_norm, k_idx_norm, w_uq, w_uk, w_uv, w_iq,
               w_out, tables, h=None):
    m = bsz * L
    if h is None:
        h = _rmsnorm(x2d, norm, BF16)

    w_in_t = w_in.T
    p_small = _matmul([h], [w_in_t], ODD_SMALL, F32, w_transposed=True, tn=256,
                      name="odd_in_small")
    gate = _matmul([h], [w_in_t], C_WIDTH, BF16, w_transposed=True, w_col0=ODD_GATE0,
                   name="odd_in_gate")
    c_q, c_kv, k_rope, k_idx, w_idx = _odd_prep(p_small, q_norm, kv_norm, k_idx_norm, tables, L)

    q = _matmul([c_q], [w_uq], C_WIDTH, BF16, epilogue="rope", extra=tables, seq_len=L,
                scale=LOG2E * HEAD_DIM ** -0.5, name="odd_q_up")
    q_idx = _matmul([c_q], [w_iq], IDX_HEADS * IDX_DIM, BF16, epilogue="rope", extra=tables,
                    seq_len=L, scale=IDX_DIM ** -0.5, name="odd_q_idx")

    w_k = jnp.pad(w_uk, ((0, 0), (0, 0), (ROT_DIM, 0))).reshape(KV_RANK, C_WIDTH)
    k = _matmul([c_kv], [w_k], C_WIDTH, BF16, epilogue="add_head", extra=(k_rope,),
                name="odd_k_up")
    v = _matmul([c_kv], [w_uv.reshape(KV_RANK, C_WIDTH)], C_WIDTH, BF16, name="odd_v_up")

    topk = min(IDX_TOPK_MAX, L // 4)
    kc = min(512, L)
    bias = _indexer(q_idx.reshape(bsz, L, -1), k_idx.reshape(bsz, L, IDX_DIM),
                    w_idx.reshape(bsz, L, LANES), topk, kc, IDX_W_LANE0)
    o = _masked_attention(q.reshape(bsz, L, C_WIDTH), k.reshape(bsz, L, C_WIDTH),
                          v.reshape(bsz, L, C_WIDTH), bias, gate.reshape(bsz, L, C_WIDTH), kc)
    return _matmul([o.reshape(m, C_WIDTH)], [w_out], D_MODEL, F32, res=x2d, name="odd_out")


def kernel(x, even_norm, even_w_in, s5_lambda_re, s5_lambda_im, s5_log_step, s5_b_re, s5_b_im,
           s5_c_re, s5_c_im, s5_d, s5_glu_w, s5_glu_b, even_w_out, odd_norm, odd_w_in,
           mla_q_norm, mla_kv_norm, idx_k_norm, mla_w_uq, mla_w_uk, mla_w_uv, idx_w_q,
           odd_w_out, final_norm):
    bsz, L, d = x.shape
    depth = even_norm.shape[0] + odd_norm.shape[0]
    tables = _rope_tables(L)
    x2d = x.reshape(bsz * L, d)
    h = None
    for layer in range(depth):
        i = layer // 2
        if layer % 2 == 0:
            next_norm = odd_norm[i] if layer + 1 < depth else None
            x2d, h = _even_layer(x2d, bsz, L, even_norm[i], even_w_in[i], s5_lambda_re[i],
                                 s5_lambda_im[i], s5_log_step[i], s5_b_re[i], s5_b_im[i],
                                 s5_c_re[i], s5_c_im[i], s5_d[i], s5_glu_w[i], s5_glu_b[i],
                                 even_w_out[i], tables, h=h, next_norm=next_norm)
        else:
            x2d = _odd_layer(x2d, bsz, L, odd_norm[i], odd_w_in[i], mla_q_norm[i],
                             mla_kv_norm[i], idx_k_norm[i], mla_w_uq[i], mla_w_uk[i],
                             mla_w_uv[i], idx_w_q[i], odd_w_out[i], tables, h=h)
            h = None
    return _rmsnorm(x2d, final_norm, F32).reshape(bsz, L, d)
```

```python
import functools
import math

import jax
import jax.numpy as jnp
from jax import lax
from jax.experimental import pallas as pl
from jax.experimental.pallas import tpu as pltpu

F32 = jnp.float32
BF16 = jnp.bfloat16

LANES = 128
SUBLANES = 8

D_MODEL = 4096
EPS = 1e-6
ROPE_THETA = 500000.0
HEAD_DIM = 128
ROT_DIM = HEAD_DIM // 4
ROT_HALF = ROT_DIM // 2

S5_WIDTH = D_MODEL // 4
S5_GROUP = 16
S5_GROUPS = S5_WIDTH // S5_GROUP
S5_STATE = 64
S5_BLOCKS = S5_WIDTH // 128
S5_BLOCK_GROUPS = S5_GROUPS // S5_BLOCKS
S5_BLOCK_STATES = S5_BLOCK_GROUPS * S5_STATE

DIL_PATTERNS = ((128, 1), (512, 4), (2048, 16))
N_DIL = len(DIL_PATTERNS)
DIL_HEADS = D_MODEL // 512
DIL_WIDTH = DIL_HEADS * HEAD_DIM
QKV_WIDTH = 3 * N_DIL * DIL_WIDTH

MLA_HEADS = D_MODEL // HEAD_DIM
MLA_NOPE = HEAD_DIM - ROT_DIM
Q_RANK = 1536
KV_RANK = 512
IDX_HEADS = D_MODEL // 128
IDX_DIM = 128
IDX_TOPK_MAX = 256
C_WIDTH = MLA_HEADS * HEAD_DIM
ODD_GATE0 = Q_RANK + KV_RANK + ROT_DIM + IDX_DIM + IDX_HEADS
ODD_SMALL = Q_RANK + KV_RANK + 2 * LANES
IDX_W_LANE0 = ROT_DIM + IDX_DIM - LANES

Q_TILE = 128
ATT_ROWS = 256
ATT_HEADS = 4
IDX_ROWS = 256
MASK_NEG = -1e30
LOG2E = 1.4426950408889634
INT_MIN = -(2 ** 31)


def _params(n_grid, vmem_mb):
    return pltpu.CompilerParams(
        dimension_semantics=("arbitrary",) * n_grid,
        vmem_limit_bytes=vmem_mb * 1024 * 1024)


def _rope_tables(L):
    inv = ROPE_THETA ** (-jnp.arange(0, ROT_DIM, 2, dtype=F32) / ROT_DIM)
    ang = jnp.arange(L, dtype=F32)[:, None] * inv[None, :]
    cos, sin = jnp.cos(ang), jnp.sin(ang)
    ones = jnp.ones((L, HEAD_DIM - ROT_DIM), F32)
    zeros = jnp.zeros((L, HEAD_DIM - ROT_DIM), F32)
    zh = jnp.zeros((L, ROT_HALF), F32)
    c = jnp.concatenate([cos, cos, ones], axis=1)
    sa = jnp.concatenate([-sin, zh, zeros], axis=1)
    sb = jnp.concatenate([zh, sin, zeros], axis=1)
    return c, sa, sb


def _rope_swap_tables(tables, width):
    c, sa, sb = tables
    col = jnp.arange(width)
    lane, head = col % HEAD_DIM, col // HEAD_DIM
    src = jnp.where(lane < ROT_HALF, col + ROT_HALF, col - ROT_HALF)
    sign = jnp.where(lane < ROT_HALF, -1.0, 1.0)
    perm = jnp.where((col[:, None] == src[None, :]) & (lane[None, :] < ROT_DIM)
                     & (head[:, None] == head[None, :]), sign[None, :], 0.0).astype(BF16)
    return c, sb - sa, perm


def _rope_head(x, c, sa, sb):
    return (x * c + pltpu.roll(x, HEAD_DIM - ROT_HALF, 1) * sa
            + pltpu.roll(x, ROT_HALF, 1) * sb)


def _rmsnorm_kernel(x_ref, g_ref, o_ref):
    x = x_ref[...]
    ms = jnp.mean(x * x, axis=-1, keepdims=True)
    o_ref[...] = (x * lax.rsqrt(ms + EPS) * g_ref[...]).astype(o_ref.dtype)


def _rmsnorm(x, g, out_dtype, tm=256):
    m, d = x.shape
    return pl.pallas_call(
        _rmsnorm_kernel,
        out_shape=jax.ShapeDtypeStruct((m, d), out_dtype),
        grid=(m // tm,),
        in_specs=[pl.BlockSpec((tm, d), lambda i: (i, 0)),
                  pl.BlockSpec((1, d), lambda i: (0, 0))],
        out_specs=pl.BlockSpec((tm, d), lambda i: (i, 0)),
        compiler_params=_params(1, 32),
        name="rmsnorm",
    )(x, g.reshape(1, d))


def _mm_kernel(*refs, n_a, has_res, epilogue, scale, tiles_per_kind, qkv_tiles, w_transposed):
    a_refs = refs[:n_a]
    w_refs = refs[n_a:2 * n_a]
    pos = 2 * n_a
    res_ref = None
    if has_res:
        res_ref = refs[pos]
        pos += 1
    extra = refs[pos:-1]
    o_ref = refs[-1]

    acc = None
    for a_ref, w_ref in zip(a_refs, w_refs):
        contract = (((1,), (1 if w_transposed else 0,)), ((), ()))
        d = lax.dot_general(a_ref[...], w_ref[...].astype(BF16), contract,
                            preferred_element_type=F32)
        acc = d if acc is None else acc + d
    if has_res:
        acc = acc + res_ref[...]
    n_heads = acc.shape[1] // HEAD_DIM

    def store_roped():
        c_ref, s_ref, perm_ref = extra
        for h in range(n_heads):
            hs = slice(h * HEAD_DIM, (h + 1) * HEAD_DIM)
            swapped = jnp.dot(acc[:, hs].astype(BF16), perm_ref[...],
                              preferred_element_type=F32)
            y = acc[:, hs] * c_ref[...] + swapped * s_ref[...]
            o_ref[:, hs] = (y * scale if scale != 1.0 else y).astype(o_ref.dtype)

    if epilogue == "none":
        o_ref[...] = acc.astype(o_ref.dtype)
    elif epilogue == "rope":
        store_roped()
    elif epilogue == "rope_qk":
        first, last = qkv_tiles
        j = pl.program_id(1)
        rotary = (j >= first) & (j < last) & (((j - first) // tiles_per_kind) % 3 != 2)

        @pl.when(jnp.logical_not(rotary))
        def _():
            o_ref[...] = acc.astype(o_ref.dtype)

        pl.when(rotary)(store_roped)
    else:
        raise ValueError(epilogue)


def _matmul(a_list, w_list, n, out_dtype, *, w_col0=0, w_row_blocks=None, res=None,
            epilogue="none", extra=(), scale=1.0, seq_len=None, w_transposed=False,
            qkv_cols=(0, 0), tn=512, vmem_mb=52, name="matmul"):
    m = a_list[0].shape[0]
    deep = sum(a.shape[1] for a in a_list) >= D_MODEL
    tm = 1024 if deep else 2048
    tm = min(tm, m)
    assert m % tm == 0 and n % tn == 0
    n_a = len(a_list)
    if w_row_blocks is None:
        w_row_blocks = [0] * n_a
    in_specs, args = [], []
    for a in a_list:
        k = a.shape[1]
        mode = {} if deep else {"pipeline_mode": pl.Buffered(1)}
        in_specs.append(pl.BlockSpec((tm, k), lambda i, j: (i, 0), **mode))
        args.append(a)
    for a, w, rb in zip(a_list, w_list, w_row_blocks):
        k = a.shape[1]
        if not w_transposed:
            assert w_col0 % tn == 0
            in_specs.append(pl.BlockSpec((k, tn),
                                         lambda i, j, rb=rb: (rb, j + w_col0 // tn)))
        elif w_col0 % tn == 0:
            in_specs.append(pl.BlockSpec((tn, k), lambda i, j: (j + w_col0 // tn, 0)))
        else:
            g = math.gcd(w_col0, tn)
            assert g % SUBLANES == 0
            in_specs.append(pl.BlockSpec(
                (pl.Element(tn), pl.Element(k)),
                lambda i, j, g=g: ((w_col0 // g + j * (tn // g)) * g, 0)))
        args.append(w)
    if res is not None:
        in_specs.append(pl.BlockSpec((tm, tn), lambda i, j: (i, j)))
        args.append(res)
    tiles_per_seq = None if seq_len is None else seq_len // tm
    if epilogue in ("rope", "rope_qk"):
        extra = _rope_swap_tables(extra, HEAD_DIM)
    for e in extra:
        if e.shape == (HEAD_DIM, HEAD_DIM):
            in_specs.append(pl.BlockSpec((HEAD_DIM, HEAD_DIM), lambda i, j: (0, 0)))
        elif e.shape[0] == m:
            in_specs.append(pl.BlockSpec((tm, e.shape[1]), lambda i, j: (i, 0)))
        else:
            assert seq_len is not None and e.shape[0] == seq_len and seq_len % tm == 0
            in_specs.append(pl.BlockSpec((tm, e.shape[1]),
                                         lambda i, j: (i % tiles_per_seq, 0)))
        args.append(e)
    kern = functools.partial(_mm_kernel, n_a=n_a, has_res=res is not None,
                             epilogue=epilogue, scale=scale,
                             tiles_per_kind=DIL_WIDTH // tn,
                             qkv_tiles=(qkv_cols[0] // tn, qkv_cols[1] // tn),
                             w_transposed=w_transposed)
    return pl.pallas_call(
        kern,
        out_shape=jax.ShapeDtypeStruct((m, n), out_dtype),
        grid=(m // tm, n // tn),
        in_specs=in_specs,
        out_specs=pl.BlockSpec((tm, tn), lambda i, j: (i, j)),
        compiler_params=_params(2, vmem_mb),
        name=name,
    )(*args)


S5_UNROLL = 4


def _s5_discretise(lam_re, lam_im, log_step, b_re, b_im, c_re, c_im, seg_len):
    lr = jnp.minimum(lam_re.astype(F32), -1e-4)
    li = lam_im.astype(F32)
    dt = jnp.exp(log_step.astype(F32))[:, None]
    mag = jnp.exp(lr * dt)
    ar = mag * jnp.cos(li * dt)
    ai = mag * jnp.sin(li * dt)
    den = lr * lr + li * li
    nr, ni = ar - 1.0, ai
    cr = (nr * lr + ni * li) / den
    ci = (ni * lr - nr * li) / den
    br32, bi32 = b_re.astype(F32), b_im.astype(F32)
    bbr = cr[..., None] * br32 - ci[..., None] * bi32
    bbi = cr[..., None] * bi32 + ci[..., None] * br32

    eye = jnp.eye(S5_BLOCK_GROUPS, dtype=F32)

    def in_blockdiag(t):
        t = t.reshape(S5_BLOCKS, S5_BLOCK_GROUPS, S5_STATE, S5_GROUP)
        t = jnp.einsum("bgpj,gh->bgjhp", t, eye)
        return t.reshape(S5_BLOCKS, 128, S5_BLOCK_STATES)

    def out_blockdiag(t):
        t = t.reshape(S5_BLOCKS, S5_BLOCK_GROUPS, S5_GROUP, S5_STATE)
        t = jnp.einsum("bgjp,gh->bgphj", t, eye)
        return t.reshape(S5_BLOCKS, S5_BLOCK_STATES, 128)

    b_mat = jnp.concatenate([in_blockdiag(bbr), in_blockdiag(bbi)], axis=2).astype(BF16)
    c_mat = jnp.concatenate([out_blockdiag(c_re.astype(F32)),
                             -out_blockdiag(c_im.astype(F32))], axis=1).astype(BF16)

    a_r = ar.reshape(S5_BLOCKS, S5_BLOCK_STATES)
    a_i = ai.reshape(S5_BLOCKS, S5_BLOCK_STATES)
    s_r, s_i = jnp.ones_like(a_r), jnp.zeros_like(a_i)
    b_r, b_i, e = a_r, a_i, seg_len
    while e:
        if e & 1:
            s_r, s_i = s_r * b_r - s_i * b_i, s_r * b_i + s_i * b_r
        b_r, b_i = b_r * b_r - b_i * b_i, 2.0 * b_r * b_i
        e >>= 1
    pad = jnp.zeros((S5_BLOCKS, SUBLANES - 2, S5_BLOCK_STATES), F32)
    pow_re = jnp.concatenate([a_r[:, None], s_r[:, None], pad], axis=1)
    pow_im = jnp.concatenate([a_i[:, None], s_i[:, None], pad], axis=1)
    return b_mat, c_mat, pow_re, pow_im


def _s5_kernel(u_ref, b_ref, c_ref, pre_ref, pim_ref, d_ref, y_ref, up_ref, s_ref, yp_ref):
    L = u_ref.shape[1]
    ns = S5_BLOCK_STATES
    seg = L // SUBLANES

    def gather(r, _):
        up_ref[pl.ds(pl.multiple_of(r * SUBLANES, SUBLANES), SUBLANES), :] = (
            u_ref[0, pl.ds(r, SUBLANES, stride=seg), :])
        return 0

    lax.fori_loop(0, seg, gather, 0, unroll=S5_UNROLL)
    u = up_ref[...]
    s_ref[...] = jnp.dot(u.astype(BF16), b_ref[0], preferred_element_type=F32)

    pre, pim = pre_ref[0], pim_ref[0]
    ar = jnp.broadcast_to(pre[0:1, :], (SUBLANES, ns))
    ai = jnp.broadcast_to(pim[0:1, :], (SUBLANES, ns))

    def load(r, half):
        return s_ref[pl.ds(pl.multiple_of(r * SUBLANES, SUBLANES), SUBLANES),
                     half * ns:(half + 1) * ns]

    def store(r, half, x):
        s_ref[pl.ds(pl.multiple_of(r * SUBLANES, SUBLANES), SUBLANES),
              half * ns:(half + 1) * ns] = x

    def local(r, carry):
        sr, si = carry
        sr, si = ar * sr - ai * si + load(r, 0), ar * si + ai * sr + load(r, 1)
        store(r, 0, sr)
        store(r, 1, si)
        return sr, si

    zeros = jnp.zeros((SUBLANES, ns), F32)
    er, ei = lax.fori_loop(0, seg, local, (zeros, zeros), unroll=S5_UNROLL)

    gr, gi = pre[1:2, :], pim[1:2, :]
    cr, ci = [jnp.zeros((1, ns), F32)], [jnp.zeros((1, ns), F32)]
    for j in range(1, SUBLANES):
        pr, pi = cr[-1], ci[-1]
        cr.append(er[j - 1:j, :] + gr * pr - gi * pi)
        ci.append(ei[j - 1:j, :] + gr * pi + gi * pr)
    dr, di = jnp.concatenate(cr, axis=0), jnp.concatenate(ci, axis=0)

    def carried(r, carry):
        dr, di = carry
        dr, di = ar * dr - ai * di, ar * di + ai * dr
        store(r, 0, load(r, 0) + dr)
        store(r, 1, load(r, 1) + di)
        return dr, di

    lax.fori_loop(0, seg, carried, (dr, di), unroll=S5_UNROLL)

    y = jnp.dot(s_ref[...].astype(BF16), c_ref[0], preferred_element_type=F32)
    yp_ref[...] = y + d_ref[0] * u
    for j in range(SUBLANES):
        y_ref[0, j * seg:(j + 1) * seg, :] = yp_ref[pl.ds(j, seg, stride=SUBLANES), :]


def _s5(proj_a, b_mat, c_mat, pow_re, pow_im, d_skip):
    bsz, L, _ = proj_a.shape
    ns = S5_BLOCK_STATES
    return pl.pallas_call(
        _s5_kernel,
        out_shape=jax.ShapeDtypeStruct((bsz, L, S5_WIDTH), F32),
        grid=(bsz, S5_BLOCKS),
        in_specs=[
            pl.BlockSpec((1, L, 128), lambda b, g: (b, 0, g)),
            pl.BlockSpec((1, 128, 2 * ns), lambda b, g: (g, 0, 0)),
            pl.BlockSpec((1, 2 * ns, 128), lambda b, g: (g, 0, 0)),
            pl.BlockSpec((1, SUBLANES, ns), lambda b, g: (g, 0, 0)),
            pl.BlockSpec((1, SUBLANES, ns), lambda b, g: (g, 0, 0)),
            pl.BlockSpec((1, 1, 128), lambda b, g: (g, 0, 0)),
        ],
        out_specs=pl.BlockSpec((1, L, 128), lambda b, g: (b, 0, g)),
        scratch_shapes=[pltpu.VMEM((L, LANES), F32),
                        pltpu.VMEM((L, 2 * ns), F32),
                        pltpu.VMEM((L, LANES), F32)],
        compiler_params=_params(2, 40),
        name="s5_scan",
    )(proj_a, b_mat, c_mat, pow_re, pow_im, d_skip.reshape(S5_BLOCKS, 1, 128))


def _glu_kernel(y_ref, z_ref, w_ref, b_ref, o_ref):
    g = jax.nn.gelu(y_ref[...])
    t = jnp.dot(g.astype(BF16), w_ref[...].astype(BF16), preferred_element_type=F32) + b_ref[...]
    z = z_ref[...]
    gate = z / (1.0 + jnp.exp(-z))
    o_ref[...] = (g / (1.0 + jnp.exp(-t)) * gate).astype(o_ref.dtype)


def _glu(y, proj_a, glu_w, glu_b, tm=512):
    m = y.shape[0]
    return pl.pallas_call(
        _glu_kernel,
        out_shape=jax.ShapeDtypeStruct((m, S5_WIDTH), BF16),
        grid=(m // tm,),
        in_specs=[pl.BlockSpec((tm, S5_WIDTH), lambda i: (i, 0)),
                  pl.BlockSpec((tm, S5_WIDTH), lambda i: (i, 1)),
                  pl.BlockSpec((S5_WIDTH, S5_WIDTH), lambda i: (0, 0)),
                  pl.BlockSpec((1, S5_WIDTH), lambda i: (0, 0))],
        out_specs=pl.BlockSpec((tm, S5_WIDTH), lambda i: (i, 0)),
        compiler_params=_params(1, 40),
        name="s5_glu",
    )(y, proj_a, glu_w, glu_b.reshape(1, S5_WIDTH))


DIL_UNROLL = 8
MERGE_ROWS = 256


def _dil_kernel(*refs, L):
    qkv_refs = refs[:3 * N_DIL]
    z_ref, out_ref, og_ref, lg_ref, cls_ref = refs[3 * N_DIL:]
    n_tiles = L // Q_TILE

    for g, (window, dil) in enumerate(DIL_PATTERNS):
        n = L // dil
        nb = n // Q_TILE
        cw = Q_TILE if nb == 1 else 2 * Q_TILE
        half = (window // dil) // 2

        def rows(start, size, dil=dil):
            return pl.ds(start, size) if dil == 1 else pl.ds(start, size, stride=dil)

        for kind in range(3):
            for r in range(dil):
                cls_ref[kind, r * n:(r + 1) * n, :] = (
                    qkv_refs[3 * g + kind][0, rows(r, n), :].astype(BF16))

        def scores(it, n=n, nb=nb, cw=cw, half=half, dil=dil):
            r = it // nb
            qb = it % nb
            kb = jnp.clip(qb - 1, 0, max(nb - 2, 0))
            qc = pl.multiple_of(r * n + qb * Q_TILE, Q_TILE)
            kc = pl.multiple_of(r * n + kb * Q_TILE, Q_TILE)
            qpos = qb * Q_TILE + lax.broadcasted_iota(jnp.int32, (Q_TILE, cw), 0)
            kpos = kb * Q_TILE + lax.broadcasted_iota(jnp.int32, (Q_TILE, cw), 1)
            valid = jnp.abs(qpos - kpos - half) <= half
            s = lax.dot_general(cls_ref[0, pl.ds(qc, Q_TILE), :], cls_ref[1, pl.ds(kc, cw), :],
                                (((1,), (1,)), ((), ())),
                                preferred_element_type=F32) * (HEAD_DIM ** -0.5)
            return jnp.where(valid, s, MASK_NEG), kc, r + qb * (Q_TILE * dil)

        def softmax(s):
            m = jnp.max(s, axis=1, keepdims=True)
            p = jnp.exp(s - m)
            l = jnp.sum(p, axis=1, keepdims=True)
            return p.astype(BF16), l, m + jnp.log(l)

        def unrolled(j, _, g=g, cw=cw, rows=rows, scores=scores, softmax=softmax):
            blocks = [scores(j * DIL_UNROLL + u) for u in range(DIL_UNROLL)]
            probs = [softmax(s) for s, _, _ in blocks]
            for (_, kc, q0), (p, l, lse) in zip(blocks, probs):
                o = jnp.dot(p, cls_ref[2, pl.ds(kc, cw), :], preferred_element_type=F32) / l
                og_ref[g, rows(q0, Q_TILE), :] = o
                lg_ref[g, rows(q0, Q_TILE), :] = jnp.broadcast_to(lse, (Q_TILE, HEAD_DIM))
            return 0

        lax.fori_loop(0, n_tiles // DIL_UNROLL, unrolled, 0)

    def merge(j, _):
        rs = pl.ds(pl.multiple_of(j * MERGE_ROWS, MERGE_ROWS), MERGE_ROWS)
        a = [lg_ref[g, rs, :] for g in range(N_DIL)]
        m = functools.reduce(jnp.maximum, a)
        e = [jnp.exp(x - m) for x in a]
        o = sum(e[g] * og_ref[g, rs, :] for g in range(N_DIL)) / sum(e)
        z = z_ref[0, rs, :]
        out_ref[0, rs, :] = (o * (z / (1.0 + jnp.exp(-z)))).astype(out_ref.dtype)
        return 0

    lax.fori_loop(0, L // MERGE_ROWS, merge, 0)


def _dilated_attention(proj, qkv_col0, gate_col0):
    bsz, L, _ = proj.shape
    for window, dil in DIL_PATTERNS:
        assert (L // dil) % Q_TILE == 0 and (window // dil) % 2 == 0
    assert (L // Q_TILE) % DIL_UNROLL == 0 and L % MERGE_ROWS == 0
    qkv_blk0, gate_blk0 = qkv_col0 // HEAD_DIM, gate_col0 // HEAD_DIM

    def in_spec(g, kind):
        return pl.BlockSpec((1, L, HEAD_DIM),
                            lambda b, h: (b, 0, qkv_blk0 + (g * 3 + kind) * DIL_HEADS + h))

    gate_spec = pl.BlockSpec((1, L, HEAD_DIM), lambda b, h: (b, 0, gate_blk0 + h))
    head_spec = pl.BlockSpec((1, L, HEAD_DIM), lambda b, h: (b, 0, h))
    return pl.pallas_call(
        functools.partial(_dil_kernel, L=L),
        out_shape=jax.ShapeDtypeStruct((bsz, L, DIL_WIDTH), BF16),
        grid=(bsz, DIL_HEADS),
        in_specs=[in_spec(g, kind) for g in range(N_DIL) for kind in range(3)] + [gate_spec],
        out_specs=head_spec,
        scratch_shapes=[pltpu.VMEM((N_DIL, L, HEAD_DIM), F32),
                        pltpu.VMEM((N_DIL, L, HEAD_DIM), F32),
                        pltpu.VMEM((3, L, HEAD_DIM), BF16)],
        compiler_params=_params(2, 48),
        name="dilated_attention",
    )(*([proj] * (3 * N_DIL + 1)))


def _out_norm_kernel(a0_ref, a1_ref, w0_ref, w1_ref, res_ref, g_ref, x_ref, h_ref):
    x = (jnp.dot(a0_ref[...], w0_ref[...], preferred_element_type=F32)
         + jnp.dot(a1_ref[...], w1_ref[...], preferred_element_type=F32) + res_ref[...])
    x_ref[...] = x
    ms = jnp.mean(x * x, axis=-1, keepdims=True)
    h_ref[...] = (x * lax.rsqrt(ms + EPS) * g_ref[...]).astype(h_ref.dtype)


def _even_out_norm(a_out, b_out, w_out, res, next_norm, tm=256):
    m, d = res.shape
    ka, kb = a_out.shape[1], b_out.shape[1]
    assert ka == kb and m % tm == 0
    w = w_out.astype(BF16)
    row = lambda k: pl.BlockSpec((tm, k), lambda i: (i, 0))
    wspec = lambda rb: pl.BlockSpec((ka, d), lambda i, rb=rb: (rb, 0),
                                    pipeline_mode=pl.Buffered(1))
    return pl.pallas_call(
        _out_norm_kernel,
        out_shape=[jax.ShapeDtypeStruct((m, d), F32), jax.ShapeDtypeStruct((m, d), BF16)],
        grid=(m // tm,),
        in_specs=[row(ka), row(kb), wspec(0), wspec(1), row(d),
                  pl.BlockSpec((1, d), lambda i: (0, 0))],
        out_specs=[row(d), row(d)],
        compiler_params=_params(1, 52),
        name="even_out_norm",
    )(a_out, b_out, w, w, res, next_norm.reshape(1, d))


def _even_layer(x2d, bsz, L, norm, w_in, lam_re, lam_im, log_step, b_re, b_im, c_re, c_im,
                d_skip, glu_w, glu_b, w_out, tables, h=None, next_norm=None):
    if h is None:
        h = _rmsnorm(x2d, norm, BF16)
    qkv0 = 2 * S5_WIDTH
    proj = _matmul([h], [w_in], w_in.shape[1], F32, epilogue="rope_qk", extra=tables,
                   seq_len=L, qkv_cols=(qkv0, qkv0 + QKV_WIDTH), name="even_in")
    proj3 = proj.reshape(bsz, L, -1)

    b_mat, c_mat, pow_re, pow_im = _s5_discretise(lam_re, lam_im, log_step, b_re, b_im,
                                                  c_re, c_im, L // SUBLANES)
    y = _s5(proj3, b_mat, c_mat, pow_re, pow_im, d_skip)
    a_out = _glu(y.reshape(bsz * L, S5_WIDTH), proj, glu_w, glu_b)

    b_out = _dilated_attention(proj3, qkv0, qkv0 + QKV_WIDTH).reshape(bsz * L, DIL_WIDTH)

    if next_norm is not None:
        return _even_out_norm(a_out, b_out, w_out, x2d, next_norm)
    return _matmul([a_out, b_out], [w_out, w_out], D_MODEL, F32, w_row_blocks=[0, 1],
                   res=x2d, name="even_out"), None


def _odd_prep_kernel(p_ref, qn_ref, kvn_ref, kin_ref, c_ref, sa_ref, sb_ref,
                     cq_ref, ckv_ref, kr_ref, ki_ref, wi_ref):
    def norm(x, g):
        ms = jnp.mean(x * x, axis=-1, keepdims=True)
        return x * lax.rsqrt(ms + EPS) * g

    c, sa, sb = c_ref[...], sa_ref[...], sb_ref[...]
    o = 0
    cq_ref[...] = norm(p_ref[:, o:o + Q_RANK], qn_ref[...]).astype(cq_ref.dtype)
    o += Q_RANK
    ckv_ref[...] = norm(p_ref[:, o:o + KV_RANK], kvn_ref[...]).astype(ckv_ref.dtype)
    o += KV_RANK
    lo = p_ref[:, o:o + LANES]
    hi = p_ref[:, o + LANES:o + 2 * LANES]
    lane = lax.broadcasted_iota(jnp.int32, lo.shape, 1)
    kr_ref[...] = _rope_head(jnp.where(lane < ROT_DIM, lo, 0.0), c, sa, sb)
    k_i = jnp.where(lane < LANES - ROT_DIM, pltpu.roll(lo, LANES - ROT_DIM, 1),
                    pltpu.roll(hi, LANES - ROT_DIM, 1))
    ki_ref[...] = _rope_head(norm(k_i, kin_ref[...]), c, sa, sb).astype(ki_ref.dtype)
    wi_ref[...] = hi * (IDX_HEADS ** -0.5)


def _odd_prep(p_small, q_norm, kv_norm, k_idx_norm, tables, L, tm=256):
    m = p_small.shape[0]
    tps = L // tm
    row = lambda w: pl.BlockSpec((tm, w), lambda i: (i, 0))
    tab = pl.BlockSpec((tm, 128), lambda i: (i % tps, 0))
    vec = lambda w: pl.BlockSpec((1, w), lambda i: (0, 0))
    return pl.pallas_call(
        _odd_prep_kernel,
        out_shape=[jax.ShapeDtypeStruct((m, Q_RANK), BF16),
                   jax.ShapeDtypeStruct((m, KV_RANK), BF16),
                   jax.ShapeDtypeStruct((m, 128), F32),
                   jax.ShapeDtypeStruct((m, 128), BF16),
                   jax.ShapeDtypeStruct((m, 128), F32)],
        grid=(m // tm,),
        in_specs=[row(ODD_SMALL), vec(Q_RANK), vec(KV_RANK), vec(IDX_DIM), tab, tab, tab],
        out_specs=[row(Q_RANK), row(KV_RANK), row(128), row(128), row(128)],
        compiler_params=_params(1, 32),
        name="odd_prep",
    )(p_small, q_norm.reshape(1, -1), kv_norm.reshape(1, -1), k_idx_norm.reshape(1, -1),
      *tables)


def _float_sort_key(s):
    b = pltpu.bitcast(s, jnp.int32)
    return b ^ ((b >> 31) & 0x7FFFFFFF)


def _indexer_kernel(q_ref, k_ref, w_ref, bias_ref, key_ref, jmax_ref, *, L, topk, kc, w_lane0):
    i = pl.program_id(1)
    nch = L // kc
    rows = IDX_ROWS
    n_vis = ((i + 1) * rows + kc - 1) // kc
    qpos = i * rows + lax.broadcasted_iota(jnp.int32, (rows, 1), 0)
    lane = lax.broadcasted_iota(jnp.int32, (rows, kc), 1)
    lane_t = lax.broadcasted_iota(jnp.int32, (Q_TILE, kc), 1)
    qpos_t = lax.broadcasted_iota(jnp.int32, (Q_TILE, 1), 0)

    def chunk(c, _):
        k0 = pl.multiple_of(c * kc, kc)
        kblk = k_ref[0, pl.ds(k0, kc), :]
        for t in range(rows // Q_TILE):
            rs = slice(t * Q_TILE, (t + 1) * Q_TILE)
            acc = jnp.zeros((Q_TILE, kc), F32)
            for h in range(IDX_HEADS):
                qh = q_ref[0, rs, h * IDX_DIM:(h + 1) * IDX_DIM]
                s = lax.dot_general(qh, kblk, (((1,), (1,)), ((), ())),
                                    preferred_element_type=F32)
                acc = acc + w_ref[0, rs, w_lane0 + h:w_lane0 + h + 1] * jnp.maximum(s, 0.0)
            acc = jnp.where(acc == 0.0, 0.0, acc)
            visible = lane_t + k0 <= qpos_t + (i * rows + t * Q_TILE)
            key_ref[c, rs, :] = jnp.where(visible, _float_sort_key(acc), INT_MIN)
        return 0

    lax.fori_loop(0, n_vis, chunk, 0)

    def select(nv):
        def count(pred):
            tot = jnp.zeros((rows, LANES), F32)
            for c in range(nv):
                hit = jnp.where(pred(key_ref[c], c), 1.0, 0.0)
                for j in range(kc // LANES):
                    tot = tot + hit[:, j * LANES:(j + 1) * LANES]
            return jnp.sum(tot, axis=1, keepdims=True)

        def tbit(b, t_u):
            cand_u = t_u | jnp.left_shift(jnp.int32(1), 31 - b)
            cand_s = cand_u ^ INT_MIN
            cnt = count(lambda key, c: key >= cand_s)
            return jnp.where(cnt >= topk, cand_u, t_u)

        t_u = lax.fori_loop(0, 32, tbit, jnp.zeros((rows, 1), jnp.int32))
        thr = t_u ^ INT_MIN

        need = topk - count(lambda key, c: key > thr)
        n_eq = count(lambda key, c: key == thr)
        jbits = max(1, (L - 1).bit_length())
        jmax_ref[...] = jnp.full((rows, 1), (1 << jbits) - 1, jnp.int32)

        unsettled = jnp.where(thr == INT_MIN, 0.0, jnp.abs(n_eq - need))

        @pl.when(jnp.max(unsettled) > 0.0)
        def _():
            def jbit(b, j):
                cand = j | jnp.left_shift(jnp.int32(1), jbits - 1 - b)
                cnt = count(lambda key, c: jnp.where(lane + c * kc < cand, key, thr - 1) == thr)
                return jnp.where(cnt < need, cand, j)

            jmax_ref[...] = lax.fori_loop(0, jbits, jbit, jnp.zeros((rows, 1), jnp.int32))

        jlim = jnp.minimum(jmax_ref[...], qpos)
        for c in range(nv):
            key = key_ref[c]
            tie = jnp.where(lane + c * kc <= jlim, 0.0, MASK_NEG)
            bias = jnp.where(key > thr, 0.0, jnp.where(key == thr, tie, MASK_NEG))
            for t in range(rows // Q_TILE):
                bias_ref[0, t, c] = bias[t * Q_TILE:(t + 1) * Q_TILE].astype(bias_ref.dtype)
        for c in range(nv, nch):
            for t in range(rows // Q_TILE):
                bias_ref[0, t, c] = jnp.full((Q_TILE, kc), MASK_NEG, bias_ref.dtype)

    for nv in range(1, nch + 1):
        pl.when(n_vis == nv)(functools.partial(select, nv))


def _indexer(q_idx, k_idx, w_idx, topk, kc, w_lane0):
    bsz, L, _ = q_idx.shape
    nq, nch = L // Q_TILE, L // kc
    rows = IDX_ROWS
    assert L % rows == 0 and rows % Q_TILE == 0
    return pl.pallas_call(
        functools.partial(_indexer_kernel, L=L, topk=topk, kc=kc, w_lane0=w_lane0),
        out_shape=jax.ShapeDtypeStruct((bsz, nq, nch, Q_TILE, kc), BF16),
        grid=(bsz, L // rows),
        in_specs=[pl.BlockSpec((1, rows, IDX_HEADS * IDX_DIM), lambda b, i: (b, i, 0)),
                  pl.BlockSpec((1, L, IDX_DIM), lambda b, i: (b, 0, 0)),
                  pl.BlockSpec((1, rows, LANES), lambda b, i: (b, i, 0))],
        out_specs=pl.BlockSpec((1, rows // Q_TILE, nch, Q_TILE, kc),
                               lambda b, i: (b, i, 0, 0, 0)),
        scratch_shapes=[pltpu.VMEM((nch, rows, kc), jnp.int32),
                        pltpu.VMEM((rows, 1), jnp.int32)],
        compiler_params=_params(2, 32),
        name="dsa_indexer",
    )(q_idx, k_idx, w_idx)


def _mla_kernel(q_ref, ckv_ref, kr_ref, wk_ref, wv_ref, bias_ref, g_ref, o_ref,
                k_ref, v_ref, s_ref, mx_ref, acc_ref, *, kc, nh):
    i = pl.program_id(2)
    rows = ATT_ROWS
    n_vis = ((i + 1) * rows + kc - 1) // kc
    slabs = kc // LANES

    @pl.when(i == 0)
    def _():
        ckv = ckv_ref[0]
        kr = kr_ref[0]
        k_all = jnp.dot(ckv, wk_ref[...].astype(BF16), preferred_element_type=F32)
        for h in range(nh):
            hs = slice(h * HEAD_DIM, (h + 1) * HEAD_DIM)
            k_ref[:, hs] = (k_all[:, hs] + kr).astype(k_ref.dtype)
        v_ref[...] = jnp.dot(ckv, wv_ref[...].astype(BF16),
                             preferred_element_type=F32).astype(v_ref.dtype)

    def fold(x, op):
        t = x[:, 0:LANES]
        for j in range(1, slabs):
            t = op(t, x[:, j * LANES:(j + 1) * LANES])
        return t

    mx_ref[...] = jnp.full(mx_ref.shape, MASK_NEG, F32)

    def scores(c, _):
        k0 = pl.multiple_of(c * kc, kc)
        bias = jnp.concatenate([bias_ref[0, t, c] for t in range(rows // Q_TILE)],
                               axis=0).astype(F32)
        raw = [lax.dot_general(q_ref[0, :, h * HEAD_DIM:(h + 1) * HEAD_DIM],
                               k_ref[pl.ds(k0, kc), h * HEAD_DIM:(h + 1) * HEAD_DIM],
                               (((1,), (1,)), ((), ())), preferred_element_type=F32)
               for h in range(nh)]
        for h in range(nh):
            s = raw[h] + bias
            s_ref[c, h] = s
            mx_ref[h] = jnp.maximum(mx_ref[h], fold(s, jnp.maximum))
        return 0

    lax.fori_loop(0, n_vis, scores, 0)

    for h in range(nh):
        m = jnp.max(mx_ref[h], axis=1, keepdims=True)
        mx_ref[h] = jnp.broadcast_to(m, (rows, LANES))
    acc_ref[...] = jnp.zeros(acc_ref.shape, F32)

    ones = jnp.ones((kc, LANES), BF16)

    def weights(c, _):
        k0 = pl.multiple_of(c * kc, kc)
        for h in range(nh):
            hs = slice(h * HEAD_DIM, (h + 1) * HEAD_DIM)
            m = mx_ref[h]
            s = s_ref[c, h]
            p = jnp.concatenate(
                [jnp.exp2(s[:, j * LANES:(j + 1) * LANES] - m) for j in range(slabs)], axis=1)
            v1 = jnp.concatenate([v_ref[pl.ds(k0, kc), hs], ones], axis=1)
            acc_ref[h] += jnp.dot(p.astype(BF16), v1, preferred_element_type=F32)
        return 0

    lax.fori_loop(0, n_vis, weights, 0)

    for h in range(nh):
        hs = slice(h * HEAD_DIM, (h + 1) * HEAD_DIM)
        g = g_ref[0, :, hs].astype(F32)
        o = acc_ref[h, :, 0:HEAD_DIM] / acc_ref[h, :, HEAD_DIM:HEAD_DIM + LANES]
        o_ref[0, :, hs] = (o * (g / (1.0 + jnp.exp(-g)))).astype(o_ref.dtype)


def _masked_attention(q, c_kv, k_rope, w_k, w_v, bias, gate, kc, nh=ATT_HEADS):
    bsz, L, _ = q.shape
    rows, nch = ATT_ROWS, L // kc
    assert L % rows == 0 and rows % Q_TILE == 0
    w = nh * HEAD_DIM
    qspec = pl.BlockSpec((1, rows, w), lambda b, g, i: (b, i, g))
    wspec = pl.BlockSpec((KV_RANK, w), lambda b, g, i: (0, g))
    return pl.pallas_call(
        functools.partial(_mla_kernel, kc=kc, nh=nh),
        out_shape=jax.ShapeDtypeStruct((bsz, L, C_WIDTH), BF16),
        grid=(bsz, MLA_HEADS // nh, L // rows),
        in_specs=[qspec,
                  pl.BlockSpec((1, L, KV_RANK), lambda b, g, i: (b, 0, 0)),
                  pl.BlockSpec((1, L, HEAD_DIM), lambda b, g, i: (b, 0, 0)),
                  wspec, wspec,
                  pl.BlockSpec((1, rows // Q_TILE, nch, Q_TILE, kc),
                               lambda b, g, i: (b, i, 0, 0, 0)),
                  qspec],
        out_specs=qspec,
        scratch_shapes=[pltpu.VMEM((L, w), BF16),
                        pltpu.VMEM((L, w), BF16),
                        pltpu.VMEM((nch, nh, rows, kc), F32),
                        pltpu.VMEM((nh, rows, LANES), F32),
                        pltpu.VMEM((nh, rows, HEAD_DIM + LANES), F32)],
        compiler_params=_params(3, 48),
        name="dsa_attention",
    )(q, c_kv, k_rope, w_k, w_v, bias, gate)


def _odd_layer(x2d, bsz, L, norm, w_in, q_norm, kv_norm, k_idx_norm, w_uq, w_uk, w_uv, w_iq,
               w_out, tables, h=None):
    m = bsz * L
    if h is None:
        h = _rmsnorm(x2d, norm, BF16)

    w_in_t = w_in.T
    p_small = _matmul([h], [w_in_t], ODD_SMALL, F32, w_transposed=True, tn=256,
                      name="odd_in_small")
    gate = _matmul([h], [w_in_t], C_WIDTH, BF16, w_transposed=True, w_col0=ODD_GATE0,
                   name="odd_in_gate")
    c_q, c_kv, k_rope, k_idx, w_idx = _odd_prep(p_small, q_norm, kv_norm, k_idx_norm, tables, L)

    q = _matmul([c_q], [w_uq], C_WIDTH, BF16, epilogue="rope", extra=tables, seq_len=L,
                scale=LOG2E * HEAD_DIM ** -0.5, name="odd_q_up")
    q_idx = _matmul([c_q], [w_iq], IDX_HEADS * IDX_DIM, BF16, epilogue="rope", extra=tables,
                    seq_len=L, scale=IDX_DIM ** -0.5, name="odd_q_idx")

    w_k = jnp.pad(w_uk, ((0, 0), (0, 0), (ROT_DIM, 0))).reshape(KV_RANK, C_WIDTH)
    w_v = w_uv.reshape(KV_RANK, C_WIDTH)

    topk = min(IDX_TOPK_MAX, L // 4)
    kc = min(512, L)
    bias = _indexer(q_idx.reshape(bsz, L, -1), k_idx.reshape(bsz, L, IDX_DIM),
                    w_idx.reshape(bsz, L, LANES), topk, kc, IDX_W_LANE0)
    o = _masked_attention(q.reshape(bsz, L, C_WIDTH), c_kv.reshape(bsz, L, KV_RANK),
                          k_rope.reshape(bsz, L, HEAD_DIM), w_k, w_v, bias,
                          gate.reshape(bsz, L, C_WIDTH), kc)
    return _matmul([o.reshape(m, C_WIDTH)], [w_out], D_MODEL, F32, res=x2d, name="odd_out")


def kernel(x, even_norm, even_w_in, s5_lambda_re, s5_lambda_im, s5_log_step, s5_b_re, s5_b_im,
           s5_c_re, s5_c_im, s5_d, s5_glu_w, s5_glu_b, even_w_out, odd_norm, odd_w_in,
           mla_q_norm, mla_kv_norm, idx_k_norm, mla_w_uq, mla_w_uk, mla_w_uv, idx_w_q,
           odd_w_out, final_norm):
    bsz, L, d = x.shape
    depth = even_norm.shape[0] + odd_norm.shape[0]
    tables = _rope_tables(L)
    x2d = x.reshape(bsz * L, d)
    h = None
    for layer in range(depth):
        i = layer // 2
        if layer % 2 == 0:
            next_norm = odd_norm[i] if layer + 1 < depth else None
            x2d, h = _even_layer(x2d, bsz, L, even_norm[i], even_w_in[i], s5_lambda_re[i],
                                 s5_lambda_im[i], s5_log_step[i], s5_b_re[i], s5_b_im[i],
                                 s5_c_re[i], s5_c_im[i], s5_d[i], s5_glu_w[i], s5_glu_b[i],
                                 even_w_out[i], tables, h=h, next_norm=next_norm)
        else:
            x2d = _odd_layer(x2d, bsz, L, odd_norm[i], odd_w_in[i], mla_q_norm[i],
                             mla_kv_norm[i], idx_k_norm[i], mla_w_uq[i], mla_w_uk[i],
                             mla_w_uv[i], idx_w_q[i], odd_w_out[i], tables, h=h)
            h = None
    return _rmsnorm(x2d, final_norm, F32).reshape(bsz, L, d)
```

```python
import functools
import math

import jax
import jax.numpy as jnp
from jax import lax
from jax.experimental import pallas as pl
from jax.experimental.pallas import tpu as pltpu

F32 = jnp.float32
BF16 = jnp.bfloat16

LANES = 128
SUBLANES = 8

D_MODEL = 4096
EPS = 1e-6
ROPE_THETA = 500000.0
HEAD_DIM = 128
ROT_DIM = HEAD_DIM // 4
ROT_HALF = ROT_DIM // 2

S5_WIDTH = D_MODEL // 4
S5_GROUP = 16
S5_GROUPS = S5_WIDTH // S5_GROUP
S5_STATE = 64
S5_BLOCKS = S5_WIDTH // 128
S5_BLOCK_GROUPS = S5_GROUPS // S5_BLOCKS
S5_BLOCK_STATES = S5_BLOCK_GROUPS * S5_STATE

DIL_PATTERNS = ((128, 1), (512, 4), (2048, 16))
N_DIL = len(DIL_PATTERNS)
DIL_HEADS = D_MODEL // 512
DIL_WIDTH = DIL_HEADS * HEAD_DIM
QKV_WIDTH = 3 * N_DIL * DIL_WIDTH

MLA_HEADS = D_MODEL // HEAD_DIM
MLA_NOPE = HEAD_DIM - ROT_DIM
Q_RANK = 1536
KV_RANK = 512
IDX_HEADS = D_MODEL // 128
IDX_DIM = 128
IDX_TOPK_MAX = 256
C_WIDTH = MLA_HEADS * HEAD_DIM
ODD_GATE0 = Q_RANK + KV_RANK + ROT_DIM + IDX_DIM + IDX_HEADS
ODD_SMALL = Q_RANK + KV_RANK + 2 * LANES
IDX_W_LANE0 = ROT_DIM + IDX_DIM - LANES

Q_TILE = 128
ROPE_SUBTILE = 512
ATT_ROWS = 256
ATT_HEADS = 4
IDX_ROWS = 256
MASK_NEG = -1e30
LOG2E = 1.4426950408889634
INT_MIN = -(2 ** 31)


def _params(n_grid, vmem_mb):
    return pltpu.CompilerParams(
        dimension_semantics=("arbitrary",) * n_grid,
        vmem_limit_bytes=vmem_mb * 1024 * 1024)


def _rope_tables(L):
    inv = ROPE_THETA ** (-jnp.arange(0, ROT_DIM, 2, dtype=F32) / ROT_DIM)
    ang = jnp.arange(L, dtype=F32)[:, None] * inv[None, :]
    cos, sin = jnp.cos(ang), jnp.sin(ang)
    ones = jnp.ones((L, HEAD_DIM - ROT_DIM), F32)
    zeros = jnp.zeros((L, HEAD_DIM - ROT_DIM), F32)
    zh = jnp.zeros((L, ROT_HALF), F32)
    c = jnp.concatenate([cos, cos, ones], axis=1)
    sa = jnp.concatenate([-sin, zh, zeros], axis=1)
    sb = jnp.concatenate([zh, sin, zeros], axis=1)
    return c, sa, sb


def _rope_swap_tables(tables, width):
    c, sa, sb = tables
    col = jnp.arange(width)
    lane, head = col % HEAD_DIM, col // HEAD_DIM
    src = jnp.where(lane < ROT_HALF, col + ROT_HALF, col - ROT_HALF)
    sign = jnp.where(lane < ROT_HALF, -1.0, 1.0)
    perm = jnp.where((col[:, None] == src[None, :]) & (lane[None, :] < ROT_DIM)
                     & (head[:, None] == head[None, :]), sign[None, :], 0.0).astype(BF16)
    return c, sb - sa, perm


def _rope_head(x, c, sa, sb):
    return (x * c + pltpu.roll(x, HEAD_DIM - ROT_HALF, 1) * sa
            + pltpu.roll(x, ROT_HALF, 1) * sb)


def _rmsnorm_kernel(x_ref, g_ref, o_ref):
    x = x_ref[...]
    ms = jnp.mean(x * x, axis=-1, keepdims=True)
    o_ref[...] = (x * lax.rsqrt(ms + EPS) * g_ref[...]).astype(o_ref.dtype)


def _rmsnorm(x, g, out_dtype, tm=256):
    m, d = x.shape
    return pl.pallas_call(
        _rmsnorm_kernel,
        out_shape=jax.ShapeDtypeStruct((m, d), out_dtype),
        grid=(m // tm,),
        in_specs=[pl.BlockSpec((tm, d), lambda i: (i, 0)),
                  pl.BlockSpec((1, d), lambda i: (0, 0))],
        out_specs=pl.BlockSpec((tm, d), lambda i: (i, 0)),
        compiler_params=_params(1, 32),
        name="rmsnorm",
    )(x, g.reshape(1, d))


def _mm_kernel(*refs, n_a, has_res, epilogue, scale, tiles_per_kind, qkv_tiles, w_transposed):
    a_refs = refs[:n_a]
    w_refs = refs[n_a:2 * n_a]
    pos = 2 * n_a
    res_ref = None
    if has_res:
        res_ref = refs[pos]
        pos += 1
    extra = refs[pos:-1]
    o_ref = refs[-1]

    if epilogue == "rope":
        (a_ref,), (w_ref,) = a_refs, w_refs
        c_ref, s_ref, perm_ref = extra
        a = a_ref[...]
        for c0 in range(0, o_ref.shape[1], ROPE_SUBTILE):
            sub = jnp.dot(a, w_ref[:, c0:c0 + ROPE_SUBTILE].astype(BF16),
                          preferred_element_type=F32)
            for h0 in range(0, ROPE_SUBTILE, HEAD_DIM):
                x = sub[:, h0:h0 + HEAD_DIM]
                swapped = jnp.dot(x.astype(BF16), perm_ref[...], preferred_element_type=F32)
                y = (x * c_ref[...] + swapped * s_ref[...]) * scale
                o_ref[:, c0 + h0:c0 + h0 + HEAD_DIM] = y.astype(o_ref.dtype)
        return

    acc = None
    for a_ref, w_ref in zip(a_refs, w_refs):
        contract = (((1,), (1 if w_transposed else 0,)), ((), ()))
        d = lax.dot_general(a_ref[...], w_ref[...].astype(BF16), contract,
                            preferred_element_type=F32)
        acc = d if acc is None else acc + d
    if has_res:
        acc = acc + res_ref[...]
    n_heads = acc.shape[1] // HEAD_DIM

    def store_roped():
        c_ref, s_ref, perm_ref = extra
        for h in range(n_heads):
            hs = slice(h * HEAD_DIM, (h + 1) * HEAD_DIM)
            swapped = jnp.dot(acc[:, hs].astype(BF16), perm_ref[...],
                              preferred_element_type=F32)
            y = acc[:, hs] * c_ref[...] + swapped * s_ref[...]
            o_ref[:, hs] = (y * scale if scale != 1.0 else y).astype(o_ref.dtype)

    if epilogue == "none":
        o_ref[...] = acc.astype(o_ref.dtype)
    elif epilogue == "rope_qk":
        first, last = qkv_tiles
        j = pl.program_id(1)
        rotary = (j >= first) & (j < last) & (((j - first) // tiles_per_kind) % 3 != 2)

        @pl.when(jnp.logical_not(rotary))
        def _():
            o_ref[...] = acc.astype(o_ref.dtype)

        pl.when(rotary)(store_roped)
    else:
        raise ValueError(epilogue)


def _matmul(a_list, w_list, n, out_dtype, *, w_col0=0, w_row_blocks=None, res=None,
            epilogue="none", extra=(), scale=1.0, seq_len=None, w_transposed=False,
            qkv_cols=(0, 0), tn=512, vmem_mb=52, name="matmul"):
    m = a_list[0].shape[0]
    deep = sum(a.shape[1] for a in a_list) >= D_MODEL
    tm = 1024 if deep else 2048
    tm = min(tm, m)
    assert m % tm == 0 and n % tn == 0
    n_a = len(a_list)
    if w_row_blocks is None:
        w_row_blocks = [0] * n_a
    in_specs, args = [], []
    for a in a_list:
        k = a.shape[1]
        mode = {} if deep else {"pipeline_mode": pl.Buffered(1)}
        in_specs.append(pl.BlockSpec((tm, k), lambda i, j: (i, 0), **mode))
        args.append(a)
    for a, w, rb in zip(a_list, w_list, w_row_blocks):
        k = a.shape[1]
        if not w_transposed:
            assert w_col0 % tn == 0
            in_specs.append(pl.BlockSpec((k, tn),
                                         lambda i, j, rb=rb: (rb, j + w_col0 // tn)))
        elif w_col0 % tn == 0:
            in_specs.append(pl.BlockSpec((tn, k), lambda i, j: (j + w_col0 // tn, 0)))
        else:
            g = math.gcd(w_col0, tn)
            assert g % SUBLANES == 0
            in_specs.append(pl.BlockSpec(
                (pl.Element(tn), pl.Element(k)),
                lambda i, j, g=g: ((w_col0 // g + j * (tn // g)) * g, 0)))
        args.append(w)
    if res is not None:
        in_specs.append(pl.BlockSpec((tm, tn), lambda i, j: (i, j)))
        args.append(res)
    tiles_per_seq = None if seq_len is None else seq_len // tm
    if epilogue in ("rope", "rope_qk"):
        extra = _rope_swap_tables(extra, HEAD_DIM)
    for e in extra:
        if e.shape == (HEAD_DIM, HEAD_DIM):
            in_specs.append(pl.BlockSpec((HEAD_DIM, HEAD_DIM), lambda i, j: (0, 0)))
        elif e.shape[0] == m:
            in_specs.append(pl.BlockSpec((tm, e.shape[1]), lambda i, j: (i, 0)))
        else:
            assert seq_len is not None and e.shape[0] == seq_len and seq_len % tm == 0
            in_specs.append(pl.BlockSpec((tm, e.shape[1]),
                                         lambda i, j: (i % tiles_per_seq, 0)))
        args.append(e)
    kern = functools.partial(_mm_kernel, n_a=n_a, has_res=res is not None,
                             epilogue=epilogue, scale=scale,
                             tiles_per_kind=DIL_WIDTH // tn,
                             qkv_tiles=(qkv_cols[0] // tn, qkv_cols[1] // tn),
                             w_transposed=w_transposed)
    return pl.pallas_call(
        kern,
        out_shape=jax.ShapeDtypeStruct((m, n), out_dtype),
        grid=(m // tm, n // tn),
        in_specs=in_specs,
        out_specs=pl.BlockSpec((tm, tn), lambda i, j: (i, j)),
        compiler_params=_params(2, vmem_mb),
        name=name,
    )(*args)


S5_UNROLL = 4


def _s5_discretise(lam_re, lam_im, log_step, b_re, b_im, c_re, c_im, seg_len):
    lr = jnp.minimum(lam_re.astype(F32), -1e-4)
    li = lam_im.astype(F32)
    dt = jnp.exp(log_step.astype(F32))[:, None]
    mag = jnp.exp(lr * dt)
    ar = mag * jnp.cos(li * dt)
    ai = mag * jnp.sin(li * dt)
    den = lr * lr + li * li
    nr, ni = ar - 1.0, ai
    cr = (nr * lr + ni * li) / den
    ci = (ni * lr - nr * li) / den
    br32, bi32 = b_re.astype(F32), b_im.astype(F32)
    bbr = cr[..., None] * br32 - ci[..., None] * bi32
    bbi = cr[..., None] * bi32 + ci[..., None] * br32

    eye = jnp.eye(S5_BLOCK_GROUPS, dtype=F32)

    def in_blockdiag(t):
        t = t.reshape(S5_BLOCKS, S5_BLOCK_GROUPS, S5_STATE, S5_GROUP)
        t = jnp.einsum("bgpj,gh->bgjhp", t, eye)
        return t.reshape(S5_BLOCKS, 128, S5_BLOCK_STATES)

    def out_blockdiag(t):
        t = t.reshape(S5_BLOCKS, S5_BLOCK_GROUPS, S5_GROUP, S5_STATE)
        t = jnp.einsum("bgjp,gh->bgphj", t, eye)
        return t.reshape(S5_BLOCKS, S5_BLOCK_STATES, 128)

    b_mat = jnp.concatenate([in_blockdiag(bbr), in_blockdiag(bbi)], axis=2).astype(BF16)
    c_mat = jnp.concatenate([out_blockdiag(c_re.astype(F32)),
                             -out_blockdiag(c_im.astype(F32))], axis=1).astype(BF16)

    a_r = ar.reshape(S5_BLOCKS, S5_BLOCK_STATES)
    a_i = ai.reshape(S5_BLOCKS, S5_BLOCK_STATES)
    s_r, s_i = jnp.ones_like(a_r), jnp.zeros_like(a_i)
    b_r, b_i, e = a_r, a_i, seg_len
    while e:
        if e & 1:
            s_r, s_i = s_r * b_r - s_i * b_i, s_r * b_i + s_i * b_r
        b_r, b_i = b_r * b_r - b_i * b_i, 2.0 * b_r * b_i
        e >>= 1
    pad = jnp.zeros((S5_BLOCKS, SUBLANES - 2, S5_BLOCK_STATES), F32)
    pow_re = jnp.concatenate([a_r[:, None], s_r[:, None], pad], axis=1)
    pow_im = jnp.concatenate([a_i[:, None], s_i[:, None], pad], axis=1)
    return b_mat, c_mat, pow_re, pow_im


def _s5_kernel(u_ref, b_ref, c_ref, pre_ref, pim_ref, d_ref, y_ref, up_ref, s_ref, yp_ref):
    L = u_ref.shape[1]
    ns = S5_BLOCK_STATES
    seg = L // SUBLANES

    def gather(r, _):
        up_ref[pl.ds(pl.multiple_of(r * SUBLANES, SUBLANES), SUBLANES), :] = (
            u_ref[0, pl.ds(r, SUBLANES, stride=seg), :])
        return 0

    lax.fori_loop(0, seg, gather, 0, unroll=S5_UNROLL)
    u = up_ref[...]
    s_ref[...] = jnp.dot(u.astype(BF16), b_ref[0], preferred_element_type=F32)

    pre, pim = pre_ref[0], pim_ref[0]
    ar = jnp.broadcast_to(pre[0:1, :], (SUBLANES, ns))
    ai = jnp.broadcast_to(pim[0:1, :], (SUBLANES, ns))

    def load(r, half):
        return s_ref[pl.ds(pl.multiple_of(r * SUBLANES, SUBLANES), SUBLANES),
                     half * ns:(half + 1) * ns]

    def store(r, half, x):
        s_ref[pl.ds(pl.multiple_of(r * SUBLANES, SUBLANES), SUBLANES),
              half * ns:(half + 1) * ns] = x

    def local(r, carry):
        sr, si = carry
        sr, si = ar * sr - ai * si + load(r, 0), ar * si + ai * sr + load(r, 1)
        store(r, 0, sr)
        store(r, 1, si)
        return sr, si

    zeros = jnp.zeros((SUBLANES, ns), F32)
    er, ei = lax.fori_loop(0, seg, local, (zeros, zeros), unroll=S5_UNROLL)

    gr, gi = pre[1:2, :], pim[1:2, :]
    cr, ci = [jnp.zeros((1, ns), F32)], [jnp.zeros((1, ns), F32)]
    for j in range(1, SUBLANES):
        pr, pi = cr[-1], ci[-1]
        cr.append(er[j - 1:j, :] + gr * pr - gi * pi)
        ci.append(ei[j - 1:j, :] + gr * pi + gi * pr)
    dr, di = jnp.concatenate(cr, axis=0), jnp.concatenate(ci, axis=0)

    def carried(r, carry):
        dr, di = carry
        dr, di = ar * dr - ai * di, ar * di + ai * dr
        store(r, 0, load(r, 0) + dr)
        store(r, 1, load(r, 1) + di)
        return dr, di

    lax.fori_loop(0, seg, carried, (dr, di), unroll=S5_UNROLL)

    y = jnp.dot(s_ref[...].astype(BF16), c_ref[0], preferred_element_type=F32)
    yp_ref[...] = y + d_ref[0] * u
    for j in range(SUBLANES):
        y_ref[0, j * seg:(j + 1) * seg, :] = yp_ref[pl.ds(j, seg, stride=SUBLANES), :]


def _s5(proj_a, b_mat, c_mat, pow_re, pow_im, d_skip):
    bsz, L, _ = proj_a.shape
    ns = S5_BLOCK_STATES
    return pl.pallas_call(
        _s5_kernel,
        out_shape=jax.ShapeDtypeStruct((bsz, L, S5_WIDTH), F32),
        grid=(bsz, S5_BLOCKS),
        in_specs=[
            pl.BlockSpec((1, L, 128), lambda b, g: (b, 0, g)),
            pl.BlockSpec((1, 128, 2 * ns), lambda b, g: (g, 0, 0)),
            pl.BlockSpec((1, 2 * ns, 128), lambda b, g: (g, 0, 0)),
            pl.BlockSpec((1, SUBLANES, ns), lambda b, g: (g, 0, 0)),
            pl.BlockSpec((1, SUBLANES, ns), lambda b, g: (g, 0, 0)),
            pl.BlockSpec((1, 1, 128), lambda b, g: (g, 0, 0)),
        ],
        out_specs=pl.BlockSpec((1, L, 128), lambda b, g: (b, 0, g)),
        scratch_shapes=[pltpu.VMEM((L, LANES), F32),
                        pltpu.VMEM((L, 2 * ns), F32),
                        pltpu.VMEM((L, LANES), F32)],
        compiler_params=_params(2, 40),
        name="s5_scan",
    )(proj_a, b_mat, c_mat, pow_re, pow_im, d_skip.reshape(S5_BLOCKS, 1, 128))


def _glu_kernel(y_ref, z_ref, w_ref, b_ref, o_ref):
    g = jax.nn.gelu(y_ref[...])
    t = jnp.dot(g.astype(BF16), w_ref[...].astype(BF16), preferred_element_type=F32) + b_ref[...]
    z = z_ref[...]
    gate = z / (1.0 + jnp.exp(-z))
    o_ref[...] = (g / (1.0 + jnp.exp(-t)) * gate).astype(o_ref.dtype)


def _glu(y, proj_a, glu_w, glu_b, tm=512):
    m = y.shape[0]
    return pl.pallas_call(
        _glu_kernel,
        out_shape=jax.ShapeDtypeStruct((m, S5_WIDTH), BF16),
        grid=(m // tm,),
        in_specs=[pl.BlockSpec((tm, S5_WIDTH), lambda i: (i, 0)),
                  pl.BlockSpec((tm, S5_WIDTH), lambda i: (i, 1)),
                  pl.BlockSpec((S5_WIDTH, S5_WIDTH), lambda i: (0, 0)),
                  pl.BlockSpec((1, S5_WIDTH), lambda i: (0, 0))],
        out_specs=pl.BlockSpec((tm, S5_WIDTH), lambda i: (i, 0)),
        compiler_params=_params(1, 40),
        name="s5_glu",
    )(y, proj_a, glu_w, glu_b.reshape(1, S5_WIDTH))


DIL_UNROLL = 8
MERGE_ROWS = 256


def _dil_kernel(*refs, L):
    qkv_refs = refs[:3 * N_DIL]
    z_ref, out_ref, og_ref, lg_ref, cls_ref = refs[3 * N_DIL:]
    n_tiles = L // Q_TILE

    for g, (window, dil) in enumerate(DIL_PATTERNS):
        n = L // dil
        nb = n // Q_TILE
        cw = Q_TILE if nb == 1 else 2 * Q_TILE
        half = (window // dil) // 2

        def rows(start, size, dil=dil):
            return pl.ds(start, size) if dil == 1 else pl.ds(start, size, stride=dil)

        for kind in range(3):
            for r in range(dil):
                cls_ref[kind, r * n:(r + 1) * n, :] = (
                    qkv_refs[3 * g + kind][0, rows(r, n), :].astype(BF16))

        def scores(it, n=n, nb=nb, cw=cw, half=half, dil=dil):
            r = it // nb
            qb = it % nb
            kb = jnp.clip(qb - 1, 0, max(nb - 2, 0))
            qc = pl.multiple_of(r * n + qb * Q_TILE, Q_TILE)
            kc = pl.multiple_of(r * n + kb * Q_TILE, Q_TILE)
            qpos = qb * Q_TILE + lax.broadcasted_iota(jnp.int32, (Q_TILE, cw), 0)
            kpos = kb * Q_TILE + lax.broadcasted_iota(jnp.int32, (Q_TILE, cw), 1)
            valid = jnp.abs(qpos - kpos - half) <= half
            s = lax.dot_general(cls_ref[0, pl.ds(qc, Q_TILE), :], cls_ref[1, pl.ds(kc, cw), :],
                                (((1,), (1,)), ((), ())),
                                preferred_element_type=F32) * (HEAD_DIM ** -0.5)
            return jnp.where(valid, s, MASK_NEG), kc, r + qb * (Q_TILE * dil)

        def softmax(s):
            m = jnp.max(s, axis=1, keepdims=True)
            p = jnp.exp(s - m)
            l = jnp.sum(p, axis=1, keepdims=True)
            return p.astype(BF16), l, m + jnp.log(l)

        def unrolled(j, _, g=g, cw=cw, rows=rows, scores=scores, softmax=softmax):
            blocks = [scores(j * DIL_UNROLL + u) for u in range(DIL_UNROLL)]
            probs = [softmax(s) for s, _, _ in blocks]
            for (_, kc, q0), (p, l, lse) in zip(blocks, probs):
                o = jnp.dot(p, cls_ref[2, pl.ds(kc, cw), :], preferred_element_type=F32) / l
                og_ref[g, rows(q0, Q_TILE), :] = o
                lg_ref[g, rows(q0, Q_TILE), :] = jnp.broadcast_to(lse, (Q_TILE, HEAD_DIM))
            return 0

        lax.fori_loop(0, n_tiles // DIL_UNROLL, unrolled, 0)

    def merge(j, _):
        rs = pl.ds(pl.multiple_of(j * MERGE_ROWS, MERGE_ROWS), MERGE_ROWS)
        a = [lg_ref[g, rs, :] for g in range(N_DIL)]
        m = functools.reduce(jnp.maximum, a)
        e = [jnp.exp(x - m) for x in a]
        o = sum(e[g] * og_ref[g, rs, :] for g in range(N_DIL)) / sum(e)
        z = z_ref[0, rs, :]
        out_ref[0, rs, :] = (o * (z / (1.0 + jnp.exp(-z)))).astype(out_ref.dtype)
        return 0

    lax.fori_loop(0, L // MERGE_ROWS, merge, 0)


def _dilated_attention(proj, qkv_col0, gate_col0):
    bsz, L, _ = proj.shape
    for window, dil in DIL_PATTERNS:
        assert (L // dil) % Q_TILE == 0 and (window // dil) % 2 == 0
    assert (L // Q_TILE) % DIL_UNROLL == 0 and L % MERGE_ROWS == 0
    qkv_blk0, gate_blk0 = qkv_col0 // HEAD_DIM, gate_col0 // HEAD_DIM

    def in_spec(g, kind):
        return pl.BlockSpec((1, L, HEAD_DIM),
                            lambda b, h: (b, 0, qkv_blk0 + (g * 3 + kind) * DIL_HEADS + h))

    gate_spec = pl.BlockSpec((1, L, HEAD_DIM), lambda b, h: (b, 0, gate_blk0 + h))
    head_spec = pl.BlockSpec((1, L, HEAD_DIM), lambda b, h: (b, 0, h))
    return pl.pallas_call(
        functools.partial(_dil_kernel, L=L),
        out_shape=jax.ShapeDtypeStruct((bsz, L, DIL_WIDTH), BF16),
        grid=(bsz, DIL_HEADS),
        in_specs=[in_spec(g, kind) for g in range(N_DIL) for kind in range(3)] + [gate_spec],
        out_specs=head_spec,
        scratch_shapes=[pltpu.VMEM((N_DIL, L, HEAD_DIM), F32),
                        pltpu.VMEM((N_DIL, L, HEAD_DIM), F32),
                        pltpu.VMEM((3, L, HEAD_DIM), BF16)],
        compiler_params=_params(2, 48),
        name="dilated_attention",
    )(*([proj] * (3 * N_DIL + 1)))


def _out_norm_kernel(a0_ref, a1_ref, w0_ref, w1_ref, res_ref, g_ref, x_ref, h_ref):
    x = (jnp.dot(a0_ref[...], w0_ref[...], preferred_element_type=F32)
         + jnp.dot(a1_ref[...], w1_ref[...], preferred_element_type=F32) + res_ref[...])
    x_ref[...] = x
    ms = jnp.mean(x * x, axis=-1, keepdims=True)
    h_ref[...] = (x * lax.rsqrt(ms + EPS) * g_ref[...]).astype(h_ref.dtype)


def _even_out_norm(a_out, b_out, w_out, res, next_norm, tm=256):
    m, d = res.shape
    ka, kb = a_out.shape[1], b_out.shape[1]
    assert ka == kb and m % tm == 0
    w = w_out.astype(BF16)
    row = lambda k: pl.BlockSpec((tm, k), lambda i: (i, 0))
    wspec = lambda rb: pl.BlockSpec((ka, d), lambda i, rb=rb: (rb, 0),
                                    pipeline_mode=pl.Buffered(1))
    return pl.pallas_call(
        _out_norm_kernel,
        out_shape=[jax.ShapeDtypeStruct((m, d), F32), jax.ShapeDtypeStruct((m, d), BF16)],
        grid=(m // tm,),
        in_specs=[row(ka), row(kb), wspec(0), wspec(1), row(d),
                  pl.BlockSpec((1, d), lambda i: (0, 0))],
        out_specs=[row(d), row(d)],
        compiler_params=_params(1, 52),
        name="even_out_norm",
    )(a_out, b_out, w, w, res, next_norm.reshape(1, d))


def _even_layer(x2d, bsz, L, norm, w_in, lam_re, lam_im, log_step, b_re, b_im, c_re, c_im,
                d_skip, glu_w, glu_b, w_out, tables, h=None, next_norm=None):
    if h is None:
        h = _rmsnorm(x2d, norm, BF16)
    qkv0 = 2 * S5_WIDTH
    proj = _matmul([h], [w_in], w_in.shape[1], F32, epilogue="rope_qk", extra=tables,
                   seq_len=L, qkv_cols=(qkv0, qkv0 + QKV_WIDTH), name="even_in")
    proj3 = proj.reshape(bsz, L, -1)

    b_mat, c_mat, pow_re, pow_im = _s5_discretise(lam_re, lam_im, log_step, b_re, b_im,
                                                  c_re, c_im, L // SUBLANES)
    y = _s5(proj3, b_mat, c_mat, pow_re, pow_im, d_skip)
    a_out = _glu(y.reshape(bsz * L, S5_WIDTH), proj, glu_w, glu_b)

    b_out = _dilated_attention(proj3, qkv0, qkv0 + QKV_WIDTH).reshape(bsz * L, DIL_WIDTH)

    if next_norm is not None:
        return _even_out_norm(a_out, b_out, w_out, x2d, next_norm)
    return _matmul([a_out, b_out], [w_out, w_out], D_MODEL, F32, w_row_blocks=[0, 1],
                   res=x2d, name="even_out"), None


def _odd_prep_kernel(p_ref, qn_ref, kvn_ref, kin_ref, c_ref, sa_ref, sb_ref,
                     cq_ref, ckv_ref, kr_ref, ki_ref, wi_ref):
    def norm(x, g):
        ms = jnp.mean(x * x, axis=-1, keepdims=True)
        return x * lax.rsqrt(ms + EPS) * g

    c, sa, sb = c_ref[...], sa_ref[...], sb_ref[...]
    o = 0
    cq_ref[...] = norm(p_ref[:, o:o + Q_RANK], qn_ref[...]).astype(cq_ref.dtype)
    o += Q_RANK
    ckv_ref[...] = norm(p_ref[:, o:o + KV_RANK], kvn_ref[...]).astype(ckv_ref.dtype)
    o += KV_RANK
    lo = p_ref[:, o:o + LANES]
    hi = p_ref[:, o + LANES:o + 2 * LANES]
    lane = lax.broadcasted_iota(jnp.int32, lo.shape, 1)
    kr_ref[...] = _rope_head(jnp.where(lane < ROT_DIM, lo, 0.0), c, sa, sb)
    k_i = jnp.where(lane < LANES - ROT_DIM, pltpu.roll(lo, LANES - ROT_DIM, 1),
                    pltpu.roll(hi, LANES - ROT_DIM, 1))
    ki_ref[...] = _rope_head(norm(k_i, kin_ref[...]), c, sa, sb).astype(ki_ref.dtype)
    wi_ref[...] = hi * (IDX_HEADS ** -0.5)


def _odd_prep(p_small, q_norm, kv_norm, k_idx_norm, tables, L, tm=256):
    m = p_small.shape[0]
    tps = L // tm
    row = lambda w: pl.BlockSpec((tm, w), lambda i: (i, 0))
    tab = pl.BlockSpec((tm, 128), lambda i: (i % tps, 0))
    vec = lambda w: pl.BlockSpec((1, w), lambda i: (0, 0))
    return pl.pallas_call(
        _odd_prep_kernel,
        out_shape=[jax.ShapeDtypeStruct((m, Q_RANK), BF16),
                   jax.ShapeDtypeStruct((m, KV_RANK), BF16),
                   jax.ShapeDtypeStruct((m, 128), F32),
                   jax.ShapeDtypeStruct((m, 128), BF16),
                   jax.ShapeDtypeStruct((m, 128), F32)],
        grid=(m // tm,),
        in_specs=[row(ODD_SMALL), vec(Q_RANK), vec(KV_RANK), vec(IDX_DIM), tab, tab, tab],
        out_specs=[row(Q_RANK), row(KV_RANK), row(128), row(128), row(128)],
        compiler_params=_params(1, 32),
        name="odd_prep",
    )(p_small, q_norm.reshape(1, -1), kv_norm.reshape(1, -1), k_idx_norm.reshape(1, -1),
      *tables)


def _float_sort_key(s):
    b = pltpu.bitcast(s, jnp.int32)
    return b ^ ((b >> 31) & 0x7FFFFFFF)


def _indexer_kernel(q_ref, k_ref, w_ref, bias_ref, key_ref, jmax_ref, *, L, topk, kc, w_lane0):
    i = pl.program_id(1)
    nch = L // kc
    rows = IDX_ROWS
    n_vis = ((i + 1) * rows + kc - 1) // kc
    qpos = i * rows + lax.broadcasted_iota(jnp.int32, (rows, 1), 0)
    lane = lax.broadcasted_iota(jnp.int32, (rows, kc), 1)
    lane_t = lax.broadcasted_iota(jnp.int32, (Q_TILE, kc), 1)
    qpos_t = lax.broadcasted_iota(jnp.int32, (Q_TILE, 1), 0)

    def chunk(c, _):
        k0 = pl.multiple_of(c * kc, kc)
        kblk = k_ref[0, pl.ds(k0, kc), :]
        for t in range(rows // Q_TILE):
            rs = slice(t * Q_TILE, (t + 1) * Q_TILE)
            acc = jnp.zeros((Q_TILE, kc), F32)
            for h in range(IDX_HEADS):
                qh = q_ref[0, rs, h * IDX_DIM:(h + 1) * IDX_DIM]
                s = lax.dot_general(qh, kblk, (((1,), (1,)), ((), ())),
                                    preferred_element_type=F32)
                acc = acc + w_ref[0, rs, w_lane0 + h:w_lane0 + h + 1] * jnp.maximum(s, 0.0)
            acc = jnp.where(acc == 0.0, 0.0, acc)
            visible = lane_t + k0 <= qpos_t + (i * rows + t * Q_TILE)
            key_ref[c, rs, :] = jnp.where(visible, _float_sort_key(acc), INT_MIN)
        return 0

    keep_all = (i + 1) * rows <= topk
    lax.fori_loop(0, jnp.where(keep_all, 0, n_vis), chunk, 0)

    @pl.when(keep_all)
    def _():
        for c in range(nch):
            bias = jnp.where(lane + c * kc <= qpos, 0.0, MASK_NEG)
            for t in range(rows // Q_TILE):
                bias_ref[0, t, c] = bias[t * Q_TILE:(t + 1) * Q_TILE].astype(bias_ref.dtype)

    def select(nv):
        def count(pred):
            tot = jnp.zeros((rows, LANES), F32)
            for c in range(nv):
                hit = jnp.where(pred(key_ref[c], c), 1.0, 0.0)
                for j in range(kc // LANES):
                    tot = tot + hit[:, j * LANES:(j + 1) * LANES]
            return jnp.sum(tot, axis=1, keepdims=True)

        def tbit(b, t_u):
            cand_u = t_u | jnp.left_shift(jnp.int32(1), 31 - b)
            cand_s = cand_u ^ INT_MIN
            cnt = count(lambda key, c: key >= cand_s)
            return jnp.where(cnt >= topk, cand_u, t_u)

        t_u = lax.fori_loop(0, 32, tbit, jnp.zeros((rows, 1), jnp.int32))
        thr = t_u ^ INT_MIN

        need = topk - count(lambda key, c: key > thr)
        n_eq = count(lambda key, c: key == thr)
        jbits = max(1, (L - 1).bit_length())
        jmax_ref[...] = jnp.full((rows, 1), (1 << jbits) - 1, jnp.int32)

        unsettled = jnp.where(thr == INT_MIN, 0.0, jnp.abs(n_eq - need))

        @pl.when(jnp.max(unsettled) > 0.0)
        def _():
            def jbit(b, j):
                cand = j | jnp.left_shift(jnp.int32(1), jbits - 1 - b)
                cnt = count(lambda key, c: jnp.where(lane + c * kc < cand, key, thr - 1) == thr)
                return jnp.where(cnt < need, cand, j)

            jmax_ref[...] = lax.fori_loop(0, jbits, jbit, jnp.zeros((rows, 1), jnp.int32))

        jlim = jnp.minimum(jmax_ref[...], qpos)
        for c in range(nv):
            key = key_ref[c]
            tie = jnp.where(lane + c * kc <= jlim, 0.0, MASK_NEG)
            bias = jnp.where(key > thr, 0.0, jnp.where(key == thr, tie, MASK_NEG))
            for t in range(rows // Q_TILE):
                bias_ref[0, t, c] = bias[t * Q_TILE:(t + 1) * Q_TILE].astype(bias_ref.dtype)
        for c in range(nv, nch):
            for t in range(rows // Q_TILE):
                bias_ref[0, t, c] = jnp.full((Q_TILE, kc), MASK_NEG, bias_ref.dtype)

    for nv in range(1, nch + 1):
        pl.when(jnp.logical_and(jnp.logical_not(keep_all), n_vis == nv))(
            functools.partial(select, nv))


def _indexer(q_idx, k_idx, w_idx, topk, kc, w_lane0):
    bsz, L, _ = q_idx.shape
    nq, nch = L // Q_TILE, L // kc
    rows = IDX_ROWS
    assert L % rows == 0 and rows % Q_TILE == 0
    return pl.pallas_call(
        functools.partial(_indexer_kernel, L=L, topk=topk, kc=kc, w_lane0=w_lane0),
        out_shape=jax.ShapeDtypeStruct((bsz, nq, nch, Q_TILE, kc), BF16),
        grid=(bsz, L // rows),
        in_specs=[pl.BlockSpec((1, rows, IDX_HEADS * IDX_DIM), lambda b, i: (b, i, 0)),
                  pl.BlockSpec((1, L, IDX_DIM), lambda b, i: (b, 0, 0)),
                  pl.BlockSpec((1, rows, LANES), lambda b, i: (b, i, 0))],
        out_specs=pl.BlockSpec((1, rows // Q_TILE, nch, Q_TILE, kc),
                               lambda b, i: (b, i, 0, 0, 0)),
        scratch_shapes=[pltpu.VMEM((nch, rows, kc), jnp.int32),
                        pltpu.VMEM((rows, 1), jnp.int32)],
        compiler_params=_params(2, 32),
        name="dsa_indexer",
    )(q_idx, k_idx, w_idx)


def _mla_kernel(q_ref, ckv_ref, kr_ref, wk_ref, wv_ref, bias_ref, g_ref, o_ref,
                k_ref, v_ref, s_ref, mx_ref, acc_ref, *, kc, nh):
    i = pl.program_id(2)
    rows = ATT_ROWS
    n_vis = ((i + 1) * rows + kc - 1) // kc
    slabs = kc // LANES

    @pl.when(i == 0)
    def _():
        ckv = ckv_ref[0]
        kr = kr_ref[0]
        k_all = jnp.dot(ckv, wk_ref[...].astype(BF16), preferred_element_type=F32)
        for h in range(nh):
            hs = slice(h * HEAD_DIM, (h + 1) * HEAD_DIM)
            k_ref[:, hs] = (k_all[:, hs] + kr).astype(k_ref.dtype)
        v_ref[...] = jnp.dot(ckv, wv_ref[...].astype(BF16),
                             preferred_element_type=F32).astype(v_ref.dtype)

    def fold(x, op):
        t = x[:, 0:LANES]
        for j in range(1, slabs):
            t = op(t, x[:, j * LANES:(j + 1) * LANES])
        return t

    mx_ref[...] = jnp.full(mx_ref.shape, MASK_NEG, F32)

    def scores(c, _):
        k0 = pl.multiple_of(c * kc, kc)
        bias = jnp.concatenate([bias_ref[0, t, c] for t in range(rows // Q_TILE)],
                               axis=0).astype(F32)
        raw = [lax.dot_general(q_ref[0, :, h * HEAD_DIM:(h + 1) * HEAD_DIM],
                               k_ref[pl.ds(k0, kc), h * HEAD_DIM:(h + 1) * HEAD_DIM],
                               (((1,), (1,)), ((), ())), preferred_element_type=F32)
               for h in range(nh)]
        for h in range(nh):
            s = raw[h] + bias
            s_ref[c, h] = s
            mx_ref[h] = jnp.maximum(mx_ref[h], fold(s, jnp.maximum))
        return 0

    lax.fori_loop(0, n_vis, scores, 0)

    for h in range(nh):
        m = jnp.max(mx_ref[h], axis=1, keepdims=True)
        mx_ref[h] = jnp.broadcast_to(m, (rows, LANES))
    acc_ref[...] = jnp.zeros(acc_ref.shape, F32)

    ones = jnp.ones((kc, LANES), BF16)

    def weights(c, _):
        k0 = pl.multiple_of(c * kc, kc)
        for h in range(nh):
            hs = slice(h * HEAD_DIM, (h + 1) * HEAD_DIM)
            m = mx_ref[h]
            s = s_ref[c, h]
            p = jnp.concatenate(
                [jnp.exp2(s[:, j * LANES:(j + 1) * LANES] - m) for j in range(slabs)], axis=1)
            v1 = jnp.concatenate([v_ref[pl.ds(k0, kc), hs], ones], axis=1)
            acc_ref[h] += jnp.dot(p.astype(BF16), v1, preferred_element_type=F32)
        return 0

    lax.fori_loop(0, n_vis, weights, 0)

    for h in range(nh):
        hs = slice(h * HEAD_DIM, (h + 1) * HEAD_DIM)
        g = g_ref[0, :, hs].astype(F32)
        o = acc_ref[h, :, 0:HEAD_DIM] / acc_ref[h, :, HEAD_DIM:HEAD_DIM + LANES]
        o_ref[0, :, hs] = (o * (g / (1.0 + jnp.exp(-g)))).astype(o_ref.dtype)


def _masked_attention(q, c_kv, k_rope, w_k, w_v, bias, gate, kc, nh=ATT_HEADS):
    bsz, L, _ = q.shape
    rows, nch = ATT_ROWS, L // kc
    assert L % rows == 0 and rows % Q_TILE == 0
    w = nh * HEAD_DIM
    qspec = pl.BlockSpec((1, rows, w), lambda b, g, i: (b, i, g))
    wspec = pl.BlockSpec((KV_RANK, w), lambda b, g, i: (0, g))
    return pl.pallas_call(
        functools.partial(_mla_kernel, kc=kc, nh=nh),
        out_shape=jax.ShapeDtypeStruct((bsz, L, C_WIDTH), BF16),
        grid=(bsz, MLA_HEADS // nh, L // rows),
        in_specs=[qspec,
                  pl.BlockSpec((1, L, KV_RANK), lambda b, g, i: (b, 0, 0)),
                  pl.BlockSpec((1, L, HEAD_DIM), lambda b, g, i: (b, 0, 0)),
                  wspec, wspec,
                  pl.BlockSpec((1, rows // Q_TILE, nch, Q_TILE, kc),
                               lambda b, g, i: (b, i, 0, 0, 0)),
                  qspec],
        out_specs=qspec,
        scratch_shapes=[pltpu.VMEM((L, w), BF16),
                        pltpu.VMEM((L, w), BF16),
                        pltpu.VMEM((nch, nh, rows, kc), F32),
                        pltpu.VMEM((nh, rows, LANES), F32),
                        pltpu.VMEM((nh, rows, HEAD_DIM + LANES), F32)],
        compiler_params=_params(3, 48),
        name="dsa_attention",
    )(q, c_kv, k_rope, w_k, w_v, bias, gate)


def _odd_layer(x2d, bsz, L, norm, w_in, q_norm, kv_norm, k_idx_norm, w_uq, w_uk, w_uv, w_iq,
               w_out, tables, h=None):
    m = bsz * L
    if h is None:
        h = _rmsnorm(x2d, norm, BF16)

    w_in_t = w_in.T
    p_small = _matmul([h], [w_in_t], ODD_SMALL, F32, w_transposed=True, tn=256,
                      name="odd_in_small")
    gate = _matmul([h], [w_in_t], C_WIDTH, BF16, w_transposed=True, w_col0=ODD_GATE0,
                   name="odd_in_gate")
    c_q, c_kv, k_rope, k_idx, w_idx = _odd_prep(p_small, q_norm, kv_norm, k_idx_norm, tables, L)

    q = _matmul([c_q], [w_uq], C_WIDTH, BF16, epilogue="rope", extra=tables, seq_len=L,
                scale=LOG2E * HEAD_DIM ** -0.5, tn=2 * ROPE_SUBTILE, name="odd_q_up")
    q_idx = _matmul([c_q], [w_iq], IDX_HEADS * IDX_DIM, BF16, epilogue="rope", extra=tables,
                    seq_len=L, scale=IDX_DIM ** -0.5, tn=2 * ROPE_SUBTILE, name="odd_q_idx")

    w_k = jnp.pad(w_uk, ((0, 0), (0, 0), (ROT_DIM, 0))).reshape(KV_RANK, C_WIDTH)
    w_v = w_uv.reshape(KV_RANK, C_WIDTH)

    topk = min(IDX_TOPK_MAX, L // 4)
    kc = min(512, L)
    bias = _indexer(q_idx.reshape(bsz, L, -1), k_idx.reshape(bsz, L, IDX_DIM),
                    w_idx.reshape(bsz, L, LANES), topk, kc, IDX_W_LANE0)
    o = _masked_attention(q.reshape(bsz, L, C_WIDTH), c_kv.reshape(bsz, L, KV_RANK),
                          k_rope.reshape(bsz, L, HEAD_DIM), w_k, w_v, bias,
                          gate.reshape(bsz, L, C_WIDTH), kc)
    return _matmul([o.reshape(m, C_WIDTH)], [w_out], D_MODEL, F32, res=x2d, name="odd_out")


def kernel(x, even_norm, even_w_in, s5_lambda_re, s5_lambda_im, s5_log_step, s5_b_re, s5_b_im,
           s5_c_re, s5_c_im, s5_d, s5_glu_w, s5_glu_b, even_w_out, odd_norm, odd_w_in,
           mla_q_norm, mla_kv_norm, idx_k_norm, mla_w_uq, mla_w_uk, mla_w_uv, idx_w_q,
           odd_w_out, final_norm):
    bsz, L, d = x.shape
    depth = even_norm.shape[0] + odd_norm.shape[0]
    tables = _rope_tables(L)
    x2d = x.reshape(bsz * L, d)
    h = None
    for layer in range(depth):
        i = layer // 2
        if layer % 2 == 0:
            next_norm = odd_norm[i] if layer + 1 < depth else None
            x2d, h = _even_layer(x2d, bsz, L, even_norm[i], even_w_in[i], s5_lambda_re[i],
                                 s5_lambda_im[i], s5_log_step[i], s5_b_re[i], s5_b_im[i],
                                 s5_c_re[i], s5_c_im[i], s5_d[i], s5_glu_w[i], s5_glu_b[i],
                                 even_w_out[i], tables, h=h, next_norm=next_norm)
        else:
            x2d = _odd_layer(x2d, bsz, L, odd_norm[i], odd_w_in[i], mla_q_norm[i],
                             mla_kv_norm[i], idx_k_norm[i], mla_w_uq[i], mla_w_uk[i],
                             mla_w_uv[i], idx_w_q[i], odd_w_out[i], tables, h=h)
            h = None
    return _rmsnorm(x2d, final_norm, F32).reshape(bsz, L, d)
```

```python
import functools
import math

import jax
import jax.numpy as jnp
from jax import lax
from jax.experimental import pallas as pl
from jax.experimental.pallas import tpu as pltpu

F32 = jnp.float32
BF16 = jnp.bfloat16

LANES = 128
SUBLANES = 8

D_MODEL = 4096
EPS = 1e-6
ROPE_THETA = 500000.0
HEAD_DIM = 128
ROT_DIM = HEAD_DIM // 4
ROT_HALF = ROT_DIM // 2

S5_WIDTH = D_MODEL // 4
S5_GROUP = 16
S5_GROUPS = S5_WIDTH // S5_GROUP
S5_STATE = 64
S5_BLOCKS = S5_WIDTH // 128
S5_BLOCK_GROUPS = S5_GROUPS // S5_BLOCKS
S5_BLOCK_STATES = S5_BLOCK_GROUPS * S5_STATE

DIL_PATTERNS = ((128, 1), (512, 4), (2048, 16))
N_DIL = len(DIL_PATTERNS)
DIL_HEADS = D_MODEL // 512
DIL_WIDTH = DIL_HEADS * HEAD_DIM
QKV_WIDTH = 3 * N_DIL * DIL_WIDTH

MLA_HEADS = D_MODEL // HEAD_DIM
Q_RANK = 1536
KV_RANK = 512
IDX_HEADS = D_MODEL // 128
IDX_DIM = 128
IDX_TOPK_MAX = 256
C_WIDTH = MLA_HEADS * HEAD_DIM
ODD_GATE0 = Q_RANK + KV_RANK + ROT_DIM + IDX_DIM + IDX_HEADS
ODD_SMALL = Q_RANK + KV_RANK + 2 * LANES
IDX_W_LANE0 = ROT_DIM + IDX_DIM - LANES

MM_TILE_N = 512
MM_ROWS_DEEP = 1024
MM_ROWS_SHALLOW = 2048
Q_TILE = 128
ROPE_SUBTILE = 512
ATT_ROWS = 256
ATT_HEADS = 4
IDX_ROWS = 256
MASK_NEG = -1e30
LOG2E = 1.4426950408889634
INT_MIN = -(2 ** 31)


def _params(n_grid, vmem_mb):
    return pltpu.CompilerParams(
        dimension_semantics=("arbitrary",) * n_grid,
        vmem_limit_bytes=vmem_mb * 1024 * 1024)


def _rope_tables(L):
    inv = ROPE_THETA ** (-jnp.arange(0, ROT_DIM, 2, dtype=F32) / ROT_DIM)
    ang = jnp.arange(L, dtype=F32)[:, None] * inv[None, :]
    cos, sin = jnp.cos(ang), jnp.sin(ang)
    ones = jnp.ones((L, HEAD_DIM - ROT_DIM), F32)
    zeros = jnp.zeros((L, HEAD_DIM - ROT_DIM), F32)
    zh = jnp.zeros((L, ROT_HALF), F32)
    c = jnp.concatenate([cos, cos, ones], axis=1)
    sa = jnp.concatenate([-sin, zh, zeros], axis=1)
    sb = jnp.concatenate([zh, sin, zeros], axis=1)
    return c, sa, sb


def _rope_swap_tables(tables, width):
    c, sa, sb = tables
    col = jnp.arange(width)
    lane, head = col % HEAD_DIM, col // HEAD_DIM
    src = jnp.where(lane < ROT_HALF, col + ROT_HALF, col - ROT_HALF)
    sign = jnp.where(lane < ROT_HALF, -1.0, 1.0)
    perm = jnp.where((col[:, None] == src[None, :]) & (lane[None, :] < ROT_DIM)
                     & (head[:, None] == head[None, :]), sign[None, :], 0.0).astype(BF16)
    return c, sb - sa, perm


def _rope_head(x, c, sa, sb):
    return (x * c + pltpu.roll(x, HEAD_DIM - ROT_HALF, 1) * sa
            + pltpu.roll(x, ROT_HALF, 1) * sb)


def _rmsnorm_kernel(x_ref, g_ref, o_ref):
    x = x_ref[...]
    ms = jnp.mean(x * x, axis=-1, keepdims=True)
    o_ref[...] = (x * lax.rsqrt(ms + EPS) * g_ref[...]).astype(o_ref.dtype)


def _rmsnorm(x, g, out_dtype, tm=256):
    m, d = x.shape
    return pl.pallas_call(
        _rmsnorm_kernel,
        out_shape=jax.ShapeDtypeStruct((m, d), out_dtype),
        grid=(m // tm,),
        in_specs=[pl.BlockSpec((tm, d), lambda i: (i, 0)),
                  pl.BlockSpec((1, d), lambda i: (0, 0))],
        out_specs=pl.BlockSpec((tm, d), lambda i: (i, 0)),
        compiler_params=_params(1, 32),
        name="rmsnorm",
    )(x, g.reshape(1, d))


def _mm_kernel(*refs, n_a, has_res, epilogue, scale, tiles_per_kind, qkv_tiles, w_transposed):
    a_refs = refs[:n_a]
    w_refs = refs[n_a:2 * n_a]
    pos = 2 * n_a
    res_ref = None
    if has_res:
        res_ref = refs[pos]
        pos += 1
    extra = refs[pos:-1]
    o_ref = refs[-1]

    if epilogue == "rope":
        (a_ref,), (w_ref,) = a_refs, w_refs
        c_ref, s_ref, perm_ref = extra
        a = a_ref[...]
        for c0 in range(0, o_ref.shape[1], ROPE_SUBTILE):
            sub = jnp.dot(a, w_ref[:, c0:c0 + ROPE_SUBTILE].astype(BF16),
                          preferred_element_type=F32)
            for h0 in range(0, ROPE_SUBTILE, HEAD_DIM):
                x = sub[:, h0:h0 + HEAD_DIM]
                swapped = jnp.dot(x.astype(BF16), perm_ref[...], preferred_element_type=F32)
                y = (x * c_ref[...] + swapped * s_ref[...]) * scale
                o_ref[:, c0 + h0:c0 + h0 + HEAD_DIM] = y.astype(o_ref.dtype)
        return

    acc = None
    for a_ref, w_ref in zip(a_refs, w_refs):
        contract = (((1,), (1 if w_transposed else 0,)), ((), ()))
        d = lax.dot_general(a_ref[...], w_ref[...].astype(BF16), contract,
                            preferred_element_type=F32)
        acc = d if acc is None else acc + d
    if has_res:
        acc = acc + res_ref[...]
    n_heads = acc.shape[1] // HEAD_DIM

    def store_roped():
        c_ref, s_ref, perm_ref = extra
        for h in range(n_heads):
            hs = slice(h * HEAD_DIM, (h + 1) * HEAD_DIM)
            swapped = jnp.dot(acc[:, hs].astype(BF16), perm_ref[...],
                              preferred_element_type=F32)
            y = acc[:, hs] * c_ref[...] + swapped * s_ref[...]
            o_ref[:, hs] = (y * scale if scale != 1.0 else y).astype(o_ref.dtype)

    if epilogue == "none":
        o_ref[...] = acc.astype(o_ref.dtype)
    elif epilogue == "rope_qk":
        first, last = qkv_tiles
        j = pl.program_id(1)
        rotary = (j >= first) & (j < last) & (((j - first) // tiles_per_kind) % 3 != 2)

        @pl.when(jnp.logical_not(rotary))
        def _():
            o_ref[...] = acc.astype(o_ref.dtype)

        pl.when(rotary)(store_roped)
    else:
        raise ValueError(epilogue)


def _matmul(a_list, w_list, n, out_dtype, *, w_col0=0, w_row_blocks=None, res=None,
            epilogue="none", extra=(), scale=1.0, seq_len=None, w_transposed=False,
            qkv_cols=(0, 0), tn=MM_TILE_N, vmem_mb=52, name="matmul"):
    m = a_list[0].shape[0]
    deep = sum(a.shape[1] for a in a_list) >= D_MODEL
    tm = min(MM_ROWS_DEEP if deep else MM_ROWS_SHALLOW, m)
    assert m % tm == 0 and n % tn == 0
    n_a = len(a_list)
    if w_row_blocks is None:
        w_row_blocks = [0] * n_a
    in_specs, args = [], []
    for a in a_list:
        k = a.shape[1]
        mode = {} if deep else {"pipeline_mode": pl.Buffered(1)}
        in_specs.append(pl.BlockSpec((tm, k), lambda i, j: (i, 0), **mode))
        args.append(a)
    for a, w, rb in zip(a_list, w_list, w_row_blocks):
        k = a.shape[1]
        if not w_transposed:
            assert w_col0 % tn == 0
            in_specs.append(pl.BlockSpec((k, tn),
                                         lambda i, j, rb=rb: (rb, j + w_col0 // tn)))
        elif w_col0 % tn == 0:
            in_specs.append(pl.BlockSpec((tn, k), lambda i, j: (j + w_col0 // tn, 0)))
        else:
            g = math.gcd(w_col0, tn)
            assert g % SUBLANES == 0
            in_specs.append(pl.BlockSpec(
                (pl.Element(tn), pl.Element(k)),
                lambda i, j, g=g: ((w_col0 // g + j * (tn // g)) * g, 0)))
        args.append(w)
    if res is not None:
        in_specs.append(pl.BlockSpec((tm, tn), lambda i, j: (i, j)))
        args.append(res)
    tiles_per_seq = None if seq_len is None else seq_len // tm
    if epilogue in ("rope", "rope_qk"):
        extra = _rope_swap_tables(extra, HEAD_DIM)
    for e in extra:
        if e.shape == (HEAD_DIM, HEAD_DIM):
            in_specs.append(pl.BlockSpec((HEAD_DIM, HEAD_DIM), lambda i, j: (0, 0)))
        else:
            assert seq_len is not None and e.shape[0] == seq_len and seq_len % tm == 0
            in_specs.append(pl.BlockSpec((tm, e.shape[1]),
                                         lambda i, j: (i % tiles_per_seq, 0)))
        args.append(e)
    kern = functools.partial(_mm_kernel, n_a=n_a, has_res=res is not None,
                             epilogue=epilogue, scale=scale,
                             tiles_per_kind=DIL_WIDTH // tn,
                             qkv_tiles=(qkv_cols[0] // tn, qkv_cols[1] // tn),
                             w_transposed=w_transposed)
    return pl.pallas_call(
        kern,
        out_shape=jax.ShapeDtypeStruct((m, n), out_dtype),
        grid=(m // tm, n // tn),
        in_specs=in_specs,
        out_specs=pl.BlockSpec((tm, tn), lambda i, j: (i, j)),
        compiler_params=_params(2, vmem_mb),
        name=name,
    )(*args)


S5_UNROLL = 4


def _s5_discretise(lam_re, lam_im, log_step, b_re, b_im, c_re, c_im, seg_len):
    lr = jnp.minimum(lam_re.astype(F32), -1e-4)
    li = lam_im.astype(F32)
    dt = jnp.exp(log_step.astype(F32))[:, None]
    mag = jnp.exp(lr * dt)
    ar = mag * jnp.cos(li * dt)
    ai = mag * jnp.sin(li * dt)
    den = lr * lr + li * li
    nr, ni = ar - 1.0, ai
    cr = (nr * lr + ni * li) / den
    ci = (ni * lr - nr * li) / den
    br32, bi32 = b_re.astype(F32), b_im.astype(F32)
    bbr = cr[..., None] * br32 - ci[..., None] * bi32
    bbi = cr[..., None] * bi32 + ci[..., None] * br32

    eye = jnp.eye(S5_BLOCK_GROUPS, dtype=F32)

    def in_blockdiag(t):
        t = t.reshape(S5_BLOCKS, S5_BLOCK_GROUPS, S5_STATE, S5_GROUP)
        t = jnp.einsum("bgpj,gh->bgjhp", t, eye)
        return t.reshape(S5_BLOCKS, 128, S5_BLOCK_STATES)

    def out_blockdiag(t):
        t = t.reshape(S5_BLOCKS, S5_BLOCK_GROUPS, S5_GROUP, S5_STATE)
        t = jnp.einsum("bgjp,gh->bgphj", t, eye)
        return t.reshape(S5_BLOCKS, S5_BLOCK_STATES, 128)

    b_mat = jnp.concatenate([in_blockdiag(bbr), in_blockdiag(bbi)], axis=2).astype(BF16)
    c_mat = jnp.concatenate([out_blockdiag(c_re.astype(F32)),
                             -out_blockdiag(c_im.astype(F32))], axis=1).astype(BF16)

    a_r = ar.reshape(S5_BLOCKS, S5_BLOCK_STATES)
    a_i = ai.reshape(S5_BLOCKS, S5_BLOCK_STATES)
    s_r, s_i = jnp.ones_like(a_r), jnp.zeros_like(a_i)
    b_r, b_i, e = a_r, a_i, seg_len
    while e:
        if e & 1:
            s_r, s_i = s_r * b_r - s_i * b_i, s_r * b_i + s_i * b_r
        b_r, b_i = b_r * b_r - b_i * b_i, 2.0 * b_r * b_i
        e >>= 1
    pad = jnp.zeros((S5_BLOCKS, SUBLANES - 2, S5_BLOCK_STATES), F32)
    pow_re = jnp.concatenate([a_r[:, None], s_r[:, None], pad], axis=1)
    pow_im = jnp.concatenate([a_i[:, None], s_i[:, None], pad], axis=1)
    return b_mat, c_mat, pow_re, pow_im


def _s5_kernel(u_ref, b_ref, c_ref, pre_ref, pim_ref, d_ref, y_ref, up_ref, s_ref, yp_ref):
    L = u_ref.shape[1]
    ns = S5_BLOCK_STATES
    seg = L // SUBLANES

    def gather(r, _):
        up_ref[pl.ds(pl.multiple_of(r * SUBLANES, SUBLANES), SUBLANES), :] = (
            u_ref[0, pl.ds(r, SUBLANES, stride=seg), :])
        return 0

    lax.fori_loop(0, seg, gather, 0, unroll=S5_UNROLL)
    u = up_ref[...]
    s_ref[...] = jnp.dot(u.astype(BF16), b_ref[0], preferred_element_type=F32)

    pre, pim = pre_ref[0], pim_ref[0]
    ar = jnp.broadcast_to(pre[0:1, :], (SUBLANES, ns))
    ai = jnp.broadcast_to(pim[0:1, :], (SUBLANES, ns))

    def load(r, half):
        return s_ref[pl.ds(pl.multiple_of(r * SUBLANES, SUBLANES), SUBLANES),
                     half * ns:(half + 1) * ns]

    def store(r, half, x):
        s_ref[pl.ds(pl.multiple_of(r * SUBLANES, SUBLANES), SUBLANES),
              half * ns:(half + 1) * ns] = x

    def local(r, carry):
        sr, si = carry
        sr, si = ar * sr - ai * si + load(r, 0), ar * si + ai * sr + load(r, 1)
        store(r, 0, sr)
        store(r, 1, si)
        return sr, si

    zeros = jnp.zeros((SUBLANES, ns), F32)
    er, ei = lax.fori_loop(0, seg, local, (zeros, zeros), unroll=S5_UNROLL)

    gr, gi = pre[1:2, :], pim[1:2, :]
    cr, ci = [jnp.zeros((1, ns), F32)], [jnp.zeros((1, ns), F32)]
    for j in range(1, SUBLANES):
        pr, pi = cr[-1], ci[-1]
        cr.append(er[j - 1:j, :] + gr * pr - gi * pi)
        ci.append(ei[j - 1:j, :] + gr * pi + gi * pr)
    dr, di = jnp.concatenate(cr, axis=0), jnp.concatenate(ci, axis=0)

    def carried(r, carry):
        dr, di = carry
        dr, di = ar * dr - ai * di, ar * di + ai * dr
        store(r, 0, load(r, 0) + dr)
        store(r, 1, load(r, 1) + di)
        return dr, di

    lax.fori_loop(0, seg, carried, (dr, di), unroll=S5_UNROLL)

    y = jnp.dot(s_ref[...].astype(BF16), c_ref[0], preferred_element_type=F32)
    yp_ref[...] = y + d_ref[0] * u
    for j in range(SUBLANES):
        y_ref[0, j * seg:(j + 1) * seg, :] = yp_ref[pl.ds(j, seg, stride=SUBLANES), :]


def _s5(proj_a, b_mat, c_mat, pow_re, pow_im, d_skip):
    bsz, L, _ = proj_a.shape
    ns = S5_BLOCK_STATES
    return pl.pallas_call(
        _s5_kernel,
        out_shape=jax.ShapeDtypeStruct((bsz, L, S5_WIDTH), F32),
        grid=(bsz, S5_BLOCKS),
        in_specs=[
            pl.BlockSpec((1, L, 128), lambda b, g: (b, 0, g)),
            pl.BlockSpec((1, 128, 2 * ns), lambda b, g: (g, 0, 0)),
            pl.BlockSpec((1, 2 * ns, 128), lambda b, g: (g, 0, 0)),
            pl.BlockSpec((1, SUBLANES, ns), lambda b, g: (g, 0, 0)),
            pl.BlockSpec((1, SUBLANES, ns), lambda b, g: (g, 0, 0)),
            pl.BlockSpec((1, 1, 128), lambda b, g: (g, 0, 0)),
        ],
        out_specs=pl.BlockSpec((1, L, 128), lambda b, g: (b, 0, g)),
        scratch_shapes=[pltpu.VMEM((L, LANES), F32),
                        pltpu.VMEM((L, 2 * ns), F32),
                        pltpu.VMEM((L, LANES), F32)],
        compiler_params=_params(2, 40),
        name="s5_scan",
    )(proj_a, b_mat, c_mat, pow_re, pow_im, d_skip.reshape(S5_BLOCKS, 1, 128))


def _glu_kernel(y_ref, z_ref, w_ref, b_ref, o_ref):
    g = jax.nn.gelu(y_ref[...])
    t = jnp.dot(g.astype(BF16), w_ref[...].astype(BF16), preferred_element_type=F32) + b_ref[...]
    z = z_ref[...]
    gate = z / (1.0 + jnp.exp(-z))
    o_ref[...] = (g / (1.0 + jnp.exp(-t)) * gate).astype(o_ref.dtype)


def _glu(y, proj_a, glu_w, glu_b, tm=512):
    m = y.shape[0]
    return pl.pallas_call(
        _glu_kernel,
        out_shape=jax.ShapeDtypeStruct((m, S5_WIDTH), BF16),
        grid=(m // tm,),
        in_specs=[pl.BlockSpec((tm, S5_WIDTH), lambda i: (i, 0)),
                  pl.BlockSpec((tm, S5_WIDTH), lambda i: (i, 1)),
                  pl.BlockSpec((S5_WIDTH, S5_WIDTH), lambda i: (0, 0)),
                  pl.BlockSpec((1, S5_WIDTH), lambda i: (0, 0))],
        out_specs=pl.BlockSpec((tm, S5_WIDTH), lambda i: (i, 0)),
        compiler_params=_params(1, 40),
        name="s5_glu",
    )(y, proj_a, glu_w, glu_b.reshape(1, S5_WIDTH))


DIL_UNROLL = 16
MERGE_ROWS = 256


def _dil_kernel(*refs, L):
    qkv_refs = refs[:3 * N_DIL]
    z_ref, out_ref, og_ref, lg_ref, cls_ref = refs[3 * N_DIL:]
    n_tiles = L // Q_TILE

    for g, (window, dil) in enumerate(DIL_PATTERNS):
        n = L // dil
        nb = n // Q_TILE
        cw = Q_TILE if nb == 1 else 2 * Q_TILE
        half = (window // dil) // 2

        def rows(start, size, dil=dil):
            return pl.ds(start, size) if dil == 1 else pl.ds(start, size, stride=dil)

        for kind in range(3):
            for r in range(dil):
                cls_ref[kind, r * n:(r + 1) * n, :] = (
                    qkv_refs[3 * g + kind][0, rows(r, n), :].astype(BF16))

        def scores(it, n=n, nb=nb, cw=cw, half=half, dil=dil):
            r = it // nb
            qb = it % nb
            kb = jnp.clip(qb - 1, 0, max(nb - 2, 0))
            qc = pl.multiple_of(r * n + qb * Q_TILE, Q_TILE)
            kc = pl.multiple_of(r * n + kb * Q_TILE, Q_TILE)
            qpos = qb * Q_TILE + lax.broadcasted_iota(jnp.int32, (Q_TILE, cw), 0)
            kpos = kb * Q_TILE + lax.broadcasted_iota(jnp.int32, (Q_TILE, cw), 1)
            valid = jnp.abs(qpos - kpos - half) <= half
            s = lax.dot_general(cls_ref[0, pl.ds(qc, Q_TILE), :], cls_ref[1, pl.ds(kc, cw), :],
                                (((1,), (1,)), ((), ())),
                                preferred_element_type=F32) * (HEAD_DIM ** -0.5)
            return jnp.where(valid, s, MASK_NEG), kc, r + qb * (Q_TILE * dil)

        def softmax(s):
            m = jnp.max(s, axis=1, keepdims=True)
            p = jnp.exp(s - m)
            l = jnp.sum(p, axis=1, keepdims=True)
            return p.astype(BF16), l, m + jnp.log(l)

        def unrolled(j, _, g=g, cw=cw, rows=rows, scores=scores, softmax=softmax):
            blocks = [scores(j * DIL_UNROLL + u) for u in range(DIL_UNROLL)]
            probs = [softmax(s) for s, _, _ in blocks]
            for (_, kc, q0), (p, l, lse) in zip(blocks, probs):
                o = jnp.dot(p, cls_ref[2, pl.ds(kc, cw), :], preferred_element_type=F32) / l
                og_ref[g, rows(q0, Q_TILE), :] = o
                lg_ref[g, rows(q0, Q_TILE), :] = jnp.broadcast_to(lse, (Q_TILE, HEAD_DIM))
            return 0

        lax.fori_loop(0, n_tiles // DIL_UNROLL, unrolled, 0)

    def merge(j, _):
        rs = pl.ds(pl.multiple_of(j * MERGE_ROWS, MERGE_ROWS), MERGE_ROWS)
        a = [lg_ref[g, rs, :] for g in range(N_DIL)]
        m = functools.reduce(jnp.maximum, a)
        e = [jnp.exp(x - m) for x in a]
        o = sum(e[g] * og_ref[g, rs, :] for g in range(N_DIL)) / sum(e)
        z = z_ref[0, rs, :]
        out_ref[0, rs, :] = (o * (z / (1.0 + jnp.exp(-z)))).astype(out_ref.dtype)
        return 0

    lax.fori_loop(0, L // MERGE_ROWS, merge, 0)


def _dilated_attention(proj, qkv_col0, gate_col0):
    bsz, L, _ = proj.shape
    for window, dil in DIL_PATTERNS:
        assert (L // dil) % Q_TILE == 0 and (window // dil) % 2 == 0
    assert (L // Q_TILE) % DIL_UNROLL == 0 and L % MERGE_ROWS == 0
    qkv_blk0, gate_blk0 = qkv_col0 // HEAD_DIM, gate_col0 // HEAD_DIM

    def in_spec(g, kind):
        return pl.BlockSpec((1, L, HEAD_DIM),
                            lambda b, h: (b, 0, qkv_blk0 + (g * 3 + kind) * DIL_HEADS + h))

    gate_spec = pl.BlockSpec((1, L, HEAD_DIM), lambda b, h: (b, 0, gate_blk0 + h))
    head_spec = pl.BlockSpec((1, L, HEAD_DIM), lambda b, h: (b, 0, h))
    return pl.pallas_call(
        functools.partial(_dil_kernel, L=L),
        out_shape=jax.ShapeDtypeStruct((bsz, L, DIL_WIDTH), BF16),
        grid=(bsz, DIL_HEADS),
        in_specs=[in_spec(g, kind) for g in range(N_DIL) for kind in range(3)] + [gate_spec],
        out_specs=head_spec,
        scratch_shapes=[pltpu.VMEM((N_DIL, L, HEAD_DIM), F32),
                        pltpu.VMEM((N_DIL, L, HEAD_DIM), F32),
                        pltpu.VMEM((3, L, HEAD_DIM), BF16)],
        compiler_params=_params(2, 48),
        name="dilated_attention",
    )(*([proj] * (3 * N_DIL + 1)))


def _out_norm_kernel(a0_ref, a1_ref, w0_ref, w1_ref, res_ref, g_ref, x_ref, h_ref):
    x = (jnp.dot(a0_ref[...], w0_ref[...], preferred_element_type=F32)
         + jnp.dot(a1_ref[...], w1_ref[...], preferred_element_type=F32) + res_ref[...])
    x_ref[...] = x
    ms = jnp.mean(x * x, axis=-1, keepdims=True)
    h_ref[...] = (x * lax.rsqrt(ms + EPS) * g_ref[...]).astype(h_ref.dtype)


def _even_out_norm(a_out, b_out, w_out, res, next_norm, tm=256):
    m, d = res.shape
    ka, kb = a_out.shape[1], b_out.shape[1]
    assert ka == kb and m % tm == 0
    w = w_out.astype(BF16)
    row = lambda k: pl.BlockSpec((tm, k), lambda i: (i, 0))
    wspec = lambda rb: pl.BlockSpec((ka, d), lambda i, rb=rb: (rb, 0),
                                    pipeline_mode=pl.Buffered(1))
    return pl.pallas_call(
        _out_norm_kernel,
        out_shape=[jax.ShapeDtypeStruct((m, d), F32), jax.ShapeDtypeStruct((m, d), BF16)],
        grid=(m // tm,),
        in_specs=[row(ka), row(kb), wspec(0), wspec(1), row(d),
                  pl.BlockSpec((1, d), lambda i: (0, 0))],
        out_specs=[row(d), row(d)],
        compiler_params=_params(1, 52),
        name="even_out_norm",
    )(a_out, b_out, w, w, res, next_norm.reshape(1, d))


def _even_layer(x2d, bsz, L, norm, w_in, lam_re, lam_im, log_step, b_re, b_im, c_re, c_im,
                d_skip, glu_w, glu_b, w_out, tables, h=None, next_norm=None):
    if h is None:
        h = _rmsnorm(x2d, norm, BF16)
    qkv0 = 2 * S5_WIDTH
    proj = _matmul([h], [w_in], w_in.shape[1], F32, epilogue="rope_qk", extra=tables,
                   seq_len=L, qkv_cols=(qkv0, qkv0 + QKV_WIDTH), name="even_in")
    proj3 = proj.reshape(bsz, L, -1)

    b_mat, c_mat, pow_re, pow_im = _s5_discretise(lam_re, lam_im, log_step, b_re, b_im,
                                                  c_re, c_im, L // SUBLANES)
    y = _s5(proj3, b_mat, c_mat, pow_re, pow_im, d_skip)
    a_out = _glu(y.reshape(bsz * L, S5_WIDTH), proj, glu_w, glu_b)

    b_out = _dilated_attention(proj3, qkv0, qkv0 + QKV_WIDTH).reshape(bsz * L, DIL_WIDTH)

    if next_norm is not None:
        return _even_out_norm(a_out, b_out, w_out, x2d, next_norm)
    return _matmul([a_out, b_out], [w_out, w_out], D_MODEL, F32, w_row_blocks=[0, 1],
                   res=x2d, name="even_out"), None


def _odd_prep_kernel(p_ref, qn_ref, kvn_ref, kin_ref, c_ref, sa_ref, sb_ref,
                     cq_ref, ckv_ref, kr_ref, ki_ref, wi_ref):
    def norm(x, g):
        ms = jnp.mean(x * x, axis=-1, keepdims=True)
        return x * lax.rsqrt(ms + EPS) * g

    c, sa, sb = c_ref[...], sa_ref[...], sb_ref[...]
    o = 0
    cq_ref[...] = norm(p_ref[:, o:o + Q_RANK], qn_ref[...]).astype(cq_ref.dtype)
    o += Q_RANK
    ckv_ref[...] = norm(p_ref[:, o:o + KV_RANK], kvn_ref[...]).astype(ckv_ref.dtype)
    o += KV_RANK
    lo = p_ref[:, o:o + LANES]
    hi = p_ref[:, o + LANES:o + 2 * LANES]
    lane = lax.broadcasted_iota(jnp.int32, lo.shape, 1)
    kr_ref[...] = _rope_head(jnp.where(lane < ROT_DIM, lo, 0.0), c, sa, sb)
    k_i = jnp.where(lane < LANES - ROT_DIM, pltpu.roll(lo, LANES - ROT_DIM, 1),
                    pltpu.roll(hi, LANES - ROT_DIM, 1))
    ki_ref[...] = _rope_head(norm(k_i, kin_ref[...]), c, sa, sb).astype(ki_ref.dtype)
    wi_ref[...] = hi * (IDX_HEADS ** -0.5)


def _odd_prep(p_small, q_norm, kv_norm, k_idx_norm, tables, L, tm=256):
    m = p_small.shape[0]
    tps = L // tm
    row = lambda w: pl.BlockSpec((tm, w), lambda i: (i, 0))
    tab = pl.BlockSpec((tm, 128), lambda i: (i % tps, 0))
    vec = lambda w: pl.BlockSpec((1, w), lambda i: (0, 0))
    return pl.pallas_call(
        _odd_prep_kernel,
        out_shape=[jax.ShapeDtypeStruct((m, Q_RANK), BF16),
                   jax.ShapeDtypeStruct((m, KV_RANK), BF16),
                   jax.ShapeDtypeStruct((m, 128), F32),
                   jax.ShapeDtypeStruct((m, 128), BF16),
                   jax.ShapeDtypeStruct((m, 128), F32)],
        grid=(m // tm,),
        in_specs=[row(ODD_SMALL), vec(Q_RANK), vec(KV_RANK), vec(IDX_DIM), tab, tab, tab],
        out_specs=[row(Q_RANK), row(KV_RANK), row(128), row(128), row(128)],
        compiler_params=_params(1, 32),
        name="odd_prep",
    )(p_small, q_norm.reshape(1, -1), kv_norm.reshape(1, -1), k_idx_norm.reshape(1, -1),
      *tables)


def _float_sort_key(s):
    b = pltpu.bitcast(s, jnp.int32)
    return b ^ ((b >> 31) & 0x7FFFFFFF)


def _indexer_kernel(q_ref, k_ref, w_ref, bias_ref, key_ref, jmax_ref, *, L, topk, kc, w_lane0):
    i = pl.program_id(1)
    nch = L // kc
    rows = IDX_ROWS
    n_vis = ((i + 1) * rows + kc - 1) // kc
    qpos = i * rows + lax.broadcasted_iota(jnp.int32, (rows, 1), 0)
    lane = lax.broadcasted_iota(jnp.int32, (rows, kc), 1)
    lane_t = lax.broadcasted_iota(jnp.int32, (Q_TILE, kc), 1)
    qpos_t = lax.broadcasted_iota(jnp.int32, (Q_TILE, 1), 0)

    def chunk(c, _):
        k0 = pl.multiple_of(c * kc, kc)
        kblk = k_ref[0, pl.ds(k0, kc), :]
        for t in range(rows // Q_TILE):
            rs = slice(t * Q_TILE, (t + 1) * Q_TILE)
            acc = jnp.zeros((Q_TILE, kc), F32)
            for h in range(IDX_HEADS):
                qh = q_ref[0, rs, h * IDX_DIM:(h + 1) * IDX_DIM]
                s = lax.dot_general(qh, kblk, (((1,), (1,)), ((), ())),
                                    preferred_element_type=F32)
                acc = acc + w_ref[0, rs, w_lane0 + h:w_lane0 + h + 1] * jnp.maximum(s, 0.0)
            acc = jnp.where(acc == 0.0, 0.0, acc)
            visible = lane_t + k0 <= qpos_t + (i * rows + t * Q_TILE)
            key_ref[c, rs, :] = jnp.where(visible, _float_sort_key(acc), INT_MIN)
        return 0

    keep_all = (i + 1) * rows <= topk
    lax.fori_loop(0, jnp.where(keep_all, 0, n_vis), chunk, 0)

    @pl.when(keep_all)
    def _():
        for c in range(nch):
            bias = jnp.where(lane + c * kc <= qpos, 0.0, MASK_NEG)
            for t in range(rows // Q_TILE):
                bias_ref[0, t, c] = bias[t * Q_TILE:(t + 1) * Q_TILE].astype(bias_ref.dtype)

    def select(nv):
        def count(pred):
            tot = jnp.zeros((rows, LANES), F32)
            for c in range(nv):
                hit = jnp.where(pred(key_ref[c], c), 1.0, 0.0)
                for j in range(kc // LANES):
                    tot = tot + hit[:, j * LANES:(j + 1) * LANES]
            return jnp.sum(tot, axis=1, keepdims=True)

        def tbit(b, t_u):
            cand_u = t_u | jnp.left_shift(jnp.int32(1), 31 - b)
            cand_s = cand_u ^ INT_MIN
            cnt = count(lambda key, c: key >= cand_s)
            return jnp.where(cnt >= topk, cand_u, t_u)

        t_u = lax.fori_loop(0, 32, tbit, jnp.zeros((rows, 1), jnp.int32))
        thr = t_u ^ INT_MIN

        need = topk - count(lambda key, c: key > thr)
        n_eq = count(lambda key, c: key == thr)
        jbits = max(1, (L - 1).bit_length())
        jmax_ref[...] = jnp.full((rows, 1), (1 << jbits) - 1, jnp.int32)

        unsettled = jnp.where(thr == INT_MIN, 0.0, jnp.abs(n_eq - need))

        @pl.when(jnp.max(unsettled) > 0.0)
        def _():
            def jbit(b, j):
                cand = j | jnp.left_shift(jnp.int32(1), jbits - 1 - b)
                cnt = count(lambda key, c: jnp.where(lane + c * kc < cand, key, thr - 1) == thr)
                return jnp.where(cnt < need, cand, j)

            jmax_ref[...] = lax.fori_loop(0, jbits, jbit, jnp.zeros((rows, 1), jnp.int32))

        jlim = jnp.minimum(jmax_ref[...], qpos)
        for c in range(nv):
            key = key_ref[c]
            tie = jnp.where(lane + c * kc <= jlim, 0.0, MASK_NEG)
            bias = jnp.where(key > thr, 0.0, jnp.where(key == thr, tie, MASK_NEG))
            for t in range(rows // Q_TILE):
                bias_ref[0, t, c] = bias[t * Q_TILE:(t + 1) * Q_TILE].astype(bias_ref.dtype)
        for c in range(nv, nch):
            for t in range(rows // Q_TILE):
                bias_ref[0, t, c] = jnp.full((Q_TILE, kc), MASK_NEG, bias_ref.dtype)

    for nv in range(1, nch + 1):
        pl.when(jnp.logical_and(jnp.logical_not(keep_all), n_vis == nv))(
            functools.partial(select, nv))


def _indexer(q_idx, k_idx, w_idx, topk, kc, w_lane0):
    bsz, L, _ = q_idx.shape
    nq, nch = L // Q_TILE, L // kc
    rows = IDX_ROWS
    assert L % rows == 0 and rows % Q_TILE == 0
    return pl.pallas_call(
        functools.partial(_indexer_kernel, L=L, topk=topk, kc=kc, w_lane0=w_lane0),
        out_shape=jax.ShapeDtypeStruct((bsz, nq, nch, Q_TILE, kc), BF16),
        grid=(bsz, L // rows),
        in_specs=[pl.BlockSpec((1, rows, IDX_HEADS * IDX_DIM), lambda b, i: (b, i, 0)),
                  pl.BlockSpec((1, L, IDX_DIM), lambda b, i: (b, 0, 0)),
                  pl.BlockSpec((1, rows, LANES), lambda b, i: (b, i, 0))],
        out_specs=pl.BlockSpec((1, rows // Q_TILE, nch, Q_TILE, kc),
                               lambda b, i: (b, i, 0, 0, 0)),
        scratch_shapes=[pltpu.VMEM((nch, rows, kc), jnp.int32),
                        pltpu.VMEM((rows, 1), jnp.int32)],
        compiler_params=_params(2, 32),
        name="dsa_indexer",
    )(q_idx, k_idx, w_idx)


def _mla_kernel(q_ref, ckv_ref, kr_ref, wk_ref, wv_ref, bias_ref, g_ref, o_ref,
                k_ref, v_ref, s_ref, mx_ref, acc_ref, *, kc, nh):
    i = pl.program_id(2)
    rows = ATT_ROWS
    n_vis = ((i + 1) * rows + kc - 1) // kc
    slabs = kc // LANES

    @pl.when(i == 0)
    def _():
        ckv = ckv_ref[0]
        kr = kr_ref[0]
        k_all = jnp.dot(ckv, wk_ref[...].astype(BF16), preferred_element_type=F32)
        for h in range(nh):
            hs = slice(h * HEAD_DIM, (h + 1) * HEAD_DIM)
            k_ref[:, hs] = (k_all[:, hs] + kr).astype(k_ref.dtype)
        v_ref[...] = jnp.dot(ckv, wv_ref[...].astype(BF16),
                             preferred_element_type=F32).astype(v_ref.dtype)

    def fold(x, op):
        t = x[:, 0:LANES]
        for j in range(1, slabs):
            t = op(t, x[:, j * LANES:(j + 1) * LANES])
        return t

    mx_ref[...] = jnp.full(mx_ref.shape, MASK_NEG, F32)

    def scores(c, _):
        k0 = pl.multiple_of(c * kc, kc)
        bias = jnp.concatenate([bias_ref[0, t, c] for t in range(rows // Q_TILE)],
                               axis=0).astype(F32)
        raw = [lax.dot_general(q_ref[0, :, h * HEAD_DIM:(h + 1) * HEAD_DIM],
                               k_ref[pl.ds(k0, kc), h * HEAD_DIM:(h + 1) * HEAD_DIM],
                               (((1,), (1,)), ((), ())), preferred_element_type=F32)
               for h in range(nh)]
        for h in range(nh):
            s = raw[h] + bias
            s_ref[c, h] = s
            mx_ref[h] = jnp.maximum(mx_ref[h], fold(s, jnp.maximum))
        return 0

    lax.fori_loop(0, n_vis, scores, 0)

    for h in range(nh):
        m = jnp.max(mx_ref[h], axis=1, keepdims=True)
        mx_ref[h] = jnp.broadcast_to(m, (rows, LANES))
    acc_ref[...] = jnp.zeros(acc_ref.shape, F32)

    ones = jnp.ones((kc, LANES), BF16)

    def weights(c, _):
        k0 = pl.multiple_of(c * kc, kc)
        for h in range(nh):
            hs = slice(h * HEAD_DIM, (h + 1) * HEAD_DIM)
            m = mx_ref[h]
            s = s_ref[c, h]
            p = jnp.concatenate(
                [jnp.exp2(s[:, j * LANES:(j + 1) * LANES] - m) for j in range(slabs)], axis=1)
            v1 = jnp.concatenate([v_ref[pl.ds(k0, kc), hs], ones], axis=1)
            acc_ref[h] += jnp.dot(p.astype(BF16), v1, preferred_element_type=F32)
        return 0

    lax.fori_loop(0, n_vis, weights, 0)

    for h in range(nh):
        hs = slice(h * HEAD_DIM, (h + 1) * HEAD_DIM)
        g = g_ref[0, :, hs].astype(F32)
        o = acc_ref[h, :, 0:HEAD_DIM] / acc_ref[h, :, HEAD_DIM:HEAD_DIM + LANES]
        o_ref[0, :, hs] = (o * (g / (1.0 + jnp.exp(-g)))).astype(o_ref.dtype)


def _masked_attention(q, c_kv, k_rope, w_k, w_v, bias, gate, kc, nh=ATT_HEADS):
    bsz, L, _ = q.shape
    rows, nch = ATT_ROWS, L // kc
    assert L % rows == 0 and rows % Q_TILE == 0
    w = nh * HEAD_DIM
    qspec = pl.BlockSpec((1, rows, w), lambda b, g, i: (b, i, g))
    wspec = pl.BlockSpec((KV_RANK, w), lambda b, g, i: (0, g))
    return pl.pallas_call(
        functools.partial(_mla_kernel, kc=kc, nh=nh),
        out_shape=jax.ShapeDtypeStruct((bsz, L, C_WIDTH), BF16),
        grid=(bsz, MLA_HEADS // nh, L // rows),
        in_specs=[qspec,
                  pl.BlockSpec((1, L, KV_RANK), lambda b, g, i: (b, 0, 0)),
                  pl.BlockSpec((1, L, HEAD_DIM), lambda b, g, i: (b, 0, 0)),
                  wspec, wspec,
                  pl.BlockSpec((1, rows // Q_TILE, nch, Q_TILE, kc),
                               lambda b, g, i: (b, i, 0, 0, 0)),
                  qspec],
        out_specs=qspec,
        scratch_shapes=[pltpu.VMEM((L, w), BF16),
                        pltpu.VMEM((L, w), BF16),
                        pltpu.VMEM((nch, nh, rows, kc), F32),
                        pltpu.VMEM((nh, rows, LANES), F32),
                        pltpu.VMEM((nh, rows, HEAD_DIM + LANES), F32)],
        compiler_params=_params(3, 48),
        name="dsa_attention",
    )(q, c_kv, k_rope, w_k, w_v, bias, gate)


def _odd_layer(x2d, bsz, L, norm, w_in, q_norm, kv_norm, k_idx_norm, w_uq, w_uk, w_uv, w_iq,
               w_out, tables, h=None):
    m = bsz * L
    if h is None:
        h = _rmsnorm(x2d, norm, BF16)

    w_in_t = w_in.T
    p_small = _matmul([h], [w_in_t], ODD_SMALL, F32, w_transposed=True, tn=256,
                      name="odd_in_small")
    gate = _matmul([h], [w_in_t], C_WIDTH, BF16, w_transposed=True, w_col0=ODD_GATE0,
                   name="odd_in_gate")
    c_q, c_kv, k_rope, k_idx, w_idx = _odd_prep(p_small, q_norm, kv_norm, k_idx_norm, tables, L)

    q = _matmul([c_q], [w_uq], C_WIDTH, BF16, epilogue="rope", extra=tables, seq_len=L,
                scale=LOG2E * HEAD_DIM ** -0.5, tn=2 * ROPE_SUBTILE, name="odd_q_up")
    q_idx = _matmul([c_q], [w_iq], IDX_HEADS * IDX_DIM, BF16, epilogue="rope", extra=tables,
                    seq_len=L, scale=IDX_DIM ** -0.5, tn=2 * ROPE_SUBTILE, name="odd_q_idx")

    w_k = jnp.pad(w_uk, ((0, 0), (0, 0), (ROT_DIM, 0))).reshape(KV_RANK, C_WIDTH)
    w_v = w_uv.reshape(KV_RANK, C_WIDTH)

    topk = min(IDX_TOPK_MAX, L // 4)
    kc = min(512, L)
    bias = _indexer(q_idx.reshape(bsz, L, -1), k_idx.reshape(bsz, L, IDX_DIM),
                    w_idx.reshape(bsz, L, LANES), topk, kc, IDX_W_LANE0)
    o = _masked_attention(q.reshape(bsz, L, C_WIDTH), c_kv.reshape(bsz, L, KV_RANK),
                          k_rope.reshape(bsz, L, HEAD_DIM), w_k, w_v, bias,
                          gate.reshape(bsz, L, C_WIDTH), kc)
    return _matmul([o.reshape(m, C_WIDTH)], [w_out], D_MODEL, F32, res=x2d, name="odd_out")


def kernel(x, even_norm, even_w_in, s5_lambda_re, s5_lambda_im, s5_log_step, s5_b_re, s5_b_im,
           s5_c_re, s5_c_im, s5_d, s5_glu_w, s5_glu_b, even_w_out, odd_norm, odd_w_in,
           mla_q_norm, mla_kv_norm, idx_k_norm, mla_w_uq, mla_w_uk, mla_w_uv, idx_w_q,
           odd_w_out, final_norm):
    bsz, L, d = x.shape
    depth = even_norm.shape[0] + odd_norm.shape[0]
    tables = _rope_tables(L)
    x2d = x.reshape(bsz * L, d)
    h = None
    for layer in range(depth):
        i = layer // 2
        if layer % 2 == 0:
            next_norm = odd_norm[i] if layer + 1 < depth else None
            x2d, h = _even_layer(x2d, bsz, L, even_norm[i], even_w_in[i], s5_lambda_re[i],
                                 s5_lambda_im[i], s5_log_step[i], s5_b_re[i], s5_b_im[i],
                                 s5_c_re[i], s5_c_im[i], s5_d[i], s5_glu_w[i], s5_glu_b[i],
                                 even_w_out[i], tables, h=h, next_norm=next_norm)
        else:
            x2d = _odd_layer(x2d, bsz, L, odd_norm[i], odd_w_in[i], mla_q_norm[i],
                             mla_kv_norm[i], idx_k_norm[i], mla_w_uq[i], mla_w_uk[i],
                             mla_w_uv[i], idx_w_q[i], odd_w_out[i], tables, h=h)
            h = None
    return _rmsnorm(x2d, final_norm, F32).reshape(bsz, L, d)
```

```python
import functools
import math

import jax
import jax.numpy as jnp
from jax import lax
from jax.experimental import pallas as pl
from jax.experimental.pallas import tpu as pltpu

F32 = jnp.float32
BF16 = jnp.bfloat16

LANES = 128
SUBLANES = 8

D_MODEL = 4096
EPS = 1e-6
ROPE_THETA = 500000.0
HEAD_DIM = 128
ROT_DIM = HEAD_DIM // 4
ROT_HALF = ROT_DIM // 2

S5_WIDTH = D_MODEL // 4
S5_GROUP = 16
S5_GROUPS = S5_WIDTH // S5_GROUP
S5_STATE = 64
S5_BLOCKS = S5_WIDTH // 128
S5_BLOCK_GROUPS = S5_GROUPS // S5_BLOCKS
S5_BLOCK_STATES = S5_BLOCK_GROUPS * S5_STATE

DIL_PATTERNS = ((128, 1), (512, 4), (2048, 16))
N_DIL = len(DIL_PATTERNS)
DIL_HEADS = D_MODEL // 512
DIL_WIDTH = DIL_HEADS * HEAD_DIM
QKV_WIDTH = 3 * N_DIL * DIL_WIDTH

MLA_HEADS = D_MODEL // HEAD_DIM
Q_RANK = 1536
KV_RANK = 512
IDX_HEADS = D_MODEL // 128
IDX_DIM = 128
IDX_TOPK_MAX = 256
C_WIDTH = MLA_HEADS * HEAD_DIM
ODD_GATE0 = Q_RANK + KV_RANK + ROT_DIM + IDX_DIM + IDX_HEADS
ODD_SMALL = Q_RANK + KV_RANK + 2 * LANES
IDX_W_LANE0 = ROT_DIM + IDX_DIM - LANES

MM_TILE_N = 512
MM_ROWS_DEEP = 1024
MM_ROWS_SHALLOW = 2048
Q_TILE = 128
ROPE_SUBTILE = 512
ATT_ROWS = 256
ATT_HEADS = 8
IDX_ROWS = 256
MASK_NEG = -1e30
LOG2E = 1.4426950408889634
INT_MIN = -(2 ** 31)


def _params(n_grid, vmem_mb):
    return pltpu.CompilerParams(
        dimension_semantics=("arbitrary",) * n_grid,
        vmem_limit_bytes=vmem_mb * 1024 * 1024)


def _rope_tables(L):
    inv = ROPE_THETA ** (-jnp.arange(0, ROT_DIM, 2, dtype=F32) / ROT_DIM)
    ang = jnp.arange(L, dtype=F32)[:, None] * inv[None, :]
    cos, sin = jnp.cos(ang), jnp.sin(ang)
    ones = jnp.ones((L, HEAD_DIM - ROT_DIM), F32)
    zeros = jnp.zeros((L, HEAD_DIM - ROT_DIM), F32)
    zh = jnp.zeros((L, ROT_HALF), F32)
    c = jnp.concatenate([cos, cos, ones], axis=1)
    sa = jnp.concatenate([-sin, zh, zeros], axis=1)
    sb = jnp.concatenate([zh, sin, zeros], axis=1)
    return c, sa, sb


def _rope_swap_tables(tables, width):
    c, sa, sb = tables
    col = jnp.arange(width)
    lane, head = col % HEAD_DIM, col // HEAD_DIM
    src = jnp.where(lane < ROT_HALF, col + ROT_HALF, col - ROT_HALF)
    sign = jnp.where(lane < ROT_HALF, -1.0, 1.0)
    perm = jnp.where((col[:, None] == src[None, :]) & (lane[None, :] < ROT_DIM)
                     & (head[:, None] == head[None, :]), sign[None, :], 0.0).astype(BF16)
    return c, sb - sa, perm


def _rope_head(x, c, sa, sb):
    return (x * c + pltpu.roll(x, HEAD_DIM - ROT_HALF, 1) * sa
            + pltpu.roll(x, ROT_HALF, 1) * sb)


def _rmsnorm_kernel(x_ref, g_ref, o_ref):
    x = x_ref[...]
    ms = jnp.mean(x * x, axis=-1, keepdims=True)
    o_ref[...] = (x * lax.rsqrt(ms + EPS) * g_ref[...]).astype(o_ref.dtype)


def _rmsnorm(x, g, out_dtype, tm=256):
    m, d = x.shape
    return pl.pallas_call(
        _rmsnorm_kernel,
        out_shape=jax.ShapeDtypeStruct((m, d), out_dtype),
        grid=(m // tm,),
        in_specs=[pl.BlockSpec((tm, d), lambda i: (i, 0)),
                  pl.BlockSpec((1, d), lambda i: (0, 0))],
        out_specs=pl.BlockSpec((tm, d), lambda i: (i, 0)),
        compiler_params=_params(1, 32),
        name="rmsnorm",
    )(x, g.reshape(1, d))


def _mm_kernel(*refs, n_a, has_res, epilogue, scale, tiles_per_kind, qkv_tiles, w_transposed):
    a_refs = refs[:n_a]
    w_refs = refs[n_a:2 * n_a]
    pos = 2 * n_a
    res_ref = None
    if has_res:
        res_ref = refs[pos]
        pos += 1
    extra = refs[pos:-1]
    o_ref = refs[-1]

    if epilogue == "rope":
        (a_ref,), (w_ref,) = a_refs, w_refs
        c_ref, s_ref, perm_ref = extra
        a = a_ref[...]
        for c0 in range(0, o_ref.shape[1], ROPE_SUBTILE):
            sub = jnp.dot(a, w_ref[:, c0:c0 + ROPE_SUBTILE].astype(BF16),
                          preferred_element_type=F32)
            for h0 in range(0, ROPE_SUBTILE, HEAD_DIM):
                x = sub[:, h0:h0 + HEAD_DIM]
                swapped = jnp.dot(x.astype(BF16), perm_ref[...], preferred_element_type=F32)
                y = (x * c_ref[...] + swapped * s_ref[...]) * scale
                o_ref[:, c0 + h0:c0 + h0 + HEAD_DIM] = y.astype(o_ref.dtype)
        return

    acc = None
    for a_ref, w_ref in zip(a_refs, w_refs):
        contract = (((1,), (1 if w_transposed else 0,)), ((), ()))
        d = lax.dot_general(a_ref[...], w_ref[...].astype(BF16), contract,
                            preferred_element_type=F32)
        acc = d if acc is None else acc + d
    if has_res:
        acc = acc + res_ref[...]
    n_heads = acc.shape[1] // HEAD_DIM

    def store_roped():
        c_ref, s_ref, perm_ref = extra
        for h in range(n_heads):
            hs = slice(h * HEAD_DIM, (h + 1) * HEAD_DIM)
            swapped = jnp.dot(acc[:, hs].astype(BF16), perm_ref[...],
                              preferred_element_type=F32)
            y = acc[:, hs] * c_ref[...] + swapped * s_ref[...]
            o_ref[:, hs] = (y * scale if scale != 1.0 else y).astype(o_ref.dtype)

    if epilogue == "none":
        o_ref[...] = acc.astype(o_ref.dtype)
    elif epilogue == "rope_qk":
        first, last = qkv_tiles
        j = pl.program_id(1)
        rotary = (j >= first) & (j < last) & (((j - first) // tiles_per_kind) % 3 != 2)

        @pl.when(jnp.logical_not(rotary))
        def _():
            o_ref[...] = acc.astype(o_ref.dtype)

        pl.when(rotary)(store_roped)
    else:
        raise ValueError(epilogue)


def _matmul(a_list, w_list, n, out_dtype, *, w_col0=0, w_row_blocks=None, res=None,
            epilogue="none", extra=(), scale=1.0, seq_len=None, w_transposed=False,
            qkv_cols=(0, 0), tn=MM_TILE_N, vmem_mb=52, name="matmul"):
    m = a_list[0].shape[0]
    deep = sum(a.shape[1] for a in a_list) >= D_MODEL
    tm = min(MM_ROWS_DEEP if deep else MM_ROWS_SHALLOW, m)
    assert m % tm == 0 and n % tn == 0
    n_a = len(a_list)
    if w_row_blocks is None:
        w_row_blocks = [0] * n_a
    in_specs, args = [], []
    for a in a_list:
        k = a.shape[1]
        mode = {} if deep else {"pipeline_mode": pl.Buffered(1)}
        in_specs.append(pl.BlockSpec((tm, k), lambda i, j: (i, 0), **mode))
        args.append(a)
    for a, w, rb in zip(a_list, w_list, w_row_blocks):
        k = a.shape[1]
        if not w_transposed:
            assert w_col0 % tn == 0
            in_specs.append(pl.BlockSpec((k, tn),
                                         lambda i, j, rb=rb: (rb, j + w_col0 // tn)))
        elif w_col0 % tn == 0:
            in_specs.append(pl.BlockSpec((tn, k), lambda i, j: (j + w_col0 // tn, 0)))
        else:
            g = math.gcd(w_col0, tn)
            assert g % SUBLANES == 0
            in_specs.append(pl.BlockSpec(
                (pl.Element(tn), pl.Element(k)),
                lambda i, j, g=g: ((w_col0 // g + j * (tn // g)) * g, 0)))
        args.append(w)
    if res is not None:
        in_specs.append(pl.BlockSpec((tm, tn), lambda i, j: (i, j)))
        args.append(res)
    tiles_per_seq = None if seq_len is None else seq_len // tm
    if epilogue in ("rope", "rope_qk"):
        extra = _rope_swap_tables(extra, HEAD_DIM)
    for e in extra:
        if e.shape == (HEAD_DIM, HEAD_DIM):
            in_specs.append(pl.BlockSpec((HEAD_DIM, HEAD_DIM), lambda i, j: (0, 0)))
        else:
            assert seq_len is not None and e.shape[0] == seq_len and seq_len % tm == 0
            in_specs.append(pl.BlockSpec((tm, e.shape[1]),
                                         lambda i, j: (i % tiles_per_seq, 0)))
        args.append(e)
    kern = functools.partial(_mm_kernel, n_a=n_a, has_res=res is not None,
                             epilogue=epilogue, scale=scale,
                             tiles_per_kind=DIL_WIDTH // tn,
                             qkv_tiles=(qkv_cols[0] // tn, qkv_cols[1] // tn),
                             w_transposed=w_transposed)
    return pl.pallas_call(
        kern,
        out_shape=jax.ShapeDtypeStruct((m, n), out_dtype),
        grid=(m // tm, n // tn),
        in_specs=in_specs,
        out_specs=pl.BlockSpec((tm, tn), lambda i, j: (i, j)),
        compiler_params=_params(2, vmem_mb),
        name=name,
    )(*args)


S5_UNROLL = 4


def _s5_discretise(lam_re, lam_im, log_step, b_re, b_im, c_re, c_im, seg_len):
    lr = jnp.minimum(lam_re.astype(F32), -1e-4)
    li = lam_im.astype(F32)
    dt = jnp.exp(log_step.astype(F32))[:, None]
    mag = jnp.exp(lr * dt)
    ar = mag * jnp.cos(li * dt)
    ai = mag * jnp.sin(li * dt)
    den = lr * lr + li * li
    nr, ni = ar - 1.0, ai
    cr = (nr * lr + ni * li) / den
    ci = (ni * lr - nr * li) / den
    br32, bi32 = b_re.astype(F32), b_im.astype(F32)
    bbr = cr[..., None] * br32 - ci[..., None] * bi32
    bbi = cr[..., None] * bi32 + ci[..., None] * br32

    eye = jnp.eye(S5_BLOCK_GROUPS, dtype=F32)

    def in_blockdiag(t):
        t = t.reshape(S5_BLOCKS, S5_BLOCK_GROUPS, S5_STATE, S5_GROUP)
        t = jnp.einsum("bgpj,gh->bgjhp", t, eye)
        return t.reshape(S5_BLOCKS, 128, S5_BLOCK_STATES)

    def out_blockdiag(t):
        t = t.reshape(S5_BLOCKS, S5_BLOCK_GROUPS, S5_GROUP, S5_STATE)
        t = jnp.einsum("bgjp,gh->bgphj", t, eye)
        return t.reshape(S5_BLOCKS, S5_BLOCK_STATES, 128)

    b_mat = jnp.concatenate([in_blockdiag(bbr), in_blockdiag(bbi)], axis=2).astype(BF16)
    c_mat = jnp.concatenate([out_blockdiag(c_re.astype(F32)),
                             -out_blockdiag(c_im.astype(F32))], axis=1).astype(BF16)

    a_r = ar.reshape(S5_BLOCKS, S5_BLOCK_STATES)
    a_i = ai.reshape(S5_BLOCKS, S5_BLOCK_STATES)
    s_r, s_i = jnp.ones_like(a_r), jnp.zeros_like(a_i)
    b_r, b_i, e = a_r, a_i, seg_len
    while e:
        if e & 1:
            s_r, s_i = s_r * b_r - s_i * b_i, s_r * b_i + s_i * b_r
        b_r, b_i = b_r * b_r - b_i * b_i, 2.0 * b_r * b_i
        e >>= 1
    pad = jnp.zeros((S5_BLOCKS, SUBLANES - 2, S5_BLOCK_STATES), F32)
    pow_re = jnp.concatenate([a_r[:, None], s_r[:, None], pad], axis=1)
    pow_im = jnp.concatenate([a_i[:, None], s_i[:, None], pad], axis=1)
    return b_mat, c_mat, pow_re, pow_im


def _s5_kernel(u_ref, b_ref, c_ref, pre_ref, pim_ref, d_ref, y_ref, up_ref, s_ref, yp_ref):
    L = u_ref.shape[1]
    ns = S5_BLOCK_STATES
    seg = L // SUBLANES

    def gather(r, _):
        up_ref[pl.ds(pl.multiple_of(r * SUBLANES, SUBLANES), SUBLANES), :] = (
            u_ref[0, pl.ds(r, SUBLANES, stride=seg), :])
        return 0

    lax.fori_loop(0, seg, gather, 0, unroll=S5_UNROLL)
    u = up_ref[...]
    s_ref[...] = jnp.dot(u.astype(BF16), b_ref[0], preferred_element_type=F32)

    pre, pim = pre_ref[0], pim_ref[0]
    ar = jnp.broadcast_to(pre[0:1, :], (SUBLANES, ns))
    ai = jnp.broadcast_to(pim[0:1, :], (SUBLANES, ns))

    def load(r, half):
        return s_ref[pl.ds(pl.multiple_of(r * SUBLANES, SUBLANES), SUBLANES),
                     half * ns:(half + 1) * ns]

    def store(r, half, x):
        s_ref[pl.ds(pl.multiple_of(r * SUBLANES, SUBLANES), SUBLANES),
              half * ns:(half + 1) * ns] = x

    def local(r, carry):
        sr, si = carry
        sr, si = ar * sr - ai * si + load(r, 0), ar * si + ai * sr + load(r, 1)
        store(r, 0, sr)
        store(r, 1, si)
        return sr, si

    zeros = jnp.zeros((SUBLANES, ns), F32)
    er, ei = lax.fori_loop(0, seg, local, (zeros, zeros), unroll=S5_UNROLL)

    gr, gi = pre[1:2, :], pim[1:2, :]
    cr, ci = [jnp.zeros((1, ns), F32)], [jnp.zeros((1, ns), F32)]
    for j in range(1, SUBLANES):
        pr, pi = cr[-1], ci[-1]
        cr.append(er[j - 1:j, :] + gr * pr - gi * pi)
        ci.append(ei[j - 1:j, :] + gr * pi + gi * pr)
    dr, di = jnp.concatenate(cr, axis=0), jnp.concatenate(ci, axis=0)

    def carried(r, carry):
        dr, di = carry
        dr, di = ar * dr - ai * di, ar * di + ai * dr
        store(r, 0, load(r, 0) + dr)
        store(r, 1, load(r, 1) + di)
        return dr, di

    lax.fori_loop(0, seg, carried, (dr, di), unroll=S5_UNROLL)

    y = jnp.dot(s_ref[...].astype(BF16), c_ref[0], preferred_element_type=F32)
    yp_ref[...] = y + d_ref[0] * u
    for j in range(SUBLANES):
        y_ref[0, j * seg:(j + 1) * seg, :] = yp_ref[pl.ds(j, seg, stride=SUBLANES), :]


def _s5(proj_a, b_mat, c_mat, pow_re, pow_im, d_skip):
    bsz, L, _ = proj_a.shape
    ns = S5_BLOCK_STATES
    return pl.pallas_call(
        _s5_kernel,
        out_shape=jax.ShapeDtypeStruct((bsz, L, S5_WIDTH), F32),
        grid=(bsz, S5_BLOCKS),
        in_specs=[
            pl.BlockSpec((1, L, 128), lambda b, g: (b, 0, g)),
            pl.BlockSpec((1, 128, 2 * ns), lambda b, g: (g, 0, 0)),
            pl.BlockSpec((1, 2 * ns, 128), lambda b, g: (g, 0, 0)),
            pl.BlockSpec((1, SUBLANES, ns), lambda b, g: (g, 0, 0)),
            pl.BlockSpec((1, SUBLANES, ns), lambda b, g: (g, 0, 0)),
            pl.BlockSpec((1, 1, 128), lambda b, g: (g, 0, 0)),
        ],
        out_specs=pl.BlockSpec((1, L, 128), lambda b, g: (b, 0, g)),
        scratch_shapes=[pltpu.VMEM((L, LANES), F32),
                        pltpu.VMEM((L, 2 * ns), F32),
                        pltpu.VMEM((L, LANES), F32)],
        compiler_params=_params(2, 40),
        name="s5_scan",
    )(proj_a, b_mat, c_mat, pow_re, pow_im, d_skip.reshape(S5_BLOCKS, 1, 128))


def _glu_kernel(y_ref, z_ref, w_ref, b_ref, o_ref):
    g = jax.nn.gelu(y_ref[...])
    t = jnp.dot(g.astype(BF16), w_ref[...].astype(BF16), preferred_element_type=F32) + b_ref[...]
    z = z_ref[...]
    gate = z / (1.0 + jnp.exp(-z))
    o_ref[...] = (g / (1.0 + jnp.exp(-t)) * gate).astype(o_ref.dtype)


def _glu(y, proj_a, glu_w, glu_b, tm=512):
    m = y.shape[0]
    return pl.pallas_call(
        _glu_kernel,
        out_shape=jax.ShapeDtypeStruct((m, S5_WIDTH), BF16),
        grid=(m // tm,),
        in_specs=[pl.BlockSpec((tm, S5_WIDTH), lambda i: (i, 0)),
                  pl.BlockSpec((tm, S5_WIDTH), lambda i: (i, 1)),
                  pl.BlockSpec((S5_WIDTH, S5_WIDTH), lambda i: (0, 0)),
                  pl.BlockSpec((1, S5_WIDTH), lambda i: (0, 0))],
        out_specs=pl.BlockSpec((tm, S5_WIDTH), lambda i: (i, 0)),
        compiler_params=_params(1, 40),
        name="s5_glu",
    )(y, proj_a, glu_w, glu_b.reshape(1, S5_WIDTH))


DIL_UNROLL = 16
MERGE_ROWS = 256


def _dil_kernel(*refs, L):
    qkv_refs = refs[:3 * N_DIL]
    z_ref, out_ref, og_ref, lg_ref, cls_ref = refs[3 * N_DIL:]
    n_tiles = L // Q_TILE

    for g, (window, dil) in enumerate(DIL_PATTERNS):
        n = L // dil
        nb = n // Q_TILE
        cw = Q_TILE if nb == 1 else 2 * Q_TILE
        half = (window // dil) // 2

        def rows(start, size, dil=dil):
            return pl.ds(start, size) if dil == 1 else pl.ds(start, size, stride=dil)

        for kind in range(3):
            for r in range(dil):
                cls_ref[kind, r * n:(r + 1) * n, :] = (
                    qkv_refs[3 * g + kind][0, rows(r, n), :].astype(BF16))

        def scores(it, n=n, nb=nb, cw=cw, half=half, dil=dil):
            r = it // nb
            qb = it % nb
            kb = jnp.clip(qb - 1, 0, max(nb - 2, 0))
            qc = pl.multiple_of(r * n + qb * Q_TILE, Q_TILE)
            kc = pl.multiple_of(r * n + kb * Q_TILE, Q_TILE)
            qpos = qb * Q_TILE + lax.broadcasted_iota(jnp.int32, (Q_TILE, cw), 0)
            kpos = kb * Q_TILE + lax.broadcasted_iota(jnp.int32, (Q_TILE, cw), 1)
            valid = jnp.abs(qpos - kpos - half) <= half
            s = lax.dot_general(cls_ref[0, pl.ds(qc, Q_TILE), :], cls_ref[1, pl.ds(kc, cw), :],
                                (((1,), (1,)), ((), ())),
                                preferred_element_type=F32) * (HEAD_DIM ** -0.5)
            return jnp.where(valid, s, MASK_NEG), kc, r + qb * (Q_TILE * dil)

        def softmax(s):
            m = jnp.max(s, axis=1, keepdims=True)
            p = jnp.exp(s - m)
            l = jnp.sum(p, axis=1, keepdims=True)
            return p.astype(BF16), l, m + jnp.log(l)

        def unrolled(j, _, g=g, cw=cw, rows=rows, scores=scores, softmax=softmax):
            blocks = [scores(j * DIL_UNROLL + u) for u in range(DIL_UNROLL)]
            probs = [softmax(s) for s, _, _ in blocks]
            for (_, kc, q0), (p, l, lse) in zip(blocks, probs):
                o = jnp.dot(p, cls_ref[2, pl.ds(kc, cw), :], preferred_element_type=F32) / l
                og_ref[g, rows(q0, Q_TILE), :] = o
                lg_ref[g, rows(q0, Q_TILE), :] = jnp.broadcast_to(lse, (Q_TILE, HEAD_DIM))
            return 0

        lax.fori_loop(0, n_tiles // DIL_UNROLL, unrolled, 0)

    def merge(j, _):
        rs = pl.ds(pl.multiple_of(j * MERGE_ROWS, MERGE_ROWS), MERGE_ROWS)
        a = [lg_ref[g, rs, :] for g in range(N_DIL)]
        m = functools.reduce(jnp.maximum, a)
        e = [jnp.exp(x - m) for x in a]
        o = sum(e[g] * og_ref[g, rs, :] for g in range(N_DIL)) / sum(e)
        z = z_ref[0, rs, :]
        out_ref[0, rs, :] = (o * (z / (1.0 + jnp.exp(-z)))).astype(out_ref.dtype)
        return 0

    lax.fori_loop(0, L // MERGE_ROWS, merge, 0)


def _dilated_attention(proj, qkv_col0, gate_col0):
    bsz, L, _ = proj.shape
    for window, dil in DIL_PATTERNS:
        assert (L // dil) % Q_TILE == 0 and (window // dil) % 2 == 0
    assert (L // Q_TILE) % DIL_UNROLL == 0 and L % MERGE_ROWS == 0
    qkv_blk0, gate_blk0 = qkv_col0 // HEAD_DIM, gate_col0 // HEAD_DIM

    def in_spec(g, kind):
        return pl.BlockSpec((1, L, HEAD_DIM),
                            lambda b, h: (b, 0, qkv_blk0 + (g * 3 + kind) * DIL_HEADS + h))

    gate_spec = pl.BlockSpec((1, L, HEAD_DIM), lambda b, h: (b, 0, gate_blk0 + h))
    head_spec = pl.BlockSpec((1, L, HEAD_DIM), lambda b, h: (b, 0, h))
    return pl.pallas_call(
        functools.partial(_dil_kernel, L=L),
        out_shape=jax.ShapeDtypeStruct((bsz, L, DIL_WIDTH), BF16),
        grid=(bsz, DIL_HEADS),
        in_specs=[in_spec(g, kind) for g in range(N_DIL) for kind in range(3)] + [gate_spec],
        out_specs=head_spec,
        scratch_shapes=[pltpu.VMEM((N_DIL, L, HEAD_DIM), F32),
                        pltpu.VMEM((N_DIL, L, HEAD_DIM), F32),
                        pltpu.VMEM((3, L, HEAD_DIM), BF16)],
        compiler_params=_params(2, 48),
        name="dilated_attention",
    )(*([proj] * (3 * N_DIL + 1)))


def _out_norm_kernel(a0_ref, a1_ref, w0_ref, w1_ref, res_ref, g_ref, x_ref, h_ref):
    x = (jnp.dot(a0_ref[...], w0_ref[...], preferred_element_type=F32)
         + jnp.dot(a1_ref[...], w1_ref[...], preferred_element_type=F32) + res_ref[...])
    x_ref[...] = x
    ms = jnp.mean(x * x, axis=-1, keepdims=True)
    h_ref[...] = (x * lax.rsqrt(ms + EPS) * g_ref[...]).astype(h_ref.dtype)


def _even_out_norm(a_out, b_out, w_out, res, next_norm, tm=256):
    m, d = res.shape
    ka, kb = a_out.shape[1], b_out.shape[1]
    assert ka == kb and m % tm == 0
    w = w_out.astype(BF16)
    row = lambda k: pl.BlockSpec((tm, k), lambda i: (i, 0))
    wspec = lambda rb: pl.BlockSpec((ka, d), lambda i, rb=rb: (rb, 0),
                                    pipeline_mode=pl.Buffered(1))
    return pl.pallas_call(
        _out_norm_kernel,
        out_shape=[jax.ShapeDtypeStruct((m, d), F32), jax.ShapeDtypeStruct((m, d), BF16)],
        grid=(m // tm,),
        in_specs=[row(ka), row(kb), wspec(0), wspec(1), row(d),
                  pl.BlockSpec((1, d), lambda i: (0, 0))],
        out_specs=[row(d), row(d)],
        compiler_params=_params(1, 52),
        name="even_out_norm",
    )(a_out, b_out, w, w, res, next_norm.reshape(1, d))


def _even_layer(x2d, bsz, L, norm, w_in, lam_re, lam_im, log_step, b_re, b_im, c_re, c_im,
                d_skip, glu_w, glu_b, w_out, tables, h=None, next_norm=None):
    if h is None:
        h = _rmsnorm(x2d, norm, BF16)
    qkv0 = 2 * S5_WIDTH
    proj = _matmul([h], [w_in], w_in.shape[1], F32, epilogue="rope_qk", extra=tables,
                   seq_len=L, qkv_cols=(qkv0, qkv0 + QKV_WIDTH), name="even_in")
    proj3 = proj.reshape(bsz, L, -1)

    b_mat, c_mat, pow_re, pow_im = _s5_discretise(lam_re, lam_im, log_step, b_re, b_im,
                                                  c_re, c_im, L // SUBLANES)
    y = _s5(proj3, b_mat, c_mat, pow_re, pow_im, d_skip)
    a_out = _glu(y.reshape(bsz * L, S5_WIDTH), proj, glu_w, glu_b)

    b_out = _dilated_attention(proj3, qkv0, qkv0 + QKV_WIDTH).reshape(bsz * L, DIL_WIDTH)

    if next_norm is not None:
        return _even_out_norm(a_out, b_out, w_out, x2d, next_norm)
    return _matmul([a_out, b_out], [w_out, w_out], D_MODEL, F32, w_row_blocks=[0, 1],
                   res=x2d, name="even_out"), None


def _odd_prep_kernel(p_ref, qn_ref, kvn_ref, kin_ref, c_ref, sa_ref, sb_ref,
                     cq_ref, ckv_ref, kr_ref, ki_ref, wi_ref):
    def norm(x, g):
        ms = jnp.mean(x * x, axis=-1, keepdims=True)
        return x * lax.rsqrt(ms + EPS) * g

    c, sa, sb = c_ref[...], sa_ref[...], sb_ref[...]
    o = 0
    cq_ref[...] = norm(p_ref[:, o:o + Q_RANK], qn_ref[...]).astype(cq_ref.dtype)
    o += Q_RANK
    ckv_ref[...] = norm(p_ref[:, o:o + KV_RANK], kvn_ref[...]).astype(ckv_ref.dtype)
    o += KV_RANK
    lo = p_ref[:, o:o + LANES]
    hi = p_ref[:, o + LANES:o + 2 * LANES]
    lane = lax.broadcasted_iota(jnp.int32, lo.shape, 1)
    kr_ref[...] = _rope_head(jnp.where(lane < ROT_DIM, lo, 0.0), c, sa, sb)
    k_i = jnp.where(lane < LANES - ROT_DIM, pltpu.roll(lo, LANES - ROT_DIM, 1),
                    pltpu.roll(hi, LANES - ROT_DIM, 1))
    ki_ref[...] = _rope_head(norm(k_i, kin_ref[...]), c, sa, sb).astype(ki_ref.dtype)
    wi_ref[...] = hi * (IDX_HEADS ** -0.5)


def _odd_prep(p_small, q_norm, kv_norm, k_idx_norm, tables, L, tm=256):
    m = p_small.shape[0]
    tps = L // tm
    row = lambda w: pl.BlockSpec((tm, w), lambda i: (i, 0))
    tab = pl.BlockSpec((tm, 128), lambda i: (i % tps, 0))
    vec = lambda w: pl.BlockSpec((1, w), lambda i: (0, 0))
    return pl.pallas_call(
        _odd_prep_kernel,
        out_shape=[jax.ShapeDtypeStruct((m, Q_RANK), BF16),
                   jax.ShapeDtypeStruct((m, KV_RANK), BF16),
                   jax.ShapeDtypeStruct((m, 128), F32),
                   jax.ShapeDtypeStruct((m, 128), BF16),
                   jax.ShapeDtypeStruct((m, 128), F32)],
        grid=(m // tm,),
        in_specs=[row(ODD_SMALL), vec(Q_RANK), vec(KV_RANK), vec(IDX_DIM), tab, tab, tab],
        out_specs=[row(Q_RANK), row(KV_RANK), row(128), row(128), row(128)],
        compiler_params=_params(1, 32),
        name="odd_prep",
    )(p_small, q_norm.reshape(1, -1), kv_norm.reshape(1, -1), k_idx_norm.reshape(1, -1),
      *tables)


def _float_sort_key(s):
    b = pltpu.bitcast(s, jnp.int32)
    return b ^ ((b >> 31) & 0x7FFFFFFF)


def _indexer_kernel(q_ref, k_ref, w_ref, bias_ref, key_ref, jmax_ref, *, L, topk, kc, w_lane0):
    i = pl.program_id(1)
    nch = L // kc
    rows = IDX_ROWS
    n_vis = ((i + 1) * rows + kc - 1) // kc
    qpos = i * rows + lax.broadcasted_iota(jnp.int32, (rows, 1), 0)
    lane = lax.broadcasted_iota(jnp.int32, (rows, kc), 1)
    lane_t = lax.broadcasted_iota(jnp.int32, (Q_TILE, kc), 1)
    qpos_t = lax.broadcasted_iota(jnp.int32, (Q_TILE, 1), 0)

    def chunk(c, _):
        k0 = pl.multiple_of(c * kc, kc)
        kblk = k_ref[0, pl.ds(k0, kc), :]
        for t in range(rows // Q_TILE):
            rs = slice(t * Q_TILE, (t + 1) * Q_TILE)
            acc = jnp.zeros((Q_TILE, kc), F32)
            for h in range(IDX_HEADS):
                qh = q_ref[0, rs, h * IDX_DIM:(h + 1) * IDX_DIM]
                s = lax.dot_general(qh, kblk, (((1,), (1,)), ((), ())),
                                    preferred_element_type=F32)
                acc = acc + w_ref[0, rs, w_lane0 + h:w_lane0 + h + 1] * jnp.maximum(s, 0.0)
            acc = jnp.where(acc == 0.0, 0.0, acc)
            visible = lane_t + k0 <= qpos_t + (i * rows + t * Q_TILE)
            key_ref[c, rs, :] = jnp.where(visible, _float_sort_key(acc), INT_MIN)
        return 0

    keep_all = (i + 1) * rows <= topk
    lax.fori_loop(0, jnp.where(keep_all, 0, n_vis), chunk, 0)

    @pl.when(keep_all)
    def _():
        for c in range(nch):
            bias = jnp.where(lane + c * kc <= qpos, 0.0, MASK_NEG)
            for t in range(rows // Q_TILE):
                bias_ref[0, t, c] = bias[t * Q_TILE:(t + 1) * Q_TILE].astype(bias_ref.dtype)

    def select(nv):
        def count(pred):
            tot = jnp.zeros((rows, LANES), F32)
            for c in range(nv):
                hit = jnp.where(pred(key_ref[c], c), 1.0, 0.0)
                for j in range(kc // LANES):
                    tot = tot + hit[:, j * LANES:(j + 1) * LANES]
            return jnp.sum(tot, axis=1, keepdims=True)

        def tbit(b, t_u):
            cand_u = t_u | jnp.left_shift(jnp.int32(1), 31 - b)
            cand_s = cand_u ^ INT_MIN
            cnt = count(lambda key, c: key >= cand_s)
            return jnp.where(cnt >= topk, cand_u, t_u)

        t_u = lax.fori_loop(0, 32, tbit, jnp.zeros((rows, 1), jnp.int32))
        thr = t_u ^ INT_MIN

        need = topk - count(lambda key, c: key > thr)
        n_eq = count(lambda key, c: key == thr)
        jbits = max(1, (L - 1).bit_length())
        jmax_ref[...] = jnp.full((rows, 1), (1 << jbits) - 1, jnp.int32)

        unsettled = jnp.where(thr == INT_MIN, 0.0, jnp.abs(n_eq - need))

        @pl.when(jnp.max(unsettled) > 0.0)
        def _():
            def jbit(b, j):
                cand = j | jnp.left_shift(jnp.int32(1), jbits - 1 - b)
                cnt = count(lambda key, c: jnp.where(lane + c * kc < cand, key, thr - 1) == thr)
                return jnp.where(cnt < need, cand, j)

            jmax_ref[...] = lax.fori_loop(0, jbits, jbit, jnp.zeros((rows, 1), jnp.int32))

        jlim = jnp.minimum(jmax_ref[...], qpos)
        for c in range(nv):
            key = key_ref[c]
            tie = jnp.where(lane + c * kc <= jlim, 0.0, MASK_NEG)
            bias = jnp.where(key > thr, 0.0, jnp.where(key == thr, tie, MASK_NEG))
            for t in range(rows // Q_TILE):
                bias_ref[0, t, c] = bias[t * Q_TILE:(t + 1) * Q_TILE].astype(bias_ref.dtype)
        for c in range(nv, nch):
            for t in range(rows // Q_TILE):
                bias_ref[0, t, c] = jnp.full((Q_TILE, kc), MASK_NEG, bias_ref.dtype)

    for nv in range(1, nch + 1):
        pl.when(jnp.logical_and(jnp.logical_not(keep_all), n_vis == nv))(
            functools.partial(select, nv))


def _indexer(q_idx, k_idx, w_idx, topk, kc, w_lane0):
    bsz, L, _ = q_idx.shape
    nq, nch = L // Q_TILE, L // kc
    rows = IDX_ROWS
    assert L % rows == 0 and rows % Q_TILE == 0
    return pl.pallas_call(
        functools.partial(_indexer_kernel, L=L, topk=topk, kc=kc, w_lane0=w_lane0),
        out_shape=jax.ShapeDtypeStruct((bsz, nq, nch, Q_TILE, kc), BF16),
        grid=(bsz, L // rows),
        in_specs=[pl.BlockSpec((1, rows, IDX_HEADS * IDX_DIM), lambda b, i: (b, i, 0)),
                  pl.BlockSpec((1, L, IDX_DIM), lambda b, i: (b, 0, 0)),
                  pl.BlockSpec((1, rows, LANES), lambda b, i: (b, i, 0))],
        out_specs=pl.BlockSpec((1, rows // Q_TILE, nch, Q_TILE, kc),
                               lambda b, i: (b, i, 0, 0, 0)),
        scratch_shapes=[pltpu.VMEM((nch, rows, kc), jnp.int32),
                        pltpu.VMEM((rows, 1), jnp.int32)],
        compiler_params=_params(2, 32),
        name="dsa_indexer",
    )(q_idx, k_idx, w_idx)


def _mla_kernel(q_ref, ckv_ref, kr_ref, wk_ref, wv_ref, bias_ref, g_ref, o_ref,
                k_ref, v_ref, s_ref, mx_ref, acc_ref, *, kc, nh):
    i = pl.program_id(2)
    rows = ATT_ROWS
    n_vis = ((i + 1) * rows + kc - 1) // kc
    slabs = kc // LANES

    @pl.when(i == 0)
    def _():
        ckv = ckv_ref[0]
        kr = kr_ref[0]
        k_all = jnp.dot(ckv, wk_ref[...].astype(BF16), preferred_element_type=F32)
        for h in range(nh):
            hs = slice(h * HEAD_DIM, (h + 1) * HEAD_DIM)
            k_ref[:, hs] = (k_all[:, hs] + kr).astype(k_ref.dtype)
        v_ref[...] = jnp.dot(ckv, wv_ref[...].astype(BF16),
                             preferred_element_type=F32).astype(v_ref.dtype)

    def fold(x, op):
        t = x[:, 0:LANES]
        for j in range(1, slabs):
            t = op(t, x[:, j * LANES:(j + 1) * LANES])
        return t

    mx_ref[...] = jnp.full(mx_ref.shape, MASK_NEG, F32)

    def scores(c, _):
        k0 = pl.multiple_of(c * kc, kc)
        bias = jnp.concatenate([bias_ref[0, t, c] for t in range(rows // Q_TILE)],
                               axis=0).astype(F32)
        raw = [lax.dot_general(q_ref[0, :, h * HEAD_DIM:(h + 1) * HEAD_DIM],
                               k_ref[pl.ds(k0, kc), h * HEAD_DIM:(h + 1) * HEAD_DIM],
                               (((1,), (1,)), ((), ())), preferred_element_type=F32)
               for h in range(nh)]
        for h in range(nh):
            s = raw[h] + bias
            s_ref[c, h] = s
            mx_ref[h] = jnp.maximum(mx_ref[h], fold(s, jnp.maximum))
        return 0

    lax.fori_loop(0, n_vis, scores, 0)

    for h in range(nh):
        m = jnp.max(mx_ref[h], axis=1, keepdims=True)
        mx_ref[h] = jnp.broadcast_to(m, (rows, LANES))
    acc_ref[...] = jnp.zeros(acc_ref.shape, F32)

    ones = jnp.ones((kc, LANES), BF16)

    def weights(c, _):
        k0 = pl.multiple_of(c * kc, kc)
        for h in range(nh):
            hs = slice(h * HEAD_DIM, (h + 1) * HEAD_DIM)
            m = mx_ref[h]
            s = s_ref[c, h]
            p = jnp.concatenate(
                [jnp.exp2(s[:, j * LANES:(j + 1) * LANES] - m) for j in range(slabs)], axis=1)
            v1 = jnp.concatenate([v_ref[pl.ds(k0, kc), hs], ones], axis=1)
            acc_ref[h] += jnp.dot(p.astype(BF16), v1, preferred_element_type=F32)
        return 0

    lax.fori_loop(0, n_vis, weights, 0)

    for h in range(nh):
        hs = slice(h * HEAD_DIM, (h + 1) * HEAD_DIM)
        g = g_ref[0, :, hs].astype(F32)
        o = acc_ref[h, :, 0:HEAD_DIM] / acc_ref[h, :, HEAD_DIM:HEAD_DIM + LANES]
        o_ref[0, :, hs] = (o * (g / (1.0 + jnp.exp(-g)))).astype(o_ref.dtype)


def _masked_attention(q, c_kv, k_rope, w_k, w_v, bias, gate, kc, nh=ATT_HEADS):
    bsz, L, _ = q.shape
    rows, nch = ATT_ROWS, L // kc
    assert L % rows == 0 and rows % Q_TILE == 0
    w = nh * HEAD_DIM
    qspec = pl.BlockSpec((1, rows, w), lambda b, g, i: (b, i, g))
    wspec = pl.BlockSpec((KV_RANK, w), lambda b, g, i: (0, g))
    return pl.pallas_call(
        functools.partial(_mla_kernel, kc=kc, nh=nh),
        out_shape=jax.ShapeDtypeStruct((bsz, L, C_WIDTH), BF16),
        grid=(bsz, MLA_HEADS // nh, L // rows),
        in_specs=[qspec,
                  pl.BlockSpec((1, L, KV_RANK), lambda b, g, i: (b, 0, 0)),
                  pl.BlockSpec((1, L, HEAD_DIM), lambda b, g, i: (b, 0, 0)),
                  wspec, wspec,
                  pl.BlockSpec((1, rows // Q_TILE, nch, Q_TILE, kc),
                               lambda b, g, i: (b, i, 0, 0, 0)),
                  qspec],
        out_specs=qspec,
        scratch_shapes=[pltpu.VMEM((L, w), BF16),
                        pltpu.VMEM((L, w), BF16),
                        pltpu.VMEM((nch, nh, rows, kc), F32),
                        pltpu.VMEM((nh, rows, LANES), F32),
                        pltpu.VMEM((nh, rows, HEAD_DIM + LANES), F32)],
        compiler_params=_params(3, 48),
        name="dsa_attention",
    )(q, c_kv, k_rope, w_k, w_v, bias, gate)


def _odd_layer(x2d, bsz, L, norm, w_in, q_norm, kv_norm, k_idx_norm, w_uq, w_uk, w_uv, w_iq,
               w_out, tables, h=None):
    m = bsz * L
    if h is None:
        h = _rmsnorm(x2d, norm, BF16)

    w_in_t = w_in.T
    p_small = _matmul([h], [w_in_t], ODD_SMALL, F32, w_transposed=True, tn=256,
                      name="odd_in_small")
    gate = _matmul([h], [w_in_t], C_WIDTH, BF16, w_transposed=True, w_col0=ODD_GATE0,
                   name="odd_in_gate")
    c_q, c_kv, k_rope, k_idx, w_idx = _odd_prep(p_small, q_norm, kv_norm, k_idx_norm, tables, L)

    q = _matmul([c_q], [w_uq], C_WIDTH, BF16, epilogue="rope", extra=tables, seq_len=L,
                scale=LOG2E * HEAD_DIM ** -0.5, tn=2 * ROPE_SUBTILE, name="odd_q_up")
    q_idx = _matmul([c_q], [w_iq], IDX_HEADS * IDX_DIM, BF16, epilogue="rope", extra=tables,
                    seq_len=L, scale=IDX_DIM ** -0.5, tn=2 * ROPE_SUBTILE, name="odd_q_idx")

    w_k = jnp.pad(w_uk, ((0, 0), (0, 0), (ROT_DIM, 0))).reshape(KV_RANK, C_WIDTH)
    w_v = w_uv.reshape(KV_RANK, C_WIDTH)

    topk = min(IDX_TOPK_MAX, L // 4)
    kc = min(512, L)
    bias = _indexer(q_idx.reshape(bsz, L, -1), k_idx.reshape(bsz, L, IDX_DIM),
                    w_idx.reshape(bsz, L, LANES), topk, kc, IDX_W_LANE0)
    o = _masked_attention(q.reshape(bsz, L, C_WIDTH), c_kv.reshape(bsz, L, KV_RANK),
                          k_rope.reshape(bsz, L, HEAD_DIM), w_k, w_v, bias,
                          gate.reshape(bsz, L, C_WIDTH), kc)
    return _matmul([o.reshape(m, C_WIDTH)], [w_out], D_MODEL, F32, res=x2d, name="odd_out")


def kernel(x, even_norm, even_w_in, s5_lambda_re, s5_lambda_im, s5_log_step, s5_b_re, s5_b_im,
           s5_c_re, s5_c_im, s5_d, s5_glu_w, s5_glu_b, even_w_out, odd_norm, odd_w_in,
           mla_q_norm, mla_kv_norm, idx_k_norm, mla_w_uq, mla_w_uk, mla_w_uv, idx_w_q,
           odd_w_out, final_norm):
    bsz, L, d = x.shape
    depth = even_norm.shape[0] + odd_norm.shape[0]
    tables = _rope_tables(L)
    x2d = x.reshape(bsz * L, d)
    h = None
    for layer in range(depth):
        i = layer // 2
        if layer % 2 == 0:
            next_norm = odd_norm[i] if layer + 1 < depth else None
            x2d, h = _even_layer(x2d, bsz, L, even_norm[i], even_w_in[i], s5_lambda_re[i],
                                 s5_lambda_im[i], s5_log_step[i], s5_b_re[i], s5_b_im[i],
                                 s5_c_re[i], s5_c_im[i], s5_d[i], s5_glu_w[i], s5_glu_b[i],
                                 even_w_out[i], tables, h=h, next_norm=next_norm)
        else:
            x2d = _odd_layer(x2d, bsz, L, odd_norm[i], odd_w_in[i], mla_q_norm[i],
                             mla_kv_norm[i], idx_k_norm[i], mla_w_uq[i], mla_w_uk[i],
                             mla_w_uv[i], idx_w_q[i], odd_w_out[i], tables, h=h)
            h = None
    return _rmsnorm(x2d, final_norm, F32).reshape(bsz, L, d)
```

```python
import functools
import math

import jax
import jax.numpy as jnp
from jax import lax
from jax.experimental import pallas as pl
from jax.experimental.pallas import tpu as pltpu

F32 = jnp.float32
BF16 = jnp.bfloat16

LANES = 128
SUBLANES = 8

D_MODEL = 4096
EPS = 1e-6
ROPE_THETA = 500000.0
HEAD_DIM = 128
ROT_DIM = HEAD_DIM // 4
ROT_HALF = ROT_DIM // 2

S5_WIDTH = D_MODEL // 4
S5_GROUP = 16
S5_GROUPS = S5_WIDTH // S5_GROUP
S5_STATE = 64
S5_BLOCKS = S5_WIDTH // 128
S5_BLOCK_GROUPS = S5_GROUPS // S5_BLOCKS
S5_BLOCK_STATES = S5_BLOCK_GROUPS * S5_STATE

DIL_PATTERNS = ((128, 1), (512, 4), (2048, 16))
N_DIL = len(DIL_PATTERNS)
DIL_HEADS = D_MODEL // 512
DIL_WIDTH = DIL_HEADS * HEAD_DIM
QKV_WIDTH = 3 * N_DIL * DIL_WIDTH

MLA_HEADS = D_MODEL // HEAD_DIM
Q_RANK = 1536
KV_RANK = 512
IDX_HEADS = D_MODEL // 128
IDX_DIM = 128
IDX_TOPK_MAX = 256
C_WIDTH = MLA_HEADS * HEAD_DIM
ODD_GATE0 = Q_RANK + KV_RANK + ROT_DIM + IDX_DIM + IDX_HEADS
ODD_SMALL = Q_RANK + KV_RANK + 2 * LANES
IDX_W_LANE0 = ROT_DIM + IDX_DIM - LANES

MM_TILE_N = 512
MM_ROWS_DEEP = 1024
MM_ROWS_SHALLOW = 2048
Q_TILE = 128
ROPE_SUBTILE = 512
ATT_ROWS = 256
ATT_HEADS = 8
IDX_ROWS = 256
MASK_NEG = -1e30
LOG2E = 1.4426950408889634
INT_MIN = -(2 ** 31)


def _params(n_grid, vmem_mb):
    return pltpu.CompilerParams(
        dimension_semantics=("arbitrary",) * n_grid,
        vmem_limit_bytes=vmem_mb * 1024 * 1024)


def _rope_tables(L):
    inv = ROPE_THETA ** (-jnp.arange(0, ROT_DIM, 2, dtype=F32) / ROT_DIM)
    ang = jnp.arange(L, dtype=F32)[:, None] * inv[None, :]
    cos, sin = jnp.cos(ang), jnp.sin(ang)
    ones = jnp.ones((L, HEAD_DIM - ROT_DIM), F32)
    zeros = jnp.zeros((L, HEAD_DIM - ROT_DIM), F32)
    zh = jnp.zeros((L, ROT_HALF), F32)
    c = jnp.concatenate([cos, cos, ones], axis=1)
    sa = jnp.concatenate([-sin, zh, zeros], axis=1)
    sb = jnp.concatenate([zh, sin, zeros], axis=1)
    return c, sa, sb


def _rope_swap_tables(tables, width):
    c, sa, sb = tables
    col = jnp.arange(width)
    lane, head = col % HEAD_DIM, col // HEAD_DIM
    src = jnp.where(lane < ROT_HALF, col + ROT_HALF, col - ROT_HALF)
    sign = jnp.where(lane < ROT_HALF, -1.0, 1.0)
    perm = jnp.where((col[:, None] == src[None, :]) & (lane[None, :] < ROT_DIM)
                     & (head[:, None] == head[None, :]), sign[None, :], 0.0).astype(BF16)
    return c, sb - sa, perm


def _rope_head(x, c, sa, sb):
    return (x * c + pltpu.roll(x, HEAD_DIM - ROT_HALF, 1) * sa
            + pltpu.roll(x, ROT_HALF, 1) * sb)


def _rmsnorm_kernel(x_ref, g_ref, o_ref):
    x = x_ref[...]
    ms = jnp.mean(x * x, axis=-1, keepdims=True)
    o_ref[...] = (x * lax.rsqrt(ms + EPS) * g_ref[...]).astype(o_ref.dtype)


def _rmsnorm(x, g, out_dtype, tm=256):
    m, d = x.shape
    return pl.pallas_call(
        _rmsnorm_kernel,
        out_shape=jax.ShapeDtypeStruct((m, d), out_dtype),
        grid=(m // tm,),
        in_specs=[pl.BlockSpec((tm, d), lambda i: (i, 0)),
                  pl.BlockSpec((1, d), lambda i: (0, 0))],
        out_specs=pl.BlockSpec((tm, d), lambda i: (i, 0)),
        compiler_params=_params(1, 32),
        name="rmsnorm",
    )(x, g.reshape(1, d))


def _mm_kernel(*refs, n_a, has_res, epilogue, scale, tiles_per_kind, qkv_tiles, w_transposed):
    a_refs = refs[:n_a]
    w_refs = refs[n_a:2 * n_a]
    pos = 2 * n_a
    res_ref = None
    if has_res:
        res_ref = refs[pos]
        pos += 1
    extra = refs[pos:-1]
    o_ref = refs[-1]

    if epilogue == "rope":
        (a_ref,), (w_ref,) = a_refs, w_refs
        c_ref, s_ref, perm_ref = extra
        a = a_ref[...]
        for c0 in range(0, o_ref.shape[1], ROPE_SUBTILE):
            sub = jnp.dot(a, w_ref[:, c0:c0 + ROPE_SUBTILE].astype(BF16),
                          preferred_element_type=F32)
            for h0 in range(0, ROPE_SUBTILE, HEAD_DIM):
                x = sub[:, h0:h0 + HEAD_DIM]
                swapped = jnp.dot(x.astype(BF16), perm_ref[...], preferred_element_type=F32)
                y = (x * c_ref[...] + swapped * s_ref[...]) * scale
                o_ref[:, c0 + h0:c0 + h0 + HEAD_DIM] = y.astype(o_ref.dtype)
        return

    acc = None
    for a_ref, w_ref in zip(a_refs, w_refs):
        contract = (((1,), (1 if w_transposed else 0,)), ((), ()))
        d = lax.dot_general(a_ref[...], w_ref[...].astype(BF16), contract,
                            preferred_element_type=F32)
        acc = d if acc is None else acc + d
    if has_res:
        acc = acc + res_ref[...]
    n_heads = acc.shape[1] // HEAD_DIM

    def store_roped():
        c_ref, s_ref, perm_ref = extra
        for h in range(n_heads):
            hs = slice(h * HEAD_DIM, (h + 1) * HEAD_DIM)
            swapped = jnp.dot(acc[:, hs].astype(BF16), perm_ref[...],
                              preferred_element_type=F32)
            y = acc[:, hs] * c_ref[...] + swapped * s_ref[...]
            o_ref[:, hs] = (y * scale if scale != 1.0 else y).astype(o_ref.dtype)

    if epilogue == "none":
        o_ref[...] = acc.astype(o_ref.dtype)
    elif epilogue == "rope_qk":
        first, last = qkv_tiles
        j = pl.program_id(1)
        rotary = (j >= first) & (j < last) & (((j - first) // tiles_per_kind) % 3 != 2)

        @pl.when(jnp.logical_not(rotary))
        def _():
            o_ref[...] = acc.astype(o_ref.dtype)

        pl.when(rotary)(store_roped)
    else:
        raise ValueError(epilogue)


def _matmul(a_list, w_list, n, out_dtype, *, w_col0=0, w_row_blocks=None, res=None,
            epilogue="none", extra=(), scale=1.0, seq_len=None, w_transposed=False,
            qkv_cols=(0, 0), tn=MM_TILE_N, vmem_mb=52, name="matmul"):
    m = a_list[0].shape[0]
    deep = sum(a.shape[1] for a in a_list) >= D_MODEL
    tm = min(MM_ROWS_DEEP if deep else MM_ROWS_SHALLOW, m)
    assert m % tm == 0 and n % tn == 0
    n_a = len(a_list)
    if w_row_blocks is None:
        w_row_blocks = [0] * n_a
    in_specs, args = [], []
    for a in a_list:
        k = a.shape[1]
        mode = {} if deep else {"pipeline_mode": pl.Buffered(1)}
        in_specs.append(pl.BlockSpec((tm, k), lambda i, j: (i, 0), **mode))
        args.append(a)
    for a, w, rb in zip(a_list, w_list, w_row_blocks):
        k = a.shape[1]
        if not w_transposed:
            assert w_col0 % tn == 0
            in_specs.append(pl.BlockSpec((k, tn),
                                         lambda i, j, rb=rb: (rb, j + w_col0 // tn)))
        elif w_col0 % tn == 0:
            in_specs.append(pl.BlockSpec((tn, k), lambda i, j: (j + w_col0 // tn, 0)))
        else:
            g = math.gcd(w_col0, tn)
            assert g % SUBLANES == 0
            in_specs.append(pl.BlockSpec(
                (pl.Element(tn), pl.Element(k)),
                lambda i, j, g=g: ((w_col0 // g + j * (tn // g)) * g, 0)))
        args.append(w)
    if res is not None:
        in_specs.append(pl.BlockSpec((tm, tn), lambda i, j: (i, j)))
        args.append(res)
    tiles_per_seq = None if seq_len is None else seq_len // tm
    if epilogue in ("rope", "rope_qk"):
        extra = _rope_swap_tables(extra, HEAD_DIM)
    for e in extra:
        if e.shape == (HEAD_DIM, HEAD_DIM):
            in_specs.append(pl.BlockSpec((HEAD_DIM, HEAD_DIM), lambda i, j: (0, 0)))
        else:
            assert seq_len is not None and e.shape[0] == seq_len and seq_len % tm == 0
            in_specs.append(pl.BlockSpec((tm, e.shape[1]),
                                         lambda i, j: (i % tiles_per_seq, 0)))
        args.append(e)
    kern = functools.partial(_mm_kernel, n_a=n_a, has_res=res is not None,
                             epilogue=epilogue, scale=scale,
                             tiles_per_kind=DIL_WIDTH // tn,
                             qkv_tiles=(qkv_cols[0] // tn, qkv_cols[1] // tn),
                             w_transposed=w_transposed)
    return pl.pallas_call(
        kern,
        out_shape=jax.ShapeDtypeStruct((m, n), out_dtype),
        grid=(m // tm, n // tn),
        in_specs=in_specs,
        out_specs=pl.BlockSpec((tm, tn), lambda i, j: (i, j)),
        compiler_params=_params(2, vmem_mb),
        name=name,
    )(*args)


S5_UNROLL = 4


def _s5_discretise(lam_re, lam_im, log_step, b_re, b_im, c_re, c_im, seg_len):
    lr = jnp.minimum(lam_re.astype(F32), -1e-4)
    li = lam_im.astype(F32)
    dt = jnp.exp(log_step.astype(F32))[:, None]
    mag = jnp.exp(lr * dt)
    ar = mag * jnp.cos(li * dt)
    ai = mag * jnp.sin(li * dt)
    den = lr * lr + li * li
    nr, ni = ar - 1.0, ai
    cr = (nr * lr + ni * li) / den
    ci = (ni * lr - nr * li) / den
    br32, bi32 = b_re.astype(F32), b_im.astype(F32)
    bbr = cr[..., None] * br32 - ci[..., None] * bi32
    bbi = cr[..., None] * bi32 + ci[..., None] * br32

    eye = jnp.eye(S5_BLOCK_GROUPS, dtype=F32)

    def in_blockdiag(t):
        t = t.reshape(S5_BLOCKS, S5_BLOCK_GROUPS, S5_STATE, S5_GROUP)
        t = jnp.einsum("bgpj,gh->bgjhp", t, eye)
        return t.reshape(S5_BLOCKS, 128, S5_BLOCK_STATES)

    def out_blockdiag(t):
        t = t.reshape(S5_BLOCKS, S5_BLOCK_GROUPS, S5_GROUP, S5_STATE)
        t = jnp.einsum("bgjp,gh->bgphj", t, eye)
        return t.reshape(S5_BLOCKS, S5_BLOCK_STATES, 128)

    b_mat = jnp.concatenate([in_blockdiag(bbr), in_blockdiag(bbi)], axis=2).astype(BF16)
    c_mat = jnp.concatenate([out_blockdiag(c_re.astype(F32)),
                             -out_blockdiag(c_im.astype(F32))], axis=1).astype(BF16)

    a_r = ar.reshape(S5_BLOCKS, S5_BLOCK_STATES)
    a_i = ai.reshape(S5_BLOCKS, S5_BLOCK_STATES)
    s_r, s_i = jnp.ones_like(a_r), jnp.zeros_like(a_i)
    b_r, b_i, e = a_r, a_i, seg_len
    while e:
        if e & 1:
            s_r, s_i = s_r * b_r - s_i * b_i, s_r * b_i + s_i * b_r
        b_r, b_i = b_r * b_r - b_i * b_i, 2.0 * b_r * b_i
        e >>= 1
    pad = jnp.zeros((S5_BLOCKS, SUBLANES - 2, S5_BLOCK_STATES), F32)
    pow_re = jnp.concatenate([a_r[:, None], s_r[:, None], pad], axis=1)
    pow_im = jnp.concatenate([a_i[:, None], s_i[:, None], pad], axis=1)
    return b_mat, c_mat, pow_re, pow_im


def _s5_kernel(u_ref, b_ref, c_ref, pre_ref, pim_ref, d_ref, y_ref, up_ref, s_ref, yp_ref):
    L = u_ref.shape[1]
    ns = S5_BLOCK_STATES
    seg = L // SUBLANES

    def gather(r, _):
        up_ref[pl.ds(pl.multiple_of(r * SUBLANES, SUBLANES), SUBLANES), :] = (
            u_ref[0, pl.ds(r, SUBLANES, stride=seg), :])
        return 0

    lax.fori_loop(0, seg, gather, 0, unroll=S5_UNROLL)
    u = up_ref[...]
    s_ref[...] = jnp.dot(u.astype(BF16), b_ref[0], preferred_element_type=F32)

    pre, pim = pre_ref[0], pim_ref[0]
    ar = jnp.broadcast_to(pre[0:1, :], (SUBLANES, ns))
    ai = jnp.broadcast_to(pim[0:1, :], (SUBLANES, ns))

    def load(r, half):
        return s_ref[pl.ds(pl.multiple_of(r * SUBLANES, SUBLANES), SUBLANES),
                     half * ns:(half + 1) * ns]

    def store(r, half, x):
        s_ref[pl.ds(pl.multiple_of(r * SUBLANES, SUBLANES), SUBLANES),
              half * ns:(half + 1) * ns] = x

    def local(r, carry):
        sr, si = carry
        sr, si = ar * sr - ai * si + load(r, 0), ar * si + ai * sr + load(r, 1)
        store(r, 0, sr)
        store(r, 1, si)
        return sr, si

    zeros = jnp.zeros((SUBLANES, ns), F32)
    er, ei = lax.fori_loop(0, seg, local, (zeros, zeros), unroll=S5_UNROLL)

    gr, gi = pre[1:2, :], pim[1:2, :]
    cr, ci = [jnp.zeros((1, ns), F32)], [jnp.zeros((1, ns), F32)]
    for j in range(1, SUBLANES):
        pr, pi = cr[-1], ci[-1]
        cr.append(er[j - 1:j, :] + gr * pr - gi * pi)
        ci.append(ei[j - 1:j, :] + gr * pi + gi * pr)
    dr, di = jnp.concatenate(cr, axis=0), jnp.concatenate(ci, axis=0)

    def carried(r, carry):
        dr, di = carry
        dr, di = ar * dr - ai * di, ar * di + ai * dr
        store(r, 0, load(r, 0) + dr)
        store(r, 1, load(r, 1) + di)
        return dr, di

    lax.fori_loop(0, seg, carried, (dr, di), unroll=S5_UNROLL)

    y = jnp.dot(s_ref[...].astype(BF16), c_ref[0], preferred_element_type=F32)
    yp_ref[...] = y + d_ref[0] * u
    for j in range(SUBLANES):
        y_ref[0, j * seg:(j + 1) * seg, :] = yp_ref[pl.ds(j, seg, stride=SUBLANES), :]


def _s5(proj_a, b_mat, c_mat, pow_re, pow_im, d_skip):
    bsz, L, _ = proj_a.shape
    ns = S5_BLOCK_STATES
    return pl.pallas_call(
        _s5_kernel,
        out_shape=jax.ShapeDtypeStruct((bsz, L, S5_WIDTH), F32),
        grid=(bsz, S5_BLOCKS),
        in_specs=[
            pl.BlockSpec((1, L, 128), lambda b, g: (b, 0, g)),
            pl.BlockSpec((1, 128, 2 * ns), lambda b, g: (g, 0, 0)),
            pl.BlockSpec((1, 2 * ns, 128), lambda b, g: (g, 0, 0)),
            pl.BlockSpec((1, SUBLANES, ns), lambda b, g: (g, 0, 0)),
            pl.BlockSpec((1, SUBLANES, ns), lambda b, g: (g, 0, 0)),
            pl.BlockSpec((1, 1, 128), lambda b, g: (g, 0, 0)),
        ],
        out_specs=pl.BlockSpec((1, L, 128), lambda b, g: (b, 0, g)),
        scratch_shapes=[pltpu.VMEM((L, LANES), F32),
                        pltpu.VMEM((L, 2 * ns), F32),
                        pltpu.VMEM((L, LANES), F32)],
        compiler_params=_params(2, 40),
        name="s5_scan",
    )(proj_a, b_mat, c_mat, pow_re, pow_im, d_skip.reshape(S5_BLOCKS, 1, 128))


def _glu_kernel(y_ref, z_ref, w_ref, b_ref, o_ref):
    g = jax.nn.gelu(y_ref[...])
    t = jnp.dot(g.astype(BF16), w_ref[...].astype(BF16), preferred_element_type=F32) + b_ref[...]
    z = z_ref[...]
    gate = z / (1.0 + jnp.exp(-z))
    o_ref[...] = (g / (1.0 + jnp.exp(-t)) * gate).astype(o_ref.dtype)


def _glu(y, proj_a, glu_w, glu_b, tm=512):
    m = y.shape[0]
    return pl.pallas_call(
        _glu_kernel,
        out_shape=jax.ShapeDtypeStruct((m, S5_WIDTH), BF16),
        grid=(m // tm,),
        in_specs=[pl.BlockSpec((tm, S5_WIDTH), lambda i: (i, 0)),
                  pl.BlockSpec((tm, S5_WIDTH), lambda i: (i, 1)),
                  pl.BlockSpec((S5_WIDTH, S5_WIDTH), lambda i: (0, 0)),
                  pl.BlockSpec((1, S5_WIDTH), lambda i: (0, 0))],
        out_specs=pl.BlockSpec((tm, S5_WIDTH), lambda i: (i, 0)),
        compiler_params=_params(1, 40),
        name="s5_glu",
    )(y, proj_a, glu_w, glu_b.reshape(1, S5_WIDTH))


DIL_UNROLL = 16
MERGE_ROWS = 256


def _dil_kernel(*refs, L):
    qkv_refs = refs[:3 * N_DIL]
    z_ref, out_ref, og_ref, lg_ref, cls_ref = refs[3 * N_DIL:]
    n_tiles = L // Q_TILE

    for g, (window, dil) in enumerate(DIL_PATTERNS):
        n = L // dil
        nb = n // Q_TILE
        cw = Q_TILE if nb == 1 else 2 * Q_TILE
        half = (window // dil) // 2

        def rows(start, size, dil=dil):
            return pl.ds(start, size) if dil == 1 else pl.ds(start, size, stride=dil)

        for kind in range(3):
            for r in range(dil):
                cls_ref[kind, r * n:(r + 1) * n, :] = (
                    qkv_refs[3 * g + kind][0, rows(r, n), :].astype(BF16))

        def scores(it, n=n, nb=nb, cw=cw, half=half, dil=dil):
            r = it // nb
            qb = it % nb
            kb = jnp.clip(qb - 1, 0, max(nb - 2, 0))
            qc = pl.multiple_of(r * n + qb * Q_TILE, Q_TILE)
            kc = pl.multiple_of(r * n + kb * Q_TILE, Q_TILE)
            qpos = qb * Q_TILE + lax.broadcasted_iota(jnp.int32, (Q_TILE, cw), 0)
            kpos = kb * Q_TILE + lax.broadcasted_iota(jnp.int32, (Q_TILE, cw), 1)
            valid = jnp.abs(qpos - kpos - half) <= half
            s = lax.dot_general(cls_ref[0, pl.ds(qc, Q_TILE), :], cls_ref[1, pl.ds(kc, cw), :],
                                (((1,), (1,)), ((), ())),
                                preferred_element_type=F32) * (HEAD_DIM ** -0.5)
            return jnp.where(valid, s, MASK_NEG), kc, r + qb * (Q_TILE * dil)

        def softmax(s):
            m = jnp.max(s, axis=1, keepdims=True)
            p = jnp.exp(s - m)
            l = jnp.sum(p, axis=1, keepdims=True)
            return p.astype(BF16), l, m + jnp.log(l)

        def unrolled(j, _, g=g, cw=cw, rows=rows, scores=scores, softmax=softmax):
            blocks = [scores(j * DIL_UNROLL + u) for u in range(DIL_UNROLL)]
            probs = [softmax(s) for s, _, _ in blocks]
            for (_, kc, q0), (p, l, lse) in zip(blocks, probs):
                o = jnp.dot(p, cls_ref[2, pl.ds(kc, cw), :], preferred_element_type=F32) / l
                og_ref[g, rows(q0, Q_TILE), :] = o
                lg_ref[g, rows(q0, Q_TILE), :] = jnp.broadcast_to(lse, (Q_TILE, HEAD_DIM))
            return 0

        lax.fori_loop(0, n_tiles // DIL_UNROLL, unrolled, 0)

    def merge(j, _):
        rs = pl.ds(pl.multiple_of(j * MERGE_ROWS, MERGE_ROWS), MERGE_ROWS)
        a = [lg_ref[g, rs, :] for g in range(N_DIL)]
        m = functools.reduce(jnp.maximum, a)
        e = [jnp.exp(x - m) for x in a]
        o = sum(e[g] * og_ref[g, rs, :] for g in range(N_DIL)) / sum(e)
        z = z_ref[0, rs, :]
        out_ref[0, rs, :] = (o * (z / (1.0 + jnp.exp(-z)))).astype(out_ref.dtype)
        return 0

    lax.fori_loop(0, L // MERGE_ROWS, merge, 0)


def _dilated_attention(proj, qkv_col0, gate_col0):
    bsz, L, _ = proj.shape
    for window, dil in DIL_PATTERNS:
        assert (L // dil) % Q_TILE == 0 and (window // dil) % 2 == 0
    assert (L // Q_TILE) % DIL_UNROLL == 0 and L % MERGE_ROWS == 0
    qkv_blk0, gate_blk0 = qkv_col0 // HEAD_DIM, gate_col0 // HEAD_DIM

    def in_spec(g, kind):
        return pl.BlockSpec((1, L, HEAD_DIM),
                            lambda b, h: (b, 0, qkv_blk0 + (g * 3 + kind) * DIL_HEADS + h))

    gate_spec = pl.BlockSpec((1, L, HEAD_DIM), lambda b, h: (b, 0, gate_blk0 + h))
    head_spec = pl.BlockSpec((1, L, HEAD_DIM), lambda b, h: (b, 0, h))
    return pl.pallas_call(
        functools.partial(_dil_kernel, L=L),
        out_shape=jax.ShapeDtypeStruct((bsz, L, DIL_WIDTH), BF16),
        grid=(bsz, DIL_HEADS),
        in_specs=[in_spec(g, kind) for g in range(N_DIL) for kind in range(3)] + [gate_spec],
        out_specs=head_spec,
        scratch_shapes=[pltpu.VMEM((N_DIL, L, HEAD_DIM), F32),
                        pltpu.VMEM((N_DIL, L, HEAD_DIM), F32),
                        pltpu.VMEM((3, L, HEAD_DIM), BF16)],
        compiler_params=_params(2, 48),
        name="dilated_attention",
    )(*([proj] * (3 * N_DIL + 1)))


def _out_norm_kernel(a0_ref, a1_ref, w0_ref, w1_ref, res_ref, g_ref, x_ref, h_ref):
    x = (jnp.dot(a0_ref[...], w0_ref[...], preferred_element_type=F32)
         + jnp.dot(a1_ref[...], w1_ref[...], preferred_element_type=F32) + res_ref[...])
    x_ref[...] = x
    ms = jnp.mean(x * x, axis=-1, keepdims=True)
    h_ref[...] = (x * lax.rsqrt(ms + EPS) * g_ref[...]).astype(h_ref.dtype)


def _even_out_norm(a_out, b_out, w_out, res, next_norm, tm=256):
    m, d = res.shape
    ka, kb = a_out.shape[1], b_out.shape[1]
    assert ka == kb and m % tm == 0
    w = w_out.astype(BF16)
    row = lambda k: pl.BlockSpec((tm, k), lambda i: (i, 0))
    wspec = lambda rb: pl.BlockSpec((ka, d), lambda i, rb=rb: (rb, 0),
                                    pipeline_mode=pl.Buffered(1))
    return pl.pallas_call(
        _out_norm_kernel,
        out_shape=[jax.ShapeDtypeStruct((m, d), F32), jax.ShapeDtypeStruct((m, d), BF16)],
        grid=(m // tm,),
        in_specs=[row(ka), row(kb), wspec(0), wspec(1), row(d),
                  pl.BlockSpec((1, d), lambda i: (0, 0))],
        out_specs=[row(d), row(d)],
        compiler_params=_params(1, 52),
        name="even_out_norm",
    )(a_out, b_out, w, w, res, next_norm.reshape(1, d))


def _even_layer(x2d, bsz, L, norm, w_in, lam_re, lam_im, log_step, b_re, b_im, c_re, c_im,
                d_skip, glu_w, glu_b, w_out, tables, h=None, next_norm=None):
    if h is None:
        h = _rmsnorm(x2d, norm, BF16)
    qkv0 = 2 * S5_WIDTH
    proj = _matmul([h], [w_in], w_in.shape[1], F32, epilogue="rope_qk", extra=tables,
                   seq_len=L, qkv_cols=(qkv0, qkv0 + QKV_WIDTH), name="even_in")
    proj3 = proj.reshape(bsz, L, -1)

    b_mat, c_mat, pow_re, pow_im = _s5_discretise(lam_re, lam_im, log_step, b_re, b_im,
                                                  c_re, c_im, L // SUBLANES)
    y = _s5(proj3, b_mat, c_mat, pow_re, pow_im, d_skip)
    a_out = _glu(y.reshape(bsz * L, S5_WIDTH), proj, glu_w, glu_b)

    b_out = _dilated_attention(proj3, qkv0, qkv0 + QKV_WIDTH).reshape(bsz * L, DIL_WIDTH)

    if next_norm is not None:
        return _even_out_norm(a_out, b_out, w_out, x2d, next_norm)
    return _matmul([a_out, b_out], [w_out, w_out], D_MODEL, F32, w_row_blocks=[0, 1],
                   res=x2d, name="even_out"), None


def _odd_prep_kernel(p_ref, qn_ref, kvn_ref, kin_ref, c_ref, sa_ref, sb_ref,
                     cq_ref, ckv_ref, kr_ref, ki_ref, wi_ref):
    def norm(x, g):
        ms = jnp.mean(x * x, axis=-1, keepdims=True)
        return x * lax.rsqrt(ms + EPS) * g

    c, sa, sb = c_ref[...], sa_ref[...], sb_ref[...]
    o = 0
    cq_ref[...] = norm(p_ref[:, o:o + Q_RANK], qn_ref[...]).astype(cq_ref.dtype)
    o += Q_RANK
    ckv_ref[...] = norm(p_ref[:, o:o + KV_RANK], kvn_ref[...]).astype(ckv_ref.dtype)
    o += KV_RANK
    lo = p_ref[:, o:o + LANES]
    hi = p_ref[:, o + LANES:o + 2 * LANES]
    lane = lax.broadcasted_iota(jnp.int32, lo.shape, 1)
    kr_ref[...] = _rope_head(jnp.where(lane < ROT_DIM, lo, 0.0), c, sa, sb)
    k_i = jnp.where(lane < LANES - ROT_DIM, pltpu.roll(lo, LANES - ROT_DIM, 1),
                    pltpu.roll(hi, LANES - ROT_DIM, 1))
    ki_ref[...] = _rope_head(norm(k_i, kin_ref[...]), c, sa, sb).astype(ki_ref.dtype)
    wi_ref[...] = hi * (IDX_HEADS ** -0.5)


def _odd_prep(p_small, q_norm, kv_norm, k_idx_norm, tables, L, tm=256):
    m = p_small.shape[0]
    tps = L // tm
    row = lambda w: pl.BlockSpec((tm, w), lambda i: (i, 0))
    tab = pl.BlockSpec((tm, 128), lambda i: (i % tps, 0))
    vec = lambda w: pl.BlockSpec((1, w), lambda i: (0, 0))
    return pl.pallas_call(
        _odd_prep_kernel,
        out_shape=[jax.ShapeDtypeStruct((m, Q_RANK), BF16),
                   jax.ShapeDtypeStruct((m, KV_RANK), BF16),
                   jax.ShapeDtypeStruct((m, 128), F32),
                   jax.ShapeDtypeStruct((m, 128), BF16),
                   jax.ShapeDtypeStruct((m, 128), F32)],
        grid=(m // tm,),
        in_specs=[row(ODD_SMALL), vec(Q_RANK), vec(KV_RANK), vec(IDX_DIM), tab, tab, tab],
        out_specs=[row(Q_RANK), row(KV_RANK), row(128), row(128), row(128)],
        compiler_params=_params(1, 32),
        name="odd_prep",
    )(p_small, q_norm.reshape(1, -1), kv_norm.reshape(1, -1), k_idx_norm.reshape(1, -1),
      *tables)


def _float_sort_key(s):
    b = pltpu.bitcast(s, jnp.int32)
    return b ^ ((b >> 31) & 0x7FFFFFFF)


def _indexer_kernel(q_ref, k_ref, w_ref, bias_ref, key_ref, jmax_ref, *, L, topk, kc, w_lane0):
    i = pl.program_id(1)
    nch = L // kc
    rows = IDX_ROWS
    n_vis = ((i + 1) * rows + kc - 1) // kc
    qpos = i * rows + lax.broadcasted_iota(jnp.int32, (rows, 1), 0)
    lane = lax.broadcasted_iota(jnp.int32, (rows, kc), 1)
    lane_t = lax.broadcasted_iota(jnp.int32, (Q_TILE, kc), 1)
    qpos_t = lax.broadcasted_iota(jnp.int32, (Q_TILE, 1), 0)

    def chunk(c, _):
        k0 = pl.multiple_of(c * kc, kc)
        kblk = k_ref[0, pl.ds(k0, kc), :]
        for t in range(rows // Q_TILE):
            rs = slice(t * Q_TILE, (t + 1) * Q_TILE)
            acc = jnp.zeros((Q_TILE, kc), F32)
            for h in range(IDX_HEADS):
                qh = q_ref[0, rs, h * IDX_DIM:(h + 1) * IDX_DIM]
                s = lax.dot_general(qh, kblk, (((1,), (1,)), ((), ())),
                                    preferred_element_type=F32)
                acc = acc + w_ref[0, rs, w_lane0 + h:w_lane0 + h + 1] * jnp.maximum(s, 0.0)
            acc = jnp.where(acc == 0.0, 0.0, acc)
            visible = lane_t + k0 <= qpos_t + (i * rows + t * Q_TILE)
            key_ref[c, rs, :] = jnp.where(visible, _float_sort_key(acc), INT_MIN)
        return 0

    keep_all = (i + 1) * rows <= topk
    lax.fori_loop(0, jnp.where(keep_all, 0, n_vis), chunk, 0)

    @pl.when(keep_all)
    def _():
        for c in range(nch):
            bias = jnp.where(lane + c * kc <= qpos, 0.0, MASK_NEG)
            for t in range(rows // Q_TILE):
                bias_ref[0, t, c] = bias[t * Q_TILE:(t + 1) * Q_TILE].astype(bias_ref.dtype)

    def select(nv):
        def count(pred):
            tot = jnp.zeros((rows, LANES), F32)
            for c in range(nv):
                hit = jnp.where(pred(key_ref[c], c), 1.0, 0.0)
                for j in range(kc // LANES):
                    tot = tot + hit[:, j * LANES:(j + 1) * LANES]
            return jnp.sum(tot, axis=1, keepdims=True)

        def tbit(b, t_u):
            cand_u = t_u | jnp.left_shift(jnp.int32(1), 31 - b)
            cand_s = cand_u ^ INT_MIN
            cnt = count(lambda key, c: key >= cand_s)
            return jnp.where(cnt >= topk, cand_u, t_u)

        t_u = lax.fori_loop(0, 32, tbit, jnp.zeros((rows, 1), jnp.int32))
        thr = t_u ^ INT_MIN

        need = topk - count(lambda key, c: key > thr)
        n_eq = count(lambda key, c: key == thr)
        jbits = max(1, (L - 1).bit_length())
        jmax_ref[...] = jnp.full((rows, 1), (1 << jbits) - 1, jnp.int32)

        unsettled = jnp.where(thr == INT_MIN, 0.0, jnp.abs(n_eq - need))

        @pl.when(jnp.max(unsettled) > 0.0)
        def _():
            def jbit(b, j):
                cand = j | jnp.left_shift(jnp.int32(1), jbits - 1 - b)
                cnt = count(lambda key, c: jnp.where(lane + c * kc < cand, key, thr - 1) == thr)
                return jnp.where(cnt < need, cand, j)

            jmax_ref[...] = lax.fori_loop(0, jbits, jbit, jnp.zeros((rows, 1), jnp.int32))

        jlim = jnp.minimum(jmax_ref[...], qpos)
        for c in range(nv):
            key = key_ref[c]
            tie = jnp.where(lane + c * kc <= jlim, 0.0, MASK_NEG)
            bias = jnp.where(key > thr, 0.0, jnp.where(key == thr, tie, MASK_NEG))
            for t in range(rows // Q_TILE):
                bias_ref[0, t, c] = bias[t * Q_TILE:(t + 1) * Q_TILE].astype(bias_ref.dtype)
        for c in range(nv, nch):
            for t in range(rows // Q_TILE):
                bias_ref[0, t, c] = jnp.full((Q_TILE, kc), MASK_NEG, bias_ref.dtype)

    for nv in range(1, nch + 1):
        pl.when(jnp.logical_and(jnp.logical_not(keep_all), n_vis == nv))(
            functools.partial(select, nv))


def _indexer(q_idx, k_idx, w_idx, topk, kc, w_lane0):
    bsz, L, _ = q_idx.shape
    nq, nch = L // Q_TILE, L // kc
    rows = IDX_ROWS
    assert L % rows == 0 and rows % Q_TILE == 0
    return pl.pallas_call(
        functools.partial(_indexer_kernel, L=L, topk=topk, kc=kc, w_lane0=w_lane0),
        out_shape=jax.ShapeDtypeStruct((bsz, nq, nch, Q_TILE, kc), BF16),
        grid=(bsz, L // rows),
        in_specs=[pl.BlockSpec((1, rows, IDX_HEADS * IDX_DIM), lambda b, i: (b, i, 0)),
                  pl.BlockSpec((1, L, IDX_DIM), lambda b, i: (b, 0, 0)),
                  pl.BlockSpec((1, rows, LANES), lambda b, i: (b, i, 0))],
        out_specs=pl.BlockSpec((1, rows // Q_TILE, nch, Q_TILE, kc),
                               lambda b, i: (b, i, 0, 0, 0)),
        scratch_shapes=[pltpu.VMEM((nch, rows, kc), jnp.int32),
                        pltpu.VMEM((rows, 1), jnp.int32)],
        compiler_params=_params(2, 32),
        name="dsa_indexer",
    )(q_idx, k_idx, w_idx)


def _mla_kernel(q_ref, ckv_ref, kr_ref, wk_ref, wv_ref, bias_ref, g_ref, o_ref,
                k_ref, v_ref, s_ref, mx_ref, acc_ref, *, kc, nh):
    i = pl.program_id(2)
    rows = ATT_ROWS
    n_vis = ((i + 1) * rows + kc - 1) // kc
    slabs = kc // LANES

    @pl.when(i == 0)
    def _():
        ckv = ckv_ref[0]
        kr = kr_ref[0]
        k_all = jnp.dot(ckv, wk_ref[...].astype(BF16), preferred_element_type=F32)
        for h in range(nh):
            hs = slice(h * HEAD_DIM, (h + 1) * HEAD_DIM)
            k_ref[:, hs] = (k_all[:, hs] + kr).astype(k_ref.dtype)
        v_ref[...] = jnp.dot(ckv, wv_ref[...].astype(BF16),
                             preferred_element_type=F32).astype(v_ref.dtype)

    def fold(x, op):
        t = x[:, 0:LANES]
        for j in range(1, slabs):
            t = op(t, x[:, j * LANES:(j + 1) * LANES])
        return t

    mx_ref[...] = jnp.full(mx_ref.shape, MASK_NEG, F32)

    def scores(c, _):
        k0 = pl.multiple_of(c * kc, kc)
        bias = jnp.concatenate([bias_ref[0, t, c] for t in range(rows // Q_TILE)],
                               axis=0).astype(F32)
        raw = [lax.dot_general(q_ref[0, :, h * HEAD_DIM:(h + 1) * HEAD_DIM],
                               k_ref[pl.ds(k0, kc), h * HEAD_DIM:(h + 1) * HEAD_DIM],
                               (((1,), (1,)), ((), ())), preferred_element_type=F32)
               for h in range(nh)]
        for h in range(nh):
            s = raw[h] + bias
            s_ref[c, h] = s
            mx_ref[h] = jnp.maximum(mx_ref[h], fold(s, jnp.maximum))
        return 0

    lax.fori_loop(0, n_vis, scores, 0)

    for h in range(nh):
        m = jnp.max(mx_ref[h], axis=1, keepdims=True)
        mx_ref[h] = jnp.broadcast_to(m, (rows, LANES))
    acc_ref[...] = jnp.zeros(acc_ref.shape, F32)

    ones = jnp.ones((kc, LANES), BF16)

    def weights(c, _):
        k0 = pl.multiple_of(c * kc, kc)
        for h in range(nh):
            hs = slice(h * HEAD_DIM, (h + 1) * HEAD_DIM)
            m = mx_ref[h]
            s = s_ref[c, h]
            p = jnp.concatenate(
                [jnp.exp2(s[:, j * LANES:(j + 1) * LANES] - m) for j in range(slabs)], axis=1)
            v1 = jnp.concatenate([v_ref[pl.ds(k0, kc), hs], ones], axis=1)
            acc_ref[h] += jnp.dot(p.astype(BF16), v1, preferred_element_type=F32)
        return 0

    lax.fori_loop(0, n_vis, weights, 0)

    for h in range(nh):
        hs = slice(h * HEAD_DIM, (h + 1) * HEAD_DIM)
        g = g_ref[0, :, hs].astype(F32)
        den = acc_ref[h, :, HEAD_DIM:HEAD_DIM + LANES] * (1.0 + jnp.exp(-g))
        o_ref[0, :, hs] = (acc_ref[h, :, 0:HEAD_DIM] * g / den).astype(o_ref.dtype)


def _masked_attention(q, c_kv, k_rope, w_k, w_v, bias, gate, kc, nh=ATT_HEADS):
    bsz, L, _ = q.shape
    rows, nch = ATT_ROWS, L // kc
    assert L % rows == 0 and rows % Q_TILE == 0
    w = nh * HEAD_DIM
    qspec = pl.BlockSpec((1, rows, w), lambda b, g, i: (b, i, g))
    wspec = pl.BlockSpec((KV_RANK, w), lambda b, g, i: (0, g))
    return pl.pallas_call(
        functools.partial(_mla_kernel, kc=kc, nh=nh),
        out_shape=jax.ShapeDtypeStruct((bsz, L, C_WIDTH), BF16),
        grid=(bsz, MLA_HEADS // nh, L // rows),
        in_specs=[qspec,
                  pl.BlockSpec((1, L, KV_RANK), lambda b, g, i: (b, 0, 0)),
                  pl.BlockSpec((1, L, HEAD_DIM), lambda b, g, i: (b, 0, 0)),
                  wspec, wspec,
                  pl.BlockSpec((1, rows // Q_TILE, nch, Q_TILE, kc),
                               lambda b, g, i: (b, i, 0, 0, 0)),
                  qspec],
        out_specs=qspec,
        scratch_shapes=[pltpu.VMEM((L, w), BF16),
                        pltpu.VMEM((L, w), BF16),
                        pltpu.VMEM((nch, nh, rows, kc), F32),
                        pltpu.VMEM((nh, rows, LANES), F32),
                        pltpu.VMEM((nh, rows, HEAD_DIM + LANES), F32)],
        compiler_params=_params(3, 48),
        name="dsa_attention",
    )(q, c_kv, k_rope, w_k, w_v, bias, gate)


def _odd_layer(x2d, bsz, L, norm, w_in, q_norm, kv_norm, k_idx_norm, w_uq, w_uk, w_uv, w_iq,
               w_out, tables, h=None):
    m = bsz * L
    if h is None:
        h = _rmsnorm(x2d, norm, BF16)

    w_in_t = w_in.T
    p_small = _matmul([h], [w_in_t], ODD_SMALL, F32, w_transposed=True, tn=256,
                      name="odd_in_small")
    gate = _matmul([h], [w_in_t], C_WIDTH, BF16, w_transposed=True, w_col0=ODD_GATE0,
                   name="odd_in_gate")
    c_q, c_kv, k_rope, k_idx, w_idx = _odd_prep(p_small, q_norm, kv_norm, k_idx_norm, tables, L)

    q = _matmul([c_q], [w_uq], C_WIDTH, BF16, epilogue="rope", extra=tables, seq_len=L,
                scale=LOG2E * HEAD_DIM ** -0.5, tn=2 * ROPE_SUBTILE, name="odd_q_up")
    q_idx = _matmul([c_q], [w_iq], IDX_HEADS * IDX_DIM, BF16, epilogue="rope", extra=tables,
                    seq_len=L, scale=IDX_DIM ** -0.5, tn=2 * ROPE_SUBTILE, name="odd_q_idx")

    w_k = jnp.pad(w_uk, ((0, 0), (0, 0), (ROT_DIM, 0))).reshape(KV_RANK, C_WIDTH)
    w_v = w_uv.reshape(KV_RANK, C_WIDTH)

    topk = min(IDX_TOPK_MAX, L // 4)
    kc = min(512, L)
    bias = _indexer(q_idx.reshape(bsz, L, -1), k_idx.reshape(bsz, L, IDX_DIM),
                    w_idx.reshape(bsz, L, LANES), topk, kc, IDX_W_LANE0)
    o = _masked_attention(q.reshape(bsz, L, C_WIDTH), c_kv.reshape(bsz, L, KV_RANK),
                          k_rope.reshape(bsz, L, HEAD_DIM), w_k, w_v, bias,
                          gate.reshape(bsz, L, C_WIDTH), kc)
    return _matmul([o.reshape(m, C_WIDTH)], [w_out], D_MODEL, F32, res=x2d, name="odd_out")


def kernel(x, even_norm, even_w_in, s5_lambda_re, s5_lambda_im, s5_log_step, s5_b_re, s5_b_im,
           s5_c_re, s5_c_im, s5_d, s5_glu_w, s5_glu_b, even_w_out, odd_norm, odd_w_in,
           mla_q_norm, mla_kv_norm, idx_k_norm, mla_w_uq, mla_w_uk, mla_w_uv, idx_w_q,
           odd_w_out, final_norm):
    bsz, L, d = x.shape
    depth = even_norm.shape[0] + odd_norm.shape[0]
    tables = _rope_tables(L)
    x2d = x.reshape(bsz * L, d)
    h = None
    for layer in range(depth):
        i = layer // 2
        if layer % 2 == 0:
            next_norm = odd_norm[i] if layer + 1 < depth else None
            x2d, h = _even_layer(x2d, bsz, L, even_norm[i], even_w_in[i], s5_lambda_re[i],
                                 s5_lambda_im[i], s5_log_step[i], s5_b_re[i], s5_b_im[i],
                                 s5_c_re[i], s5_c_im[i], s5_d[i], s5_glu_w[i], s5_glu_b[i],
                                 even_w_out[i], tables, h=h, next_norm=next_norm)
        else:
            x2d = _odd_layer(x2d, bsz, L, odd_norm[i], odd_w_in[i], mla_q_norm[i],
                             mla_kv_norm[i], idx_k_norm[i], mla_w_uq[i], mla_w_uk[i],
                             mla_w_uv[i], idx_w_q[i], odd_w_out[i], tables, h=h)
            h = None
    return _rmsnorm(x2d, final_norm, F32).reshape(bsz, L, d)
```
